```python
import math
import jax, jax.numpy as jnp
from jax import lax
import numpy as np

D_MODEL = 2048
BATCH = 8
SEQ = 4096
DEPTH = 2

HEAD_DIM = 128
UNIT = D_MODEL // (4 * HEAD_DIM)
GMLP_GROUPS = UNIT
GMLP_CHUNK = 128
GMLP_WIDTH = GMLP_GROUPS * HEAD_DIM
DIFF_HEADS = UNIT
DIFF_QK_WIDTH = DIFF_HEADS * 2 * HEAD_DIM
DIFF_V_DIM = 2 * HEAD_DIM
DIFF_WIDTH = DIFF_HEADS * DIFF_V_DIM
DIFF_Q_BLOCK = 128
CONV_GROUPS = UNIT
CONV_WIDTH = CONV_GROUPS * HEAD_DIM
CONV_K = 3
MIX_WIDTH = GMLP_WIDTH + DIFF_WIDTH + CONV_WIDTH
IN_PROJ_WIDTH = 2 * GMLP_WIDTH + 2 * DIFF_QK_WIDTH + DIFF_WIDTH + 3 * CONV_WIDTH
SPLIT_IDX = [2 * GMLP_WIDTH,
             2 * GMLP_WIDTH + DIFF_QK_WIDTH,
             2 * GMLP_WIDTH + 2 * DIFF_QK_WIDTH,
             2 * GMLP_WIDTH + 2 * DIFF_QK_WIDTH + DIFF_WIDTH]
D_FF = ((8 * D_MODEL + 3 * 256 - 1) // (3 * 256)) * 256
ROPE_THETA = 10000.0
RMS_EPS = 1e-6
LN_EPS = 1e-5

kernel_name = "hymba_gmlp_diffattn_shortconv_block"


def rms_norm(x, g):
    xf = x.astype(jnp.float32)
    y = xf * lax.rsqrt(jnp.mean(xf * xf, axis=-1, keepdims=True) + RMS_EPS)
    return (y * g.astype(jnp.float32)).astype(x.dtype)


def group_rms(x):
    xf = x.astype(jnp.float32)
    return (xf * lax.rsqrt(jnp.mean(xf * xf, axis=-1, keepdims=True) + RMS_EPS)).astype(x.dtype)


def rope(x, cos, sin):
    xf = x.astype(jnp.float32)
    x1, x2 = jnp.split(xf, 2, axis=-1)
    out = jnp.concatenate([x1 * cos - x2 * sin, x2 * cos + x1 * sin], axis=-1)
    return out.astype(x.dtype)


def gmlp_mixer(z, ln_g, ln_b, ws, bs):
    b_, s_, _ = z.shape
    z = jax.nn.gelu(z, approximate=False).reshape(b_, s_, 2, GMLP_GROUPS, HEAD_DIM)
    u, v = z[:, :, 0], z[:, :, 1]
    vf = v.astype(jnp.float32)
    mu = jnp.mean(vf, axis=-1, keepdims=True)
    var = jnp.mean(jnp.square(vf - mu), axis=-1, keepdims=True)
    vn = ((vf - mu) * lax.rsqrt(var + LN_EPS) * ln_g.astype(jnp.float32)
          + ln_b.astype(jnp.float32)).astype(v.dtype)
    vn = vn.reshape(b_, s_ // GMLP_CHUNK, GMLP_CHUNK, GMLP_GROUPS, HEAD_DIM)
    causal = jnp.tril(jnp.ones((GMLP_CHUNK, GMLP_CHUNK), dtype=bool))
    w = jnp.where(causal[None], ws, jnp.zeros_like(ws))
    mixed = jnp.einsum('gts,bnsgc->bntgc', w, vn) + jnp.transpose(bs)[:, :, None]
    return u * mixed.reshape(b_, s_, GMLP_GROUPS, HEAD_DIM)


def diff_attention(q, k, v, lam, cos, sin):
    b_, s_, h_, _, d_ = q.shape
    nb = s_ // DIFF_Q_BLOCK
    q = rope(q, cos, sin) * (1.0 / math.sqrt(d_))
    k = rope(k, cos, sin)
    qb = q.reshape(b_, nb, DIFF_Q_BLOCK, h_, 2, d_).transpose(1, 0, 3, 4, 2, 5)
    kt = k.transpose(0, 2, 3, 1, 4)
    vt = v.transpose(0, 2, 1, 3)
    key_pos = jnp.arange(s_)

    def block(args):
        q_blk, i = args
        s = jnp.einsum('bhmqd,bhmkd->bhmqk', q_blk, kt).astype(jnp.float32)
        q_pos = i * DIFF_Q_BLOCK + jnp.arange(DIFF_Q_BLOCK)
        causal = key_pos[None, :] <= q_pos[:, None]
        s = jnp.where(causal, s, -jnp.inf)
        p = jax.nn.softmax(s, axis=-1)
        a = p[:, :, 0] - lam * p[:, :, 1]
        return jnp.einsum('bhqk,bhkd->bhqd', a.astype(vt.dtype), vt)

    o = lax.map(block, (qb, jnp.arange(nb)))
    return o.transpose(1, 0, 3, 2, 4).reshape(b_, s_, h_, 2 * d_)


def short_conv_mixer(bg, cg, hc, conv_w):
    xh = cg * hc
    y = lax.conv_general_dilated(
        xh, conv_w[:, None, :].astype(xh.dtype), window_strides=(1,),
        padding=[(CONV_K - 1, 0)], dimension_numbers=('NWC', 'WIO', 'NWC'),
        feature_group_count=CONV_WIDTH)
    return bg * y


def setup_inputs(seed: int = 0) -> dict:
    key = jax.random.key(seed)
    ks = jax.random.split(key, 20)
    f32 = jnp.float32
    nrm = lambda k, shape, scale: jax.random.normal(k, shape, f32) * scale
    return {
        "x": nrm(ks[0], (BATCH, SEQ, D_MODEL), 1.0),
        "positions": jnp.broadcast_to(jnp.arange(SEQ, dtype=jnp.int32), (BATCH, SEQ)),
        "attn_norm": 1.0 + nrm(ks[1], (DEPTH, D_MODEL), 0.02),
        "w_in": nrm(ks[2], (DEPTH, D_MODEL, IN_PROJ_WIDTH), D_MODEL ** -0.5),
        "gmlp_ln_g": 1.0 + nrm(ks[3], (DEPTH, GMLP_GROUPS, HEAD_DIM), 0.02),
        "gmlp_ln_b": nrm(ks[4], (DEPTH, GMLP_GROUPS, HEAD_DIM), 0.02),
        "gmlp_ws": nrm(ks[5], (DEPTH, GMLP_GROUPS, GMLP_CHUNK, GMLP_CHUNK), GMLP_CHUNK ** -0.5),
        "gmlp_bs": 1.0 + nrm(ks[6], (DEPTH, GMLP_GROUPS, GMLP_CHUNK), 0.02),
        "lambda_q1": nrm(ks[7], (DEPTH, HEAD_DIM), 0.1),
        "lambda_k1": nrm(ks[8], (DEPTH, HEAD_DIM), 0.1),
        "lambda_q2": nrm(ks[9], (DEPTH, HEAD_DIM), 0.1),
        "lambda_k2": nrm(ks[10], (DEPTH, HEAD_DIM), 0.1),
        "conv_w": nrm(ks[11], (DEPTH, CONV_K, CONV_WIDTH), CONV_K ** -0.5),
        "mix_norm": 1.0 + nrm(ks[12], (DEPTH, MIX_WIDTH), 0.02),
        "w_out": nrm(ks[13], (DEPTH, MIX_WIDTH, D_MODEL), MIX_WIDTH ** -0.5),
        "ffn_norm": 1.0 + nrm(ks[14], (DEPTH, D_MODEL), 0.02),
        "w_gate": nrm(ks[15], (DEPTH, D_MODEL, D_FF), D_MODEL ** -0.5),
        "w_up": nrm(ks[16], (DEPTH, D_MODEL, D_FF), D_MODEL ** -0.5),
        "w_down": nrm(ks[17], (DEPTH, D_FF, D_MODEL), D_FF ** -0.5),
        "final_norm": 1.0 + nrm(ks[18], (D_MODEL,), 0.02),
    }


def reference(x, positions, attn_norm, w_in, gmlp_ln_g, gmlp_ln_b, gmlp_ws, gmlp_bs,
              lambda_q1, lambda_k1, lambda_q2, lambda_k2, conv_w, mix_norm, w_out,
              ffn_norm, w_gate, w_up, w_down, final_norm):
    b_, s_, _ = x.shape
    inv_freq = 1.0 / (ROPE_THETA ** (jnp.arange(0, HEAD_DIM, 2, dtype=jnp.float32) / HEAD_DIM))
    ang = positions.astype(jnp.float32)[..., None] * inv_freq
    cos = jnp.cos(ang)[:, :, None, None, :]
    sin = jnp.sin(ang)[:, :, None, None, :]

    for l in range(DEPTH):
        h = rms_norm(x, attn_norm[l])
        z = h @ w_in[l]
        za, zq, zk, zv, zc = jnp.split(z, SPLIT_IDX, axis=-1)

        ya = gmlp_mixer(za, gmlp_ln_g[l], gmlp_ln_b[l], gmlp_ws[l], gmlp_bs[l])

        lam_init = 0.8 - 0.6 * math.exp(-0.3 * l)
        lam = (jnp.exp(jnp.sum(lambda_q1[l].astype(jnp.float32) * lambda_k1[l].astype(jnp.float32)))
               - jnp.exp(jnp.sum(lambda_q2[l].astype(jnp.float32) * lambda_k2[l].astype(jnp.float32)))
               + lam_init)
        yb = diff_attention(zq.reshape(b_, s_, DIFF_HEADS, 2, HEAD_DIM),
                            zk.reshape(b_, s_, DIFF_HEADS, 2, HEAD_DIM),
                            zv.reshape(b_, s_, DIFF_HEADS, DIFF_V_DIM), lam, cos, sin)
        yb = group_rms(yb) * (1.0 - lam_init)

        bg, cg, hc = jnp.split(zc, 3, axis=-1)
        yc = short_conv_mixer(bg, cg, hc, conv_w[l]).reshape(b_, s_, CONV_GROUPS, HEAD_DIM)

        mix = jnp.concatenate([group_rms(ya).reshape(b_, s_, GMLP_WIDTH),
                               yb.reshape(b_, s_, DIFF_WIDTH),
                               group_rms(yc).reshape(b_, s_, CONV_WIDTH)], axis=-1)
        mix = mix * mix_norm[l].astype(mix.dtype)
        x = x + mix @ w_out[l]

        h = rms_norm(x, ffn_norm[l])
        x = x + (jax.nn.silu(h @ w_gate[l]) * (h @ w_up[l])) @ w_down[l]

    return rms_norm(x, final_norm)
```

```python
import functools
import math

import jax
import jax.numpy as jnp
from jax import lax
from jax.experimental import pallas as pl
from jax.experimental.pallas import tpu as pltpu

F32 = jnp.float32
BF16 = jnp.bfloat16

HEAD_DIM = 128
N_GROUPS = 4
GMLP_WIDTH = N_GROUPS * HEAD_DIM
GMLP_CHUNK = 128
DIFF_QK_WIDTH = N_GROUPS * 2 * HEAD_DIM
DIFF_V_DIM = 2 * HEAD_DIM
DIFF_WIDTH = N_GROUPS * DIFF_V_DIM
CONV_WIDTH = N_GROUPS * HEAD_DIM
CONV_K = 3
ROPE_THETA = 10000.0
RMS_EPS = 1e-6
LN_EPS = 1e-5

Q_OFF = 2 * GMLP_WIDTH
K_OFF = Q_OFF + DIFF_QK_WIDTH
V_OFF = K_OFF + DIFF_QK_WIDTH
C_OFF = V_OFF + DIFF_WIDTH

VMEM_LIMIT_BYTES = 56 * 1024 * 1024

IN_PROJ_TM = 1024
IN_PROJ_TN = 512
LOCAL_ROWS = 512
ATTN_TQ = 512
OUT_PROJ_TM = 512
FFN_TM = 1024
FFN_TF = 256
CONV_HALO_ROWS = 8


def _params(*semantics):
    return pltpu.CompilerParams(dimension_semantics=semantics, vmem_limit_bytes=VMEM_LIMIT_BYTES)


def _rms_rows(x, gain):
    return x * lax.rsqrt(jnp.mean(x * x, axis=-1, keepdims=True) + RMS_EPS) * gain


def _group_rms(x):
    return x * lax.rsqrt(jnp.mean(x * x, axis=-1, keepdims=True) + RMS_EPS)


def _in_proj_kernel(x_ref, g_ref, w_ref, cos_ref, sin_ref, z_ref, h_ref, *, tn):
    j = pl.program_id(1)

    @pl.when(j == 0)
    def _():
        h_ref[...] = _rms_rows(x_ref[...], g_ref[...]).astype(BF16)

    acc = jnp.dot(h_ref[...], w_ref[...], preferred_element_type=F32)
    q_lo, k_lo, v_lo = Q_OFF // tn, K_OFF // tn, V_OFF // tn

    def rope(scale):
        cosf, sinf = cos_ref[...], sin_ref[...]
        for c in range(tn // HEAD_DIM):
            xc = acc[:, c * HEAD_DIM:(c + 1) * HEAD_DIM]
            r = xc * cosf + pltpu.roll(xc, HEAD_DIM // 2, 1) * sinf
            if scale != 1.0:
                r = r * scale
            z_ref[:, c * HEAD_DIM:(c + 1) * HEAD_DIM] = r.astype(z_ref.dtype)

    @pl.when((j >= q_lo) & (j < k_lo))
    def _():
        rope(1.0 / math.sqrt(HEAD_DIM))

    @pl.when((j >= k_lo) & (j < v_lo))
    def _():
        rope(1.0)

    @pl.when((j < q_lo) | (j >= v_lo))
    def _():
        z_ref[...] = acc.astype(z_ref.dtype)


def _in_proj(x2, gain, w, cosf, sinf):
    m, d = x2.shape
    n = w.shape[1]
    tm, tn = IN_PROJ_TM, IN_PROJ_TN
    assert m % tm == 0 and n % tn == 0 and Q_OFF % tn == 0
    return pl.pallas_call(
        functools.partial(_in_proj_kernel, tn=tn),
        grid=(m // tm, n // tn),
        in_specs=[
            pl.BlockSpec((tm, d), lambda i, j: (i, 0)),
            pl.BlockSpec((1, d), lambda i, j: (0, 0)),
            pl.BlockSpec((d, tn), lambda i, j: (0, j)),
            pl.BlockSpec((tm, HEAD_DIM), lambda i, j: (i, 0)),
            pl.BlockSpec((tm, HEAD_DIM), lambda i, j: (i, 0)),
        ],
        out_specs=pl.BlockSpec((tm, tn), lambda i, j: (i, j)),
        out_shape=jax.ShapeDtypeStruct((m, n), BF16),
        scratch_shapes=[pltpu.VMEM((tm, d), BF16)],
        compiler_params=_params("parallel", "arbitrary"),
        name="in_proj",
    )(x2, gain, w, cosf, sinf)


def _gelu(x):
    return 0.5 * x * (1.0 + lax.erf(x * math.sqrt(0.5)))


def _local_kernel(za_ref, bg_ref, cg_ref, hc_ref, cgh_ref, hch_ref, lng_ref, lnb_ref, ws_ref, bs_ref,
                  cw_ref, ga_ref, gc_ref, oa_ref, oc_ref, *, rows):
    i = pl.program_id(1)

    t_idx = lax.broadcasted_iota(jnp.int32, (GMLP_CHUNK, GMLP_CHUNK), 0)
    s_idx = lax.broadcasted_iota(jnp.int32, (GMLP_CHUNK, GMLP_CHUNK), 1)
    for g in range(N_GROUPS):
        w = jnp.where(s_idx <= t_idx, ws_ref[g], 0.0).astype(BF16)
        bias = bs_ref[g]
        lo = g * HEAD_DIM
        for c in range(rows // GMLP_CHUNK):
            r0 = c * GMLP_CHUNK
            u = _gelu(za_ref[0, r0:r0 + GMLP_CHUNK, lo:lo + HEAD_DIM].astype(F32))
            v = _gelu(za_ref[0, r0:r0 + GMLP_CHUNK, GMLP_WIDTH + lo:GMLP_WIDTH + lo + HEAD_DIM].astype(F32))
            mu = jnp.mean(v, axis=-1, keepdims=True)
            vc = v - mu
            var = jnp.mean(vc * vc, axis=-1, keepdims=True)
            vn = vc * lax.rsqrt(var + LN_EPS) * lng_ref[g:g + 1, :] + lnb_ref[g:g + 1, :]
            mixed = jnp.dot(w, vn.astype(BF16), preferred_element_type=F32) + bias
            ya = _group_rms(u * mixed) * ga_ref[:, lo:lo + HEAD_DIM]
            oa_ref[0, r0:r0 + GMLP_CHUNK, lo:lo + HEAD_DIM] = ya.astype(oa_ref.dtype)

    xh = cg_ref[0].astype(F32) * hc_ref[0].astype(F32)
    halo = cgh_ref[0].astype(F32) * hch_ref[0].astype(F32)
    halo = jnp.where(i == 0, 0.0, halo)
    row = lax.broadcasted_iota(jnp.int32, xh.shape, 0)
    prev1 = jnp.where(row == 0, halo[CONV_HALO_ROWS - 1:CONV_HALO_ROWS, :], pltpu.roll(xh, 1, 0))
    prev2 = jnp.where(row == 0, halo[CONV_HALO_ROWS - 2:CONV_HALO_ROWS - 1, :],
                      jnp.where(row == 1, halo[CONV_HALO_ROWS - 1:CONV_HALO_ROWS, :], pltpu.roll(xh, 2, 0)))
    y = cw_ref[0:1, :] * prev2 + cw_ref[1:2, :] * prev1 + cw_ref[2:3, :] * xh
    yc = bg_ref[0].astype(F32) * y
    for g in range(N_GROUPS):
        lo = g * HEAD_DIM
        oc_ref[0, :, lo:lo + HEAD_DIM] = (_group_rms(yc[:, lo:lo + HEAD_DIM])
                                          * gc_ref[:, lo:lo + HEAD_DIM]).astype(oc_ref.dtype)


def _local_mixers(z3, ln_g, ln_b, ws, bs, conv_w, gain_a, gain_c):
    b, s, _ = z3.shape
    rows = LOCAL_ROWS
    assert s % rows == 0 and rows % GMLP_CHUNK == 0 and C_OFF % CONV_WIDTH == 0
    cb = C_OFF // CONV_WIDTH
    hb = rows // CONV_HALO_ROWS
    full = lambda shape: pl.BlockSpec(shape, lambda bi, i: (0,) * len(shape))
    halo_map = lambda col: (lambda bi, i: (bi, jnp.maximum(i * hb - 1, 0), col))
    return pl.pallas_call(
        functools.partial(_local_kernel, rows=rows),
        grid=(b, s // rows),
        in_specs=[
            pl.BlockSpec((1, rows, 2 * GMLP_WIDTH), lambda bi, i: (bi, i, 0)),
            pl.BlockSpec((1, rows, CONV_WIDTH), lambda bi, i: (bi, i, cb)),
            pl.BlockSpec((1, rows, CONV_WIDTH), lambda bi, i: (bi, i, cb + 1)),
            pl.BlockSpec((1, rows, CONV_WIDTH), lambda bi, i: (bi, i, cb + 2)),
            pl.BlockSpec((1, CONV_HALO_ROWS, CONV_WIDTH), halo_map(cb + 1)),
            pl.BlockSpec((1, CONV_HALO_ROWS, CONV_WIDTH), halo_map(cb + 2)),
            full((N_GROUPS, HEAD_DIM)),
            full((N_GROUPS, HEAD_DIM)),
            full((N_GROUPS, GMLP_CHUNK, GMLP_CHUNK)),
            full((N_GROUPS, GMLP_CHUNK, 1)),
            full((CONV_K, CONV_WIDTH)),
            full((1, GMLP_WIDTH)),
            full((1, CONV_WIDTH)),
        ],
        out_specs=[
            pl.BlockSpec((1, rows, GMLP_WIDTH), lambda bi, i: (bi, i, 0)),
            pl.BlockSpec((1, rows, CONV_WIDTH), lambda bi, i: (bi, i, 0)),
        ],
        out_shape=[jax.ShapeDtypeStruct((b, s, GMLP_WIDTH), BF16),
                   jax.ShapeDtypeStruct((b, s, CONV_WIDTH), BF16)],
        compiler_params=_params("parallel", "arbitrary"),
        name="local_mixers",
    )(z3, z3, z3, z3, z3, z3, ln_g, ln_b, ws, bs, conv_w, gain_a, gain_c)


def _attn_kernel(lq1_ref, lk1_ref, lq2_ref, lk2_ref, q_ref, k_ref, v_ref, gain_ref, o_ref, *, tq, lam_init):
    qi = pl.program_id(2)
    lam = (jnp.exp(jnp.sum(lq1_ref[...] * lk1_ref[...], axis=-1, keepdims=True))
           - jnp.exp(jnp.sum(lq2_ref[...] * lk2_ref[...], axis=-1, keepdims=True)) + lam_init)

    q = q_ref[0]
    zero = jnp.zeros((tq, HEAD_DIM), q.dtype)
    qs = jnp.concatenate([jnp.concatenate([q[:, :HEAD_DIM], zero], axis=1),
                          jnp.concatenate([zero, q[:, HEAD_DIM:]], axis=1)], axis=0)

    def scores(kb):
        k = k_ref[0, pl.ds(pl.multiple_of(kb * tq, tq), tq), :]
        return lax.dot_general(qs, k, (((1,), (1,)), ((), ())), preferred_element_type=F32)

    def update(kb, s, carry):
        m, l, acc = carry
        m_new = jnp.maximum(m, jnp.max(s, axis=-1, keepdims=True))
        alpha = jnp.exp(m - m_new)
        p = jnp.exp(s - m_new)
        v = v_ref[0, pl.ds(pl.multiple_of(kb * tq, tq), tq), :]
        l = alpha * l + jnp.sum(p, axis=-1, keepdims=True)
        acc = alpha * acc + jnp.dot(p.astype(BF16), v, preferred_element_type=F32)
        return m_new, l, acc

    init = (jnp.full((2 * tq, 1), -jnp.inf, F32), jnp.zeros((2 * tq, 1), F32),
            jnp.zeros((2 * tq, DIFF_V_DIM), F32))
    carry = lax.fori_loop(0, qi, lambda kb, c: update(kb, scores(kb), c), init)

    row = lax.broadcasted_iota(jnp.int32, (2 * tq, tq), 0)
    col = lax.broadcasted_iota(jnp.int32, (2 * tq, tq), 1)
    row = jnp.where(row >= tq, row - tq, row)
    s = jnp.where(col <= row, scores(qi), -jnp.inf)
    m, l, acc = update(qi, s, carry)

    o = acc / l
    o = o[:tq] - lam * o[tq:]
    o = _group_rms(o) * (1.0 - lam_init) * gain_ref[0]
    o_ref[0] = o.astype(o_ref.dtype)


def _diff_attention(z3, lq1, lk1, lq2, lk2, gain_b, lam_init):
    b, s, _ = z3.shape
    tq = ATTN_TQ
    assert s % tq == 0
    qb, kb, vb = Q_OFF // DIFF_V_DIM, K_OFF // DIFF_V_DIM, V_OFF // DIFF_V_DIM
    vec = pl.BlockSpec((1, HEAD_DIM), lambda bi, h, i: (0, 0))
    return pl.pallas_call(
        functools.partial(_attn_kernel, tq=tq, lam_init=lam_init),
        grid=(b, N_GROUPS, s // tq),
        in_specs=[
            vec, vec, vec, vec,
            pl.BlockSpec((1, tq, DIFF_V_DIM), lambda bi, h, i: (bi, i, qb + h)),
            pl.BlockSpec((1, s, DIFF_V_DIM), lambda bi, h, i: (bi, 0, kb + h)),
            pl.BlockSpec((1, s, DIFF_V_DIM), lambda bi, h, i: (bi, 0, vb + h)),
            pl.BlockSpec((1, 1, DIFF_V_DIM), lambda bi, h, i: (h, 0, 0)),
        ],
        out_specs=pl.BlockSpec((1, tq, DIFF_V_DIM), lambda bi, h, i: (bi, i, h)),
        out_shape=jax.ShapeDtypeStruct((b, s, DIFF_WIDTH), BF16),
        compiler_params=_params("parallel", "parallel", "arbitrary"),
        name="diff_attention",
    )(lq1, lk1, lq2, lk2, z3, z3, z3, gain_b)


def _out_proj_kernel(x_ref, ma_ref, mb_ref, mc_ref, w_ref, o_ref, mix_ref):
    mix_ref[:, :GMLP_WIDTH] = ma_ref[...]
    mix_ref[:, GMLP_WIDTH:GMLP_WIDTH + DIFF_WIDTH] = mb_ref[...]
    mix_ref[:, GMLP_WIDTH + DIFF_WIDTH:] = mc_ref[...]
    o_ref[...] = x_ref[...] + jnp.dot(mix_ref[...], w_ref[...], preferred_element_type=F32)


def _out_proj(x2, mix_a, mix_b, mix_c, w):
    m, d = x2.shape
    tm = OUT_PROJ_TM
    assert m % tm == 0
    return pl.pallas_call(
        _out_proj_kernel,
        grid=(m // tm,),
        in_specs=[
            pl.BlockSpec((tm, d), lambda i: (i, 0)),
            pl.BlockSpec((tm, GMLP_WIDTH), lambda i: (i, 0)),
            pl.BlockSpec((tm, DIFF_WIDTH), lambda i: (i, 0)),
            pl.BlockSpec((tm, CONV_WIDTH), lambda i: (i, 0)),
            pl.BlockSpec(w.shape, lambda i: (0, 0)),
        ],
        out_specs=pl.BlockSpec((tm, d), lambda i: (i, 0)),
        out_shape=jax.ShapeDtypeStruct((m, d), F32),
        scratch_shapes=[pltpu.VMEM((tm, w.shape[0]), BF16)],
        compiler_params=_params("parallel"),
        name="out_proj",
    )(x2, mix_a, mix_b, mix_c, w)


def _ffn_kernel(x_ref, g_ref, wg_ref, wu_ref, wd_ref, fg_ref, o_ref, h_ref, *, final_norm):
    f = pl.program_id(1)

    @pl.when(f == 0)
    def _():
        x = x_ref[...]
        h_ref[...] = _rms_rows(x, g_ref[...]).astype(BF16)
        o_ref[...] = x

    h = h_ref[...]
    gate = jnp.dot(h, wg_ref[...], preferred_element_type=F32)
    up = jnp.dot(h, wu_ref[...], preferred_element_type=F32)
    act = (gate * jax.nn.sigmoid(gate) * up).astype(BF16)
    o_ref[...] += jnp.dot(act, wd_ref[...], preferred_element_type=F32)

    if final_norm:
        @pl.when(f == pl.num_programs(1) - 1)
        def _():
            o_ref[...] = _rms_rows(o_ref[...], fg_ref[...])


def _ffn(x2, gain, wg, wu, wd, final_gain, final_norm):
    m, d = x2.shape
    dff = wg.shape[1]
    tm, tf = FFN_TM, FFN_TF
    assert m % tm == 0 and dff % tf == 0
    return pl.pallas_call(
        functools.partial(_ffn_kernel, final_norm=final_norm),
        grid=(m // tm, dff // tf),
        in_specs=[
            pl.BlockSpec((tm, d), lambda i, f: (i, 0)),
            pl.BlockSpec((1, d), lambda i, f: (0, 0)),
            pl.BlockSpec((d, tf), lambda i, f: (0, f)),
            pl.BlockSpec((d, tf), lambda i, f: (0, f)),
            pl.BlockSpec((tf, d), lambda i, f: (f, 0)),
            pl.BlockSpec((1, d), lambda i, f: (0, 0)),
        ],
        out_specs=pl.BlockSpec((tm, d), lambda i, f: (i, 0)),
        out_shape=jax.ShapeDtypeStruct((m, d), F32),
        scratch_shapes=[pltpu.VMEM((tm, d), BF16)],
        compiler_params=_params("parallel", "arbitrary"),
        name="ffn",
    )(x2, gain, wg, wu, wd, final_gain)


def kernel(x, positions, attn_norm, w_in, gmlp_ln_g, gmlp_ln_b, gmlp_ws, gmlp_bs, lambda_q1, lambda_k1,
           lambda_q2, lambda_k2, conv_w, mix_norm, w_out, ffn_norm, w_gate, w_up, w_down, final_norm):
    b, s, d = x.shape
    m = b * s
    depth = w_in.shape[0]

    inv_freq = 1.0 / (ROPE_THETA ** (jnp.arange(0, HEAD_DIM, 2, dtype=F32) / HEAD_DIM))
    ang = positions.astype(F32).reshape(m, 1) * inv_freq
    cosf = jnp.concatenate([jnp.cos(ang), jnp.cos(ang)], axis=-1)
    sinf = jnp.concatenate([-jnp.sin(ang), jnp.sin(ang)], axis=-1)

    x2 = x.reshape(m, d)
    for l in range(depth):
        lam_init = 0.8 - 0.6 * math.exp(-0.3 * l)
        gain = mix_norm[l].astype(F32)
        z = _in_proj(x2, attn_norm[l].reshape(1, d), w_in[l].astype(BF16), cosf, sinf)
        z3 = z.reshape(b, s, z.shape[1])
        mix_a, mix_c = _local_mixers(
            z3, gmlp_ln_g[l], gmlp_ln_b[l], gmlp_ws[l], gmlp_bs[l].reshape(N_GROUPS, GMLP_CHUNK, 1), conv_w[l],
            gain[:GMLP_WIDTH].reshape(1, GMLP_WIDTH), gain[GMLP_WIDTH + DIFF_WIDTH:].reshape(1, CONV_WIDTH))
        mix_b = _diff_attention(
            z3, lambda_q1[l].reshape(1, HEAD_DIM), lambda_k1[l].reshape(1, HEAD_DIM),
            lambda_q2[l].reshape(1, HEAD_DIM), lambda_k2[l].reshape(1, HEAD_DIM),
            gain[GMLP_WIDTH:GMLP_WIDTH + DIFF_WIDTH].reshape(N_GROUPS, 1, DIFF_V_DIM), lam_init)
        x2 = _out_proj(x2, mix_a.reshape(m, GMLP_WIDTH), mix_b.reshape(m, DIFF_WIDTH),
                       mix_c.reshape(m, CONV_WIDTH), w_out[l].astype(BF16))
        x2 = _ffn(x2, ffn_norm[l].reshape(1, d), w_gate[l].astype(BF16), w_up[l].astype(BF16),
                  w_down[l].astype(BF16), final_norm.reshape(1, d), final_norm=(l == depth - 1))
    return x2.reshape(b, s, d)
```

```python
import functools
import math

import jax
import jax.numpy as jnp
from jax import lax
from jax.experimental import pallas as pl
from jax.experimental.pallas import tpu as pltpu

F32 = jnp.float32
BF16 = jnp.bfloat16

HEAD_DIM = 128
N_GROUPS = 4
GMLP_WIDTH = N_GROUPS * HEAD_DIM
GMLP_CHUNK = 128
DIFF_QK_WIDTH = N_GROUPS * 2 * HEAD_DIM
DIFF_V_DIM = 2 * HEAD_DIM
DIFF_WIDTH = N_GROUPS * DIFF_V_DIM
CONV_WIDTH = N_GROUPS * HEAD_DIM
CONV_K = 3
ROPE_THETA = 10000.0
RMS_EPS = 1e-6
LN_EPS = 1e-5
LOG2_E = math.log2(math.e)

Q_OFF = 2 * GMLP_WIDTH
K_OFF = Q_OFF + DIFF_QK_WIDTH
V_OFF = K_OFF + DIFF_QK_WIDTH
C_OFF = V_OFF + DIFF_WIDTH

VMEM_LIMIT_BYTES = 56 * 1024 * 1024

IN_PROJ_TM = 1024
IN_PROJ_TN = 512
LOCAL_ROWS = 512
ATTN_TQ = 512
ATTN_ROW_CHUNK = 32
OUT_PROJ_TM = 512
FFN_TM = 1024
FFN_TF = 256
CONV_HALO_ROWS = 8


def _params(*semantics):
    return pltpu.CompilerParams(dimension_semantics=semantics, vmem_limit_bytes=VMEM_LIMIT_BYTES)


def _rms_rows(x, gain):
    return x * lax.rsqrt(jnp.mean(x * x, axis=-1, keepdims=True) + RMS_EPS) * gain


def _group_rms(x):
    return x * lax.rsqrt(jnp.mean(x * x, axis=-1, keepdims=True) + RMS_EPS)


def _in_proj_kernel(x_ref, g_ref, w_ref, cos_ref, sin_ref, z_ref, h_ref, *, tn):
    j = pl.program_id(1)

    @pl.when(j == 0)
    def _():
        h_ref[...] = _rms_rows(x_ref[...], g_ref[...]).astype(BF16)

    acc = jnp.dot(h_ref[...], w_ref[...], preferred_element_type=F32)
    q_lo, k_lo, v_lo = Q_OFF // tn, K_OFF // tn, V_OFF // tn

    def rope(scale):
        cosf, sinf = cos_ref[...], sin_ref[...]
        for c in range(tn // HEAD_DIM):
            xc = acc[:, c * HEAD_DIM:(c + 1) * HEAD_DIM]
            r = xc * cosf + pltpu.roll(xc, HEAD_DIM // 2, 1) * sinf
            if scale != 1.0:
                r = r * scale
            z_ref[:, c * HEAD_DIM:(c + 1) * HEAD_DIM] = r.astype(z_ref.dtype)

    @pl.when((j >= q_lo) & (j < k_lo))
    def _():
        rope(LOG2_E / math.sqrt(HEAD_DIM))

    @pl.when((j >= k_lo) & (j < v_lo))
    def _():
        rope(1.0)

    @pl.when((j < q_lo) | (j >= v_lo))
    def _():
        z_ref[...] = acc.astype(z_ref.dtype)


def _in_proj(x2, gain, w, cosf, sinf):
    m, d = x2.shape
    n = w.shape[1]
    tm, tn = IN_PROJ_TM, IN_PROJ_TN
    assert m % tm == 0 and n % tn == 0 and Q_OFF % tn == 0
    return pl.pallas_call(
        functools.partial(_in_proj_kernel, tn=tn),
        grid=(m // tm, n // tn),
        in_specs=[
            pl.BlockSpec((tm, d), lambda i, j: (i, 0)),
            pl.BlockSpec((1, d), lambda i, j: (0, 0)),
            pl.BlockSpec((d, tn), lambda i, j: (0, j)),
            pl.BlockSpec((tm, HEAD_DIM), lambda i, j: (i, 0)),
            pl.BlockSpec((tm, HEAD_DIM), lambda i, j: (i, 0)),
        ],
        out_specs=pl.BlockSpec((tm, tn), lambda i, j: (i, j)),
        out_shape=jax.ShapeDtypeStruct((m, n), BF16),
        scratch_shapes=[pltpu.VMEM((tm, d), BF16)],
        compiler_params=_params("parallel", "arbitrary"),
        name="in_proj",
    )(x2, gain, w, cosf, sinf)


def _gelu(x):
    return 0.5 * x * (1.0 + lax.erf(x * math.sqrt(0.5)))


def _local_kernel(za_ref, bg_ref, cg_ref, hc_ref, cgh_ref, hch_ref, lng_ref, lnb_ref, ws_ref, bs_ref,
                  cw_ref, ga_ref, gc_ref, oa_ref, oc_ref, *, rows):
    i = pl.program_id(1)

    t_idx = lax.broadcasted_iota(jnp.int32, (GMLP_CHUNK, GMLP_CHUNK), 0)
    s_idx = lax.broadcasted_iota(jnp.int32, (GMLP_CHUNK, GMLP_CHUNK), 1)
    for g in range(N_GROUPS):
        w = jnp.where(s_idx <= t_idx, ws_ref[g], 0.0).astype(BF16)
        bias = bs_ref[g]
        lo = g * HEAD_DIM
        for c in range(rows // GMLP_CHUNK):
            r0 = c * GMLP_CHUNK
            u = _gelu(za_ref[0, r0:r0 + GMLP_CHUNK, lo:lo + HEAD_DIM].astype(F32))
            v = _gelu(za_ref[0, r0:r0 + GMLP_CHUNK, GMLP_WIDTH + lo:GMLP_WIDTH + lo + HEAD_DIM].astype(F32))
            mu = jnp.mean(v, axis=-1, keepdims=True)
            vc = v - mu
            var = jnp.mean(vc * vc, axis=-1, keepdims=True)
            vn = vc * lax.rsqrt(var + LN_EPS) * lng_ref[g:g + 1, :] + lnb_ref[g:g + 1, :]
            mixed = jnp.dot(w, vn.astype(BF16), preferred_element_type=F32) + bias
            ya = _group_rms(u * mixed) * ga_ref[:, lo:lo + HEAD_DIM]
            oa_ref[0, r0:r0 + GMLP_CHUNK, lo:lo + HEAD_DIM] = ya.astype(oa_ref.dtype)

    xh = cg_ref[0].astype(F32) * hc_ref[0].astype(F32)
    halo = cgh_ref[0].astype(F32) * hch_ref[0].astype(F32)
    halo = jnp.where(i == 0, 0.0, halo)
    row = lax.broadcasted_iota(jnp.int32, xh.shape, 0)
    prev1 = jnp.where(row == 0, halo[CONV_HALO_ROWS - 1:CONV_HALO_ROWS, :], pltpu.roll(xh, 1, 0))
    prev2 = jnp.where(row == 0, halo[CONV_HALO_ROWS - 2:CONV_HALO_ROWS - 1, :],
                      jnp.where(row == 1, halo[CONV_HALO_ROWS - 1:CONV_HALO_ROWS, :], pltpu.roll(xh, 2, 0)))
    y = cw_ref[0:1, :] * prev2 + cw_ref[1:2, :] * prev1 + cw_ref[2:3, :] * xh
    yc = bg_ref[0].astype(F32) * y
    for g in range(N_GROUPS):
        lo = g * HEAD_DIM
        oc_ref[0, :, lo:lo + HEAD_DIM] = (_group_rms(yc[:, lo:lo + HEAD_DIM])
                                          * gc_ref[:, lo:lo + HEAD_DIM]).astype(oc_ref.dtype)


def _local_mixers(z3, ln_g, ln_b, ws, bs, conv_w, gain_a, gain_c):
    b, s, _ = z3.shape
    rows = LOCAL_ROWS
    assert s % rows == 0 and rows % GMLP_CHUNK == 0 and C_OFF % CONV_WIDTH == 0
    cb = C_OFF // CONV_WIDTH
    hb = rows // CONV_HALO_ROWS
    full = lambda shape: pl.BlockSpec(shape, lambda bi, i: (0,) * len(shape))
    halo_map = lambda col: (lambda bi, i: (bi, jnp.maximum(i * hb - 1, 0), col))
    return pl.pallas_call(
        functools.partial(_local_kernel, rows=rows),
        grid=(b, s // rows),
        in_specs=[
            pl.BlockSpec((1, rows, 2 * GMLP_WIDTH), lambda bi, i: (bi, i, 0)),
            pl.BlockSpec((1, rows, CONV_WIDTH), lambda bi, i: (bi, i, cb)),
            pl.BlockSpec((1, rows, CONV_WIDTH), lambda bi, i: (bi, i, cb + 1)),
            pl.BlockSpec((1, rows, CONV_WIDTH), lambda bi, i: (bi, i, cb + 2)),
            pl.BlockSpec((1, CONV_HALO_ROWS, CONV_WIDTH), halo_map(cb + 1)),
            pl.BlockSpec((1, CONV_HALO_ROWS, CONV_WIDTH), halo_map(cb + 2)),
            full((N_GROUPS, HEAD_DIM)),
            full((N_GROUPS, HEAD_DIM)),
            full((N_GROUPS, GMLP_CHUNK, GMLP_CHUNK)),
            full((N_GROUPS, GMLP_CHUNK, 1)),
            full((CONV_K, CONV_WIDTH)),
            full((1, GMLP_WIDTH)),
            full((1, CONV_WIDTH)),
        ],
        out_specs=[
            pl.BlockSpec((1, rows, GMLP_WIDTH), lambda bi, i: (bi, i, 0)),
            pl.BlockSpec((1, rows, CONV_WIDTH), lambda bi, i: (bi, i, 0)),
        ],
        out_shape=[jax.ShapeDtypeStruct((b, s, GMLP_WIDTH), BF16),
                   jax.ShapeDtypeStruct((b, s, CONV_WIDTH), BF16)],
        compiler_params=_params("parallel", "arbitrary"),
        name="local_mixers",
    )(z3, z3, z3, z3, z3, z3, ln_g, ln_b, ws, bs, conv_w, gain_a, gain_c)


def _attn_kernel(lq1_ref, lk1_ref, lq2_ref, lk2_ref, q_ref, k_ref, v_ref, gain_ref, o_ref,
                 qs_ref, m_ref, l_ref, alpha_ref, acc_ref, p_ref, *, tq, lam_init):
    qi = pl.program_id(2)
    rc = ATTN_ROW_CHUNK
    n_tiles = tq // HEAD_DIM

    q = q_ref[0]
    zero = jnp.zeros((tq, HEAD_DIM), q.dtype)
    qs_ref[:tq, :HEAD_DIM] = q[:, :HEAD_DIM]
    qs_ref[:tq, HEAD_DIM:] = zero
    qs_ref[tq:, :HEAD_DIM] = zero
    qs_ref[tq:, HEAD_DIM:] = q[:, HEAD_DIM:]
    m_ref[...] = jnp.full(m_ref.shape, -jnp.inf, F32)
    l_ref[...] = jnp.zeros(l_ref.shape, F32)
    acc_ref[...] = jnp.zeros(acc_ref.shape, F32)

    def step(kb, diagonal):
        off = pl.multiple_of(kb * tq, tq)
        s = lax.dot_general(qs_ref[...], k_ref[0, pl.ds(off, tq), :], (((1,), (1,)), ((), ())),
                            preferred_element_type=F32)
        for c in range(2 * tq // rc):
            r0 = c * rc
            q0 = r0 % tq
            live = [t for t in range(n_tiles) if (not diagonal) or t * HEAD_DIM <= q0 + rc - 1]
            tiles = []
            for t in live:
                st = s[r0:r0 + rc, t * HEAD_DIM:(t + 1) * HEAD_DIM]
                if diagonal and (t + 1) * HEAD_DIM - 1 > q0:
                    row = lax.broadcasted_iota(jnp.int32, (rc, HEAD_DIM), 0) + q0
                    col = lax.broadcasted_iota(jnp.int32, (rc, HEAD_DIM), 1) + t * HEAD_DIM
                    st = jnp.where(col <= row, st, -jnp.inf)
                tiles.append(st)
            m_old = m_ref[r0:r0 + rc, :]
            m_new = jnp.maximum(m_old, jnp.max(functools.reduce(jnp.maximum, tiles), axis=-1, keepdims=True))
            alpha = jnp.exp2(m_old - m_new)
            ps = [jnp.exp2(st - m_new) for st in tiles]
            l_ref[r0:r0 + rc, :] = alpha * l_ref[r0:r0 + rc, :] + functools.reduce(jnp.add, ps)
            m_ref[r0:r0 + rc, :] = m_new
            alpha_ref[r0:r0 + rc, :] = alpha
            for t in range(n_tiles):
                p_t = ps[live.index(t)].astype(BF16) if t in live else jnp.zeros((rc, HEAD_DIM), BF16)
                p_ref[r0:r0 + rc, t * HEAD_DIM:(t + 1) * HEAD_DIM] = p_t
        pv = jnp.dot(p_ref[...], v_ref[0, pl.ds(off, tq), :], preferred_element_type=F32)
        alpha = alpha_ref[...]
        for t in range(DIFF_V_DIM // HEAD_DIM):
            cols = slice(t * HEAD_DIM, (t + 1) * HEAD_DIM)
            acc_ref[:, cols] = acc_ref[:, cols] * alpha + pv[:, cols]

    def body(kb, carry):
        step(kb, diagonal=False)
        return carry

    lax.fori_loop(0, qi, body, 0)
    step(qi, diagonal=True)

    lam = (jnp.exp(jnp.sum(lq1_ref[...] * lk1_ref[...], axis=-1, keepdims=True))
           - jnp.exp(jnp.sum(lq2_ref[...] * lk2_ref[...], axis=-1, keepdims=True)) + lam_init)
    o = acc_ref[...] / jnp.sum(l_ref[...], axis=-1, keepdims=True)
    o = o[:tq] - lam * o[tq:]
    o = _group_rms(o) * (1.0 - lam_init) * gain_ref[0]
    o_ref[0] = o.astype(o_ref.dtype)


def _diff_attention(z3, lq1, lk1, lq2, lk2, gain_b, lam_init):
    b, s, _ = z3.shape
    tq = ATTN_TQ
    assert s % tq == 0
    qb, kb, vb = Q_OFF // DIFF_V_DIM, K_OFF // DIFF_V_DIM, V_OFF // DIFF_V_DIM
    vec = pl.BlockSpec((1, HEAD_DIM), lambda bi, h, i: (0, 0))
    return pl.pallas_call(
        functools.partial(_attn_kernel, tq=tq, lam_init=lam_init),
        grid=(b, N_GROUPS, s // tq),
        in_specs=[
            vec, vec, vec, vec,
            pl.BlockSpec((1, tq, DIFF_V_DIM), lambda bi, h, i: (bi, i, qb + h)),
            pl.BlockSpec((1, s, DIFF_V_DIM), lambda bi, h, i: (bi, 0, kb + h)),
            pl.BlockSpec((1, s, DIFF_V_DIM), lambda bi, h, i: (bi, 0, vb + h)),
            pl.BlockSpec((1, 1, DIFF_V_DIM), lambda bi, h, i: (h, 0, 0)),
        ],
        out_specs=pl.BlockSpec((1, tq, DIFF_V_DIM), lambda bi, h, i: (bi, i, h)),
        out_shape=jax.ShapeDtypeStruct((b, s, DIFF_WIDTH), BF16),
        scratch_shapes=[
            pltpu.VMEM((2 * tq, 2 * HEAD_DIM), BF16),
            pltpu.VMEM((2 * tq, HEAD_DIM), F32),
            pltpu.VMEM((2 * tq, HEAD_DIM), F32),
            pltpu.VMEM((2 * tq, HEAD_DIM), F32),
            pltpu.VMEM((2 * tq, DIFF_V_DIM), F32),
            pltpu.VMEM((2 * tq, tq), BF16),
        ],
        compiler_params=_params("parallel", "parallel", "arbitrary"),
        name="diff_attention",
    )(lq1, lk1, lq2, lk2, z3, z3, z3, gain_b)


def _out_proj_kernel(x_ref, ma_ref, mb_ref, mc_ref, w_ref, o_ref, mix_ref):
    mix_ref[:, :GMLP_WIDTH] = ma_ref[...]
    mix_ref[:, GMLP_WIDTH:GMLP_WIDTH + DIFF_WIDTH] = mb_ref[...]
    mix_ref[:, GMLP_WIDTH + DIFF_WIDTH:] = mc_ref[...]
    o_ref[...] = x_ref[...] + jnp.dot(mix_ref[...], w_ref[...], preferred_element_type=F32)


def _out_proj(x2, mix_a, mix_b, mix_c, w):
    m, d = x2.shape
    tm = OUT_PROJ_TM
    assert m % tm == 0
    return pl.pallas_call(
        _out_proj_kernel,
        grid=(m // tm,),
        in_specs=[
            pl.BlockSpec((tm, d), lambda i: (i, 0)),
            pl.BlockSpec((tm, GMLP_WIDTH), lambda i: (i, 0)),
            pl.BlockSpec((tm, DIFF_WIDTH), lambda i: (i, 0)),
            pl.BlockSpec((tm, CONV_WIDTH), lambda i: (i, 0)),
            pl.BlockSpec(w.shape, lambda i: (0, 0)),
        ],
        out_specs=pl.BlockSpec((tm, d), lambda i: (i, 0)),
        out_shape=jax.ShapeDtypeStruct((m, d), F32),
        scratch_shapes=[pltpu.VMEM((tm, w.shape[0]), BF16)],
        compiler_params=_params("parallel"),
        name="out_proj",
    )(x2, mix_a, mix_b, mix_c, w)


def _ffn_kernel(x_ref, g_ref, wg_ref, wu_ref, wd_ref, fg_ref, o_ref, h_ref, *, final_norm):
    f = pl.program_id(1)

    @pl.when(f == 0)
    def _():
        x = x_ref[...]
        h_ref[...] = _rms_rows(x, g_ref[...]).astype(BF16)
        o_ref[...] = x

    h = h_ref[...]
    gate = jnp.dot(h, wg_ref[...], preferred_element_type=F32)
    up = jnp.dot(h, wu_ref[...], preferred_element_type=F32)
    act = (gate * jax.nn.sigmoid(gate) * up).astype(BF16)
    o_ref[...] += jnp.dot(act, wd_ref[...], preferred_element_type=F32)

    if final_norm:
        @pl.when(f == pl.num_programs(1) - 1)
        def _():
            o_ref[...] = _rms_rows(o_ref[...], fg_ref[...])


def _ffn(x2, gain, wg, wu, wd, final_gain, final_norm):
    m, d = x2.shape
    dff = wg.shape[1]
    tm, tf = FFN_TM, FFN_TF
    assert m % tm == 0 and dff % tf == 0
    return pl.pallas_call(
        functools.partial(_ffn_kernel, final_norm=final_norm),
        grid=(m // tm, dff // tf),
        in_specs=[
            pl.BlockSpec((tm, d), lambda i, f: (i, 0)),
            pl.BlockSpec((1, d), lambda i, f: (0, 0)),
            pl.BlockSpec((d, tf), lambda i, f: (0, f)),
            pl.BlockSpec((d, tf), lambda i, f: (0, f)),
            pl.BlockSpec((tf, d), lambda i, f: (f, 0)),
            pl.BlockSpec((1, d), lambda i, f: (0, 0)),
        ],
        out_specs=pl.BlockSpec((tm, d), lambda i, f: (i, 0)),
        out_shape=jax.ShapeDtypeStruct((m, d), F32),
        scratch_shapes=[pltpu.VMEM((tm, d), BF16)],
        compiler_params=_params("parallel", "arbitrary"),
        name="ffn",
    )(x2, gain, wg, wu, wd, final_gain)


def kernel(x, positions, attn_norm, w_in, gmlp_ln_g, gmlp_ln_b, gmlp_ws, gmlp_bs, lambda_q1, lambda_k1,
           lambda_q2, lambda_k2, conv_w, mix_norm, w_out, ffn_norm, w_gate, w_up, w_down, final_norm):
    b, s, d = x.shape
    m = b * s
    depth = w_in.shape[0]

    inv_freq = 1.0 / (ROPE_THETA ** (jnp.arange(0, HEAD_DIM, 2, dtype=F32) / HEAD_DIM))
    ang = positions.astype(F32).reshape(m, 1) * inv_freq
    cosf = jnp.concatenate([jnp.cos(ang), jnp.cos(ang)], axis=-1)
    sinf = jnp.concatenate([-jnp.sin(ang), jnp.sin(ang)], axis=-1)

    x2 = x.reshape(m, d)
    for l in range(depth):
        lam_init = 0.8 - 0.6 * math.exp(-0.3 * l)
        gain = mix_norm[l].astype(F32)
        z = _in_proj(x2, attn_norm[l].reshape(1, d), w_in[l].astype(BF16), cosf, sinf)
        z3 = z.reshape(b, s, z.shape[1])
        mix_a, mix_c = _local_mixers(
            z3, gmlp_ln_g[l], gmlp_ln_b[l], gmlp_ws[l], gmlp_bs[l].reshape(N_GROUPS, GMLP_CHUNK, 1), conv_w[l],
            gain[:GMLP_WIDTH].reshape(1, GMLP_WIDTH), gain[GMLP_WIDTH + DIFF_WIDTH:].reshape(1, CONV_WIDTH))
        mix_b = _diff_attention(
            z3, lambda_q1[l].reshape(1, HEAD_DIM), lambda_k1[l].reshape(1, HEAD_DIM),
            lambda_q2[l].reshape(1, HEAD_DIM), lambda_k2[l].reshape(1, HEAD_DIM),
            gain[GMLP_WIDTH:GMLP_WIDTH + DIFF_WIDTH].reshape(N_GROUPS, 1, DIFF_V_DIM), lam_init)
        x2 = _out_proj(x2, mix_a.reshape(m, GMLP_WIDTH), mix_b.reshape(m, DIFF_WIDTH),
                       mix_c.reshape(m, CONV_WIDTH), w_out[l].astype(BF16))
        x2 = _ffn(x2, ffn_norm[l].reshape(1, d), w_gate[l].astype(BF16), w_up[l].astype(BF16),
                  w_down[l].astype(BF16), final_norm.reshape(1, d), final_norm=(l == depth - 1))
    return x2.reshape(b, s, d)
```

```python
import functools
import math

import jax
import jax.numpy as jnp
from jax import lax
from jax.experimental import pallas as pl
from jax.experimental.pallas import tpu as pltpu

F32 = jnp.float32
BF16 = jnp.bfloat16

HEAD_DIM = 128
N_GROUPS = 4
GMLP_WIDTH = N_GROUPS * HEAD_DIM
GMLP_CHUNK = 128
DIFF_QK_WIDTH = N_GROUPS * 2 * HEAD_DIM
DIFF_V_DIM = 2 * HEAD_DIM
DIFF_WIDTH = N_GROUPS * DIFF_V_DIM
CONV_WIDTH = N_GROUPS * HEAD_DIM
CONV_K = 3
ROPE_THETA = 10000.0
RMS_EPS = 1e-6
LN_EPS = 1e-5
LOG2_E = math.log2(math.e)

Q_OFF = 2 * GMLP_WIDTH
K_OFF = Q_OFF + DIFF_QK_WIDTH
V_OFF = K_OFF + DIFF_QK_WIDTH
C_OFF = V_OFF + DIFF_WIDTH

VMEM_LIMIT_BYTES = 56 * 1024 * 1024

IN_PROJ_TM = 1024
IN_PROJ_TN = 512
LOCAL_ROWS = 512
ATTN_TQ = 512
ATTN_ROW_CHUNK = 32
OUT_PROJ_TM = 512
FFN_TM = 1024
FFN_TF = 256
CONV_HALO_ROWS = 8


def _params(*semantics):
    return pltpu.CompilerParams(dimension_semantics=semantics, vmem_limit_bytes=VMEM_LIMIT_BYTES)


def _rms_rows(x, gain):
    return x * lax.rsqrt(jnp.mean(x * x, axis=-1, keepdims=True) + RMS_EPS) * gain


def _group_rms(x):
    return x * lax.rsqrt(jnp.mean(x * x, axis=-1, keepdims=True) + RMS_EPS)


def _in_proj_kernel(x_ref, g_ref, w_ref, cos_ref, sin_ref, z_ref, h_ref, *, tn):
    j = pl.program_id(1)

    @pl.when(j == 0)
    def _():
        h_ref[...] = _rms_rows(x_ref[...], g_ref[...]).astype(BF16)

    acc = jnp.dot(h_ref[...], w_ref[...], preferred_element_type=F32)
    q_lo, k_lo, v_lo = Q_OFF // tn, K_OFF // tn, V_OFF // tn

    def rope(scale):
        cosf, sinf = cos_ref[...], sin_ref[...]
        for c in range(tn // HEAD_DIM):
            xc = acc[:, c * HEAD_DIM:(c + 1) * HEAD_DIM]
            r = xc * cosf + pltpu.roll(xc, HEAD_DIM // 2, 1) * sinf
            if scale != 1.0:
                r = r * scale
            z_ref[:, c * HEAD_DIM:(c + 1) * HEAD_DIM] = r.astype(z_ref.dtype)

    @pl.when((j >= q_lo) & (j < k_lo))
    def _():
        rope(LOG2_E / math.sqrt(HEAD_DIM))

    @pl.when((j >= k_lo) & (j < v_lo))
    def _():
        rope(1.0)

    @pl.when((j < q_lo) | (j >= v_lo))
    def _():
        z_ref[...] = acc.astype(z_ref.dtype)


def _in_proj(x2, gain, w, cosf, sinf):
    m, d = x2.shape
    n = w.shape[1]
    tm, tn = IN_PROJ_TM, IN_PROJ_TN
    assert m % tm == 0 and n % tn == 0 and Q_OFF % tn == 0
    return pl.pallas_call(
        functools.partial(_in_proj_kernel, tn=tn),
        grid=(m // tm, n // tn),
        in_specs=[
            pl.BlockSpec((tm, d), lambda i, j: (i, 0)),
            pl.BlockSpec((1, d), lambda i, j: (0, 0)),
            pl.BlockSpec((d, tn), lambda i, j: (0, j)),
            pl.BlockSpec((tm, HEAD_DIM), lambda i, j: (i, 0)),
            pl.BlockSpec((tm, HEAD_DIM), lambda i, j: (i, 0)),
        ],
        out_specs=pl.BlockSpec((tm, tn), lambda i, j: (i, j)),
        out_shape=jax.ShapeDtypeStruct((m, n), BF16),
        scratch_shapes=[pltpu.VMEM((tm, d), BF16)],
        compiler_params=_params("parallel", "arbitrary"),
        name="in_proj",
    )(x2, gain, w, cosf, sinf)


def _gelu(x):
    return 0.5 * x * (1.0 + lax.erf(x * math.sqrt(0.5)))


def _local_kernel(za_ref, bg_ref, cg_ref, hc_ref, cgh_ref, hch_ref, lng_ref, lnb_ref, ws_ref, bs_ref,
                  cw_ref, ga_ref, gc_ref, oa_ref, oc_ref, *, rows):
    i = pl.program_id(1)

    t_idx = lax.broadcasted_iota(jnp.int32, (GMLP_CHUNK, GMLP_CHUNK), 0)
    s_idx = lax.broadcasted_iota(jnp.int32, (GMLP_CHUNK, GMLP_CHUNK), 1)
    for g in range(N_GROUPS):
        w = jnp.where(s_idx <= t_idx, ws_ref[g], 0.0).astype(BF16)
        bias = bs_ref[g]
        lo = g * HEAD_DIM
        for c in range(rows // GMLP_CHUNK):
            r0 = c * GMLP_CHUNK
            u = _gelu(za_ref[0, r0:r0 + GMLP_CHUNK, lo:lo + HEAD_DIM].astype(F32))
            v = _gelu(za_ref[0, r0:r0 + GMLP_CHUNK, GMLP_WIDTH + lo:GMLP_WIDTH + lo + HEAD_DIM].astype(F32))
            mu = jnp.mean(v, axis=-1, keepdims=True)
            vc = v - mu
            var = jnp.mean(vc * vc, axis=-1, keepdims=True)
            vn = vc * lax.rsqrt(var + LN_EPS) * lng_ref[g:g + 1, :] + lnb_ref[g:g + 1, :]
            mixed = jnp.dot(w, vn.astype(BF16), preferred_element_type=F32) + bias
            ya = _group_rms(u * mixed) * ga_ref[:, lo:lo + HEAD_DIM]
            oa_ref[0, r0:r0 + GMLP_CHUNK, lo:lo + HEAD_DIM] = ya.astype(oa_ref.dtype)

    xh = cg_ref[0].astype(F32) * hc_ref[0].astype(F32)
    halo = cgh_ref[0].astype(F32) * hch_ref[0].astype(F32)
    halo = jnp.where(i == 0, 0.0, halo)
    row = lax.broadcasted_iota(jnp.int32, xh.shape, 0)
    prev1 = jnp.where(row == 0, halo[CONV_HALO_ROWS - 1:CONV_HALO_ROWS, :], pltpu.roll(xh, 1, 0))
    prev2 = jnp.where(row == 0, halo[CONV_HALO_ROWS - 2:CONV_HALO_ROWS - 1, :],
                      jnp.where(row == 1, halo[CONV_HALO_ROWS - 1:CONV_HALO_ROWS, :], pltpu.roll(xh, 2, 0)))
    y = cw_ref[0:1, :] * prev2 + cw_ref[1:2, :] * prev1 + cw_ref[2:3, :] * xh
    yc = bg_ref[0].astype(F32) * y
    for g in range(N_GROUPS):
        lo = g * HEAD_DIM
        oc_ref[0, :, lo:lo + HEAD_DIM] = (_group_rms(yc[:, lo:lo + HEAD_DIM])
                                          * gc_ref[:, lo:lo + HEAD_DIM]).astype(oc_ref.dtype)


def _local_mixers(z3, ln_g, ln_b, ws, bs, conv_w, gain_a, gain_c):
    b, s, _ = z3.shape
    rows = LOCAL_ROWS
    assert s % rows == 0 and rows % GMLP_CHUNK == 0 and C_OFF % CONV_WIDTH == 0
    cb = C_OFF // CONV_WIDTH
    hb = rows // CONV_HALO_ROWS
    full = lambda shape: pl.BlockSpec(shape, lambda bi, i: (0,) * len(shape))
    halo_map = lambda col: (lambda bi, i: (bi, jnp.maximum(i * hb - 1, 0), col))
    return pl.pallas_call(
        functools.partial(_local_kernel, rows=rows),
        grid=(b, s // rows),
        in_specs=[
            pl.BlockSpec((1, rows, 2 * GMLP_WIDTH), lambda bi, i: (bi, i, 0)),
            pl.BlockSpec((1, rows, CONV_WIDTH), lambda bi, i: (bi, i, cb)),
            pl.BlockSpec((1, rows, CONV_WIDTH), lambda bi, i: (bi, i, cb + 1)),
            pl.BlockSpec((1, rows, CONV_WIDTH), lambda bi, i: (bi, i, cb + 2)),
            pl.BlockSpec((1, CONV_HALO_ROWS, CONV_WIDTH), halo_map(cb + 1)),
            pl.BlockSpec((1, CONV_HALO_ROWS, CONV_WIDTH), halo_map(cb + 2)),
            full((N_GROUPS, HEAD_DIM)),
            full((N_GROUPS, HEAD_DIM)),
            full((N_GROUPS, GMLP_CHUNK, GMLP_CHUNK)),
            full((N_GROUPS, GMLP_CHUNK, 1)),
            full((CONV_K, CONV_WIDTH)),
            full((1, GMLP_WIDTH)),
            full((1, CONV_WIDTH)),
        ],
        out_specs=[
            pl.BlockSpec((1, rows, GMLP_WIDTH), lambda bi, i: (bi, i, 0)),
            pl.BlockSpec((1, rows, CONV_WIDTH), lambda bi, i: (bi, i, 0)),
        ],
        out_shape=[jax.ShapeDtypeStruct((b, s, GMLP_WIDTH), BF16),
                   jax.ShapeDtypeStruct((b, s, CONV_WIDTH), BF16)],
        compiler_params=_params("parallel", "arbitrary"),
        name="local_mixers",
    )(z3, z3, z3, z3, z3, z3, ln_g, ln_b, ws, bs, conv_w, gain_a, gain_c)


def _attn_kernel(lq1_ref, lk1_ref, lq2_ref, lk2_ref, q_ref, k_ref, v_ref, gain_ref, o_ref,
                 qs_ref, m_ref, l_ref, acc_ref, s0_ref, s1_ref, p0_ref, p1_ref, a0_ref, a1_ref, *, tq, lam_init):
    qi = pl.program_id(2)
    rc = ATTN_ROW_CHUNK
    n_tiles = tq // HEAD_DIM
    s_bufs, p_bufs, a_bufs = (s0_ref, s1_ref), (p0_ref, p1_ref), (a0_ref, a1_ref)

    q = q_ref[0]
    zero = jnp.zeros((tq, HEAD_DIM), q.dtype)
    qs_ref[:tq, :HEAD_DIM] = q[:, :HEAD_DIM]
    qs_ref[:tq, HEAD_DIM:] = zero
    qs_ref[tq:, :HEAD_DIM] = zero
    qs_ref[tq:, HEAD_DIM:] = q[:, HEAD_DIM:]
    m_ref[...] = jnp.full(m_ref.shape, -jnp.inf, F32)
    l_ref[...] = jnp.zeros(l_ref.shape, F32)
    acc_ref[...] = jnp.zeros(acc_ref.shape, F32)

    def key_rows(j):
        return pl.ds(pl.multiple_of(j * tq, tq), tq)

    def scores(j, buf):
        s_bufs[buf][...] = lax.dot_general(qs_ref[...], k_ref[0, key_rows(j), :], (((1,), (1,)), ((), ())),
                                           preferred_element_type=F32)

    def softmax(buf, diagonal):
        s_ref, p_ref, alpha_ref = s_bufs[buf], p_bufs[buf], a_bufs[buf]
        for c in range(2 * tq // rc):
            r0 = c * rc
            q0 = r0 % tq
            live = [t for t in range(n_tiles) if (not diagonal) or t * HEAD_DIM <= q0 + rc - 1]
            tiles = []
            for t in live:
                st = s_ref[r0:r0 + rc, t * HEAD_DIM:(t + 1) * HEAD_DIM]
                if diagonal and (t + 1) * HEAD_DIM - 1 > q0:
                    row = lax.broadcasted_iota(jnp.int32, (rc, HEAD_DIM), 0) + q0
                    col = lax.broadcasted_iota(jnp.int32, (rc, HEAD_DIM), 1) + t * HEAD_DIM
                    st = jnp.where(col <= row, st, -jnp.inf)
                tiles.append(st)
            m_old = m_ref[r0:r0 + rc, :]
            m_new = jnp.maximum(m_old, jnp.max(functools.reduce(jnp.maximum, tiles), axis=-1, keepdims=True))
            alpha = jnp.exp2(m_old - m_new)
            ps = [jnp.exp2(st - m_new) for st in tiles]
            l_ref[r0:r0 + rc, :] = alpha * l_ref[r0:r0 + rc, :] + functools.reduce(jnp.add, ps)
            m_ref[r0:r0 + rc, :] = m_new
            alpha_ref[r0:r0 + rc, :] = alpha
            for t in range(n_tiles):
                p_t = ps[live.index(t)].astype(BF16) if t in live else jnp.zeros((rc, HEAD_DIM), BF16)
                p_ref[r0:r0 + rc, t * HEAD_DIM:(t + 1) * HEAD_DIM] = p_t

    def accumulate(j, buf):
        pv = jnp.dot(p_bufs[buf][...], v_ref[0, key_rows(j), :], preferred_element_type=F32)
        alpha = a_bufs[buf][...]
        for t in range(DIFF_V_DIM // HEAD_DIM):
            cols = slice(t * HEAD_DIM, (t + 1) * HEAD_DIM)
            acc_ref[:, cols] = acc_ref[:, cols] * alpha + pv[:, cols]

    def stage(j, buf):
        scores(j + 1, 1 - buf)
        accumulate(j - 1, 1 - buf)
        softmax(buf, diagonal=False)

    scores(0, 0)

    @pl.when(qi >= 1)
    def _():
        scores(1, 1)
        softmax(0, diagonal=False)

    def pair(t, carry):
        stage(2 * t + 1, 1)
        stage(2 * t + 2, 0)
        return carry

    lax.fori_loop(0, (qi - 1) // 2, pair, 0)

    @pl.when((qi >= 2) & (qi % 2 == 0))
    def _():
        stage(qi - 1, 1)

    @pl.when(qi == 0)
    def _():
        softmax(0, diagonal=True)
        accumulate(0, 0)

    @pl.when((qi >= 1) & (qi % 2 == 0))
    def _():
        accumulate(qi - 1, 1)
        softmax(0, diagonal=True)
        accumulate(qi, 0)

    @pl.when(qi % 2 == 1)
    def _():
        accumulate(qi - 1, 0)
        softmax(1, diagonal=True)
        accumulate(qi, 1)

    lam = (jnp.exp(jnp.sum(lq1_ref[...] * lk1_ref[...], axis=-1, keepdims=True))
           - jnp.exp(jnp.sum(lq2_ref[...] * lk2_ref[...], axis=-1, keepdims=True)) + lam_init)
    o = acc_ref[...] / jnp.sum(l_ref[...], axis=-1, keepdims=True)
    o = o[:tq] - lam * o[tq:]
    o = _group_rms(o) * (1.0 - lam_init) * gain_ref[0]
    o_ref[0] = o.astype(o_ref.dtype)


def _diff_attention(z3, lq1, lk1, lq2, lk2, gain_b, lam_init):
    b, s, _ = z3.shape
    tq = ATTN_TQ
    assert s % tq == 0
    qb, kb, vb = Q_OFF // DIFF_V_DIM, K_OFF // DIFF_V_DIM, V_OFF // DIFF_V_DIM
    vec = pl.BlockSpec((1, HEAD_DIM), lambda bi, h, i: (0, 0))
    return pl.pallas_call(
        functools.partial(_attn_kernel, tq=tq, lam_init=lam_init),
        grid=(b, N_GROUPS, s // tq),
        in_specs=[
            vec, vec, vec, vec,
            pl.BlockSpec((1, tq, DIFF_V_DIM), lambda bi, h, i: (bi, i, qb + h)),
            pl.BlockSpec((1, s, DIFF_V_DIM), lambda bi, h, i: (bi, 0, kb + h)),
            pl.BlockSpec((1, s, DIFF_V_DIM), lambda bi, h, i: (bi, 0, vb + h)),
            pl.BlockSpec((1, 1, DIFF_V_DIM), lambda bi, h, i: (h, 0, 0)),
        ],
        out_specs=pl.BlockSpec((1, tq, DIFF_V_DIM), lambda bi, h, i: (bi, i, h)),
        out_shape=jax.ShapeDtypeStruct((b, s, DIFF_WIDTH), BF16),
        scratch_shapes=[
            pltpu.VMEM((2 * tq, 2 * HEAD_DIM), BF16),
            pltpu.VMEM((2 * tq, HEAD_DIM), F32),
            pltpu.VMEM((2 * tq, HEAD_DIM), F32),
            pltpu.VMEM((2 * tq, DIFF_V_DIM), F32),
            pltpu.VMEM((2 * tq, tq), F32),
            pltpu.VMEM((2 * tq, tq), F32),
            pltpu.VMEM((2 * tq, tq), BF16),
            pltpu.VMEM((2 * tq, tq), BF16),
            pltpu.VMEM((2 * tq, HEAD_DIM), F32),
            pltpu.VMEM((2 * tq, HEAD_DIM), F32),
        ],
        compiler_params=_params("parallel", "parallel", "arbitrary"),
        name="diff_attention",
    )(lq1, lk1, lq2, lk2, z3, z3, z3, gain_b)


def _out_proj_kernel(x_ref, ma_ref, mb_ref, mc_ref, w_ref, o_ref, mix_ref):
    mix_ref[:, :GMLP_WIDTH] = ma_ref[...]
    mix_ref[:, GMLP_WIDTH:GMLP_WIDTH + DIFF_WIDTH] = mb_ref[...]
    mix_ref[:, GMLP_WIDTH + DIFF_WIDTH:] = mc_ref[...]
    o_ref[...] = x_ref[...] + jnp.dot(mix_ref[...], w_ref[...], preferred_element_type=F32)


def _out_proj(x2, mix_a, mix_b, mix_c, w):
    m, d = x2.shape
    tm = OUT_PROJ_TM
    assert m % tm == 0
    return pl.pallas_call(
        _out_proj_kernel,
        grid=(m // tm,),
        in_specs=[
            pl.BlockSpec((tm, d), lambda i: (i, 0)),
            pl.BlockSpec((tm, GMLP_WIDTH), lambda i: (i, 0)),
            pl.BlockSpec((tm, DIFF_WIDTH), lambda i: (i, 0)),
            pl.BlockSpec((tm, CONV_WIDTH), lambda i: (i, 0)),
            pl.BlockSpec(w.shape, lambda i: (0, 0)),
        ],
        out_specs=pl.BlockSpec((tm, d), lambda i: (i, 0)),
        out_shape=jax.ShapeDtypeStruct((m, d), F32),
        scratch_shapes=[pltpu.VMEM((tm, w.shape[0]), BF16)],
        compiler_params=_params("parallel"),
        name="out_proj",
    )(x2, mix_a, mix_b, mix_c, w)


def _ffn_kernel(x_ref, g_ref, wg_ref, wu_ref, wd_ref, fg_ref, o_ref, h_ref, *, final_norm):
    f = pl.program_id(1)

    @pl.when(f == 0)
    def _():
        x = x_ref[...]
        h_ref[...] = _rms_rows(x, g_ref[...]).astype(BF16)
        o_ref[...] = x

    h = h_ref[...]
    gate = jnp.dot(h, wg_ref[...], preferred_element_type=F32)
    up = jnp.dot(h, wu_ref[...], preferred_element_type=F32)
    act = (gate * jax.nn.sigmoid(gate) * up).astype(BF16)
    o_ref[...] += jnp.dot(act, wd_ref[...], preferred_element_type=F32)

    if final_norm:
        @pl.when(f == pl.num_programs(1) - 1)
        def _():
            o_ref[...] = _rms_rows(o_ref[...], fg_ref[...])


def _ffn(x2, gain, wg, wu, wd, final_gain, final_norm):
    m, d = x2.shape
    dff = wg.shape[1]
    tm, tf = FFN_TM, FFN_TF
    assert m % tm == 0 and dff % tf == 0
    return pl.pallas_call(
        functools.partial(_ffn_kernel, final_norm=final_norm),
        grid=(m // tm, dff // tf),
        in_specs=[
            pl.BlockSpec((tm, d), lambda i, f: (i, 0)),
            pl.BlockSpec((1, d), lambda i, f: (0, 0)),
            pl.BlockSpec((d, tf), lambda i, f: (0, f)),
            pl.BlockSpec((d, tf), lambda i, f: (0, f)),
            pl.BlockSpec((tf, d), lambda i, f: (f, 0)),
            pl.BlockSpec((1, d), lambda i, f: (0, 0)),
        ],
        out_specs=pl.BlockSpec((tm, d), lambda i, f: (i, 0)),
        out_shape=jax.ShapeDtypeStruct((m, d), F32),
        scratch_shapes=[pltpu.VMEM((tm, d), BF16)],
        compiler_params=_params("parallel", "arbitrary"),
        name="ffn",
    )(x2, gain, wg, wu, wd, final_gain)


def kernel(x, positions, attn_norm, w_in, gmlp_ln_g, gmlp_ln_b, gmlp_ws, gmlp_bs, lambda_q1, lambda_k1,
           lambda_q2, lambda_k2, conv_w, mix_norm, w_out, ffn_norm, w_gate, w_up, w_down, final_norm):
    b, s, d = x.shape
    m = b * s
    depth = w_in.shape[0]

    inv_freq = 1.0 / (ROPE_THETA ** (jnp.arange(0, HEAD_DIM, 2, dtype=F32) / HEAD_DIM))
    ang = positions.astype(F32).reshape(m, 1) * inv_freq
    cosf = jnp.concatenate([jnp.cos(ang), jnp.cos(ang)], axis=-1)
    sinf = jnp.concatenate([-jnp.sin(ang), jnp.sin(ang)], axis=-1)

    x2 = x.reshape(m, d)
    for l in range(depth):
        lam_init = 0.8 - 0.6 * math.exp(-0.3 * l)
        gain = mix_norm[l].astype(F32)
        z = _in_proj(x2, attn_norm[l].reshape(1, d), w_in[l].astype(BF16), cosf, sinf)
        z3 = z.reshape(b, s, z.shape[1])
        mix_a, mix_c = _local_mixers(
            z3, gmlp_ln_g[l], gmlp_ln_b[l], gmlp_ws[l], gmlp_bs[l].reshape(N_GROUPS, GMLP_CHUNK, 1), conv_w[l],
            gain[:GMLP_WIDTH].reshape(1, GMLP_WIDTH), gain[GMLP_WIDTH + DIFF_WIDTH:].reshape(1, CONV_WIDTH))
        mix_b = _diff_attention(
            z3, lambda_q1[l].reshape(1, HEAD_DIM), lambda_k1[l].reshape(1, HEAD_DIM),
            lambda_q2[l].reshape(1, HEAD_DIM), lambda_k2[l].reshape(1, HEAD_DIM),
            gain[GMLP_WIDTH:GMLP_WIDTH + DIFF_WIDTH].reshape(N_GROUPS, 1, DIFF_V_DIM), lam_init)
        x2 = _out_proj(x2, mix_a.reshape(m, GMLP_WIDTH), mix_b.reshape(m, DIFF_WIDTH),
                       mix_c.reshape(m, CONV_WIDTH), w_out[l].astype(BF16))
        x2 = _ffn(x2, ffn_norm[l].reshape(1, d), w_gate[l].astype(BF16), w_up[l].astype(BF16),
                  w_down[l].astype(BF16), final_norm.reshape(1, d), final_norm=(l == depth - 1))
    return x2.reshape(b, s, d)
```

```python
import functools
import math

import jax
import jax.numpy as jnp
from jax import lax
from jax.experimental import pallas as pl
from jax.experimental.pallas import tpu as pltpu

F32 = jnp.float32
BF16 = jnp.bfloat16

HEAD_DIM = 128
N_GROUPS = 4
GMLP_WIDTH = N_GROUPS * HEAD_DIM
GMLP_CHUNK = 128
DIFF_QK_WIDTH = N_GROUPS * 2 * HEAD_DIM
DIFF_V_DIM = 2 * HEAD_DIM
DIFF_WIDTH = N_GROUPS * DIFF_V_DIM
CONV_WIDTH = N_GROUPS * HEAD_DIM
CONV_K = 3
ROPE_THETA = 10000.0
RMS_EPS = 1e-6
LN_EPS = 1e-5
LOG2_E = math.log2(math.e)

Q_OFF = 2 * GMLP_WIDTH
K_OFF = Q_OFF + DIFF_QK_WIDTH
V_OFF = K_OFF + DIFF_QK_WIDTH
C_OFF = V_OFF + DIFF_WIDTH

VMEM_LIMIT_BYTES = 56 * 1024 * 1024

IN_PROJ_TM = 512
IN_PROJ_TN = 2816
LOCAL_ROWS = 512
ATTN_TQ = 512
ATTN_ROW_CHUNK = 32
OUT_PROJ_TM = 512
FFN_TM = 1024
FFN_TF = 512
CONV_HALO_ROWS = 8


def _params(*semantics):
    return pltpu.CompilerParams(dimension_semantics=semantics, vmem_limit_bytes=VMEM_LIMIT_BYTES)


def _rms_rows(x, gain):
    return x * lax.rsqrt(jnp.mean(x * x, axis=-1, keepdims=True) + RMS_EPS) * gain


def _group_rms(x):
    return x * lax.rsqrt(jnp.mean(x * x, axis=-1, keepdims=True) + RMS_EPS)


def _in_proj_kernel(x_ref, g_ref, w_ref, cos_ref, sin_ref, z_ref, h_ref, *, tn, n_col_tiles):
    j = pl.program_id(1)

    @pl.when(j == 0)
    def _():
        h_ref[...] = _rms_rows(x_ref[...], g_ref[...]).astype(BF16)

    def tile(col0):
        acc = jnp.dot(h_ref[...], w_ref[...], preferred_element_type=F32)
        cosf, sinf = cos_ref[...], sin_ref[...]
        is_rope = [Q_OFF <= col0 + c * HEAD_DIM < V_OFF for c in range(tn // HEAD_DIM)]
        c = 0
        while c < len(is_rope):
            if is_rope[c]:
                xc = acc[:, c * HEAD_DIM:(c + 1) * HEAD_DIM]
                r = xc * cosf + pltpu.roll(xc, HEAD_DIM // 2, 1) * sinf
                if col0 + c * HEAD_DIM < K_OFF:
                    r = r * (LOG2_E / math.sqrt(HEAD_DIM))
                z_ref[:, c * HEAD_DIM:(c + 1) * HEAD_DIM] = r.astype(z_ref.dtype)
                c += 1
            else:
                e = c
                while e < len(is_rope) and not is_rope[e]:
                    e += 1
                z_ref[:, c * HEAD_DIM:e * HEAD_DIM] = acc[:, c * HEAD_DIM:e * HEAD_DIM].astype(z_ref.dtype)
                c = e

    for jt in range(n_col_tiles):
        pl.when(j == jt)(functools.partial(tile, jt * tn))


def _in_proj(x2, gain, w, cosf, sinf):
    m, d = x2.shape
    n = w.shape[1]
    tm, tn = IN_PROJ_TM, IN_PROJ_TN
    assert m % tm == 0 and n % tn == 0 and tn % HEAD_DIM == 0
    return pl.pallas_call(
        functools.partial(_in_proj_kernel, tn=tn, n_col_tiles=n // tn),
        grid=(m // tm, n // tn),
        in_specs=[
            pl.BlockSpec((tm, d), lambda i, j: (i, 0)),
            pl.BlockSpec((1, d), lambda i, j: (0, 0)),
            pl.BlockSpec((d, tn), lambda i, j: (0, j)),
            pl.BlockSpec((tm, HEAD_DIM), lambda i, j: (i, 0)),
            pl.BlockSpec((tm, HEAD_DIM), lambda i, j: (i, 0)),
        ],
        out_specs=pl.BlockSpec((tm, tn), lambda i, j: (i, j)),
        out_shape=jax.ShapeDtypeStruct((m, n), BF16),
        scratch_shapes=[pltpu.VMEM((tm, d), BF16)],
        compiler_params=_params("parallel", "arbitrary"),
        name="in_proj",
    )(x2, gain, w, cosf, sinf)


def _gelu(x):
    return 0.5 * x * (1.0 + lax.erf(x * math.sqrt(0.5)))


def _local_kernel(za_ref, bg_ref, cg_ref, hc_ref, cgh_ref, hch_ref, lng_ref, lnb_ref, ws_ref, bs_ref,
                  cw_ref, ga_ref, gc_ref, oa_ref, oc_ref, *, rows):
    i = pl.program_id(1)

    t_idx = lax.broadcasted_iota(jnp.int32, (GMLP_CHUNK, GMLP_CHUNK), 0)
    s_idx = lax.broadcasted_iota(jnp.int32, (GMLP_CHUNK, GMLP_CHUNK), 1)
    for g in range(N_GROUPS):
        w = jnp.where(s_idx <= t_idx, ws_ref[g], 0.0).astype(BF16)
        bias = bs_ref[g]
        lo = g * HEAD_DIM
        for c in range(rows // GMLP_CHUNK):
            r0 = c * GMLP_CHUNK
            u = _gelu(za_ref[0, r0:r0 + GMLP_CHUNK, lo:lo + HEAD_DIM].astype(F32))
            v = _gelu(za_ref[0, r0:r0 + GMLP_CHUNK, GMLP_WIDTH + lo:GMLP_WIDTH + lo + HEAD_DIM].astype(F32))
            mu = jnp.mean(v, axis=-1, keepdims=True)
            vc = v - mu
            var = jnp.mean(vc * vc, axis=-1, keepdims=True)
            vn = vc * lax.rsqrt(var + LN_EPS) * lng_ref[g:g + 1, :] + lnb_ref[g:g + 1, :]
            mixed = jnp.dot(w, vn.astype(BF16), preferred_element_type=F32) + bias
            ya = _group_rms(u * mixed) * ga_ref[:, lo:lo + HEAD_DIM]
            oa_ref[0, r0:r0 + GMLP_CHUNK, lo:lo + HEAD_DIM] = ya.astype(oa_ref.dtype)

    xh = cg_ref[0].astype(F32) * hc_ref[0].astype(F32)
    halo = cgh_ref[0].astype(F32) * hch_ref[0].astype(F32)
    halo = jnp.where(i == 0, 0.0, halo)
    row = lax.broadcasted_iota(jnp.int32, xh.shape, 0)
    prev1 = jnp.where(row == 0, halo[CONV_HALO_ROWS - 1:CONV_HALO_ROWS, :], pltpu.roll(xh, 1, 0))
    prev2 = jnp.where(row == 0, halo[CONV_HALO_ROWS - 2:CONV_HALO_ROWS - 1, :],
                      jnp.where(row == 1, halo[CONV_HALO_ROWS - 1:CONV_HALO_ROWS, :], pltpu.roll(xh, 2, 0)))
    y = cw_ref[0:1, :] * prev2 + cw_ref[1:2, :] * prev1 + cw_ref[2:3, :] * xh
    yc = bg_ref[0].astype(F32) * y
    for g in range(N_GROUPS):
        lo = g * HEAD_DIM
        oc_ref[0, :, lo:lo + HEAD_DIM] = (_group_rms(yc[:, lo:lo + HEAD_DIM])
                                          * gc_ref[:, lo:lo + HEAD_DIM]).astype(oc_ref.dtype)


def _local_mixers(z3, ln_g, ln_b, ws, bs, conv_w, gain_a, gain_c):
    b, s, _ = z3.shape
    rows = LOCAL_ROWS
    assert s % rows == 0 and rows % GMLP_CHUNK == 0 and C_OFF % CONV_WIDTH == 0
    cb = C_OFF // CONV_WIDTH
    hb = rows // CONV_HALO_ROWS
    full = lambda shape: pl.BlockSpec(shape, lambda bi, i: (0,) * len(shape))
    halo_map = lambda col: (lambda bi, i: (bi, jnp.maximum(i * hb - 1, 0), col))
    return pl.pallas_call(
        functools.partial(_local_kernel, rows=rows),
        grid=(b, s // rows),
        in_specs=[
            pl.BlockSpec((1, rows, 2 * GMLP_WIDTH), lambda bi, i: (bi, i, 0)),
            pl.BlockSpec((1, rows, CONV_WIDTH), lambda bi, i: (bi, i, cb)),
            pl.BlockSpec((1, rows, CONV_WIDTH), lambda bi, i: (bi, i, cb + 1)),
            pl.BlockSpec((1, rows, CONV_WIDTH), lambda bi, i: (bi, i, cb + 2)),
            pl.BlockSpec((1, CONV_HALO_ROWS, CONV_WIDTH), halo_map(cb + 1)),
            pl.BlockSpec((1, CONV_HALO_ROWS, CONV_WIDTH), halo_map(cb + 2)),
            full((N_GROUPS, HEAD_DIM)),
            full((N_GROUPS, HEAD_DIM)),
            full((N_GROUPS, GMLP_CHUNK, GMLP_CHUNK)),
            full((N_GROUPS, GMLP_CHUNK, 1)),
            full((CONV_K, CONV_WIDTH)),
            full((1, GMLP_WIDTH)),
            full((1, CONV_WIDTH)),
        ],
        out_specs=[
            pl.BlockSpec((1, rows, GMLP_WIDTH), lambda bi, i: (bi, i, 0)),
            pl.BlockSpec((1, rows, CONV_WIDTH), lambda bi, i: (bi, i, 0)),
        ],
        out_shape=[jax.ShapeDtypeStruct((b, s, GMLP_WIDTH), BF16),
                   jax.ShapeDtypeStruct((b, s, CONV_WIDTH), BF16)],
        compiler_params=_params("parallel", "arbitrary"),
        name="local_mixers",
    )(z3, z3, z3, z3, z3, z3, ln_g, ln_b, ws, bs, conv_w, gain_a, gain_c)


def _attn_kernel(lq1_ref, lk1_ref, lq2_ref, lk2_ref, q_ref, k_ref, v_ref, gain_ref, o_ref,
                 qs_ref, m_ref, l_ref, acc_ref, s0_ref, s1_ref, p0_ref, p1_ref, a0_ref, a1_ref, *, tq, lam_init):
    qi = pl.program_id(2)
    rc = ATTN_ROW_CHUNK
    n_tiles = tq // HEAD_DIM
    s_bufs, p_bufs, a_bufs = (s0_ref, s1_ref), (p0_ref, p1_ref), (a0_ref, a1_ref)

    q = q_ref[0]
    zero = jnp.zeros((tq, HEAD_DIM), q.dtype)
    qs_ref[:tq, :HEAD_DIM] = q[:, :HEAD_DIM]
    qs_ref[:tq, HEAD_DIM:] = zero
    qs_ref[tq:, :HEAD_DIM] = zero
    qs_ref[tq:, HEAD_DIM:] = q[:, HEAD_DIM:]
    m_ref[...] = jnp.full(m_ref.shape, -jnp.inf, F32)
    l_ref[...] = jnp.zeros(l_ref.shape, F32)
    acc_ref[...] = jnp.zeros(acc_ref.shape, F32)

    def key_rows(j):
        return pl.ds(pl.multiple_of(j * tq, tq), tq)

    def scores(j, buf):
        s_bufs[buf][...] = lax.dot_general(qs_ref[...], k_ref[0, key_rows(j), :], (((1,), (1,)), ((), ())),
                                           preferred_element_type=F32)

    def softmax(buf, diagonal):
        s_ref, p_ref, alpha_ref = s_bufs[buf], p_bufs[buf], a_bufs[buf]
        for c in range(2 * tq // rc):
            r0 = c * rc
            q0 = r0 % tq
            live = [t for t in range(n_tiles) if (not diagonal) or t * HEAD_DIM <= q0 + rc - 1]
            tiles = []
            for t in live:
                st = s_ref[r0:r0 + rc, t * HEAD_DIM:(t + 1) * HEAD_DIM]
                if diagonal and (t + 1) * HEAD_DIM - 1 > q0:
                    row = lax.broadcasted_iota(jnp.int32, (rc, HEAD_DIM), 0) + q0
                    col = lax.broadcasted_iota(jnp.int32, (rc, HEAD_DIM), 1) + t * HEAD_DIM
                    st = jnp.where(col <= row, st, -jnp.inf)
                tiles.append(st)
            m_old = m_ref[r0:r0 + rc, :]
            m_new = jnp.maximum(m_old, jnp.max(functools.reduce(jnp.maximum, tiles), axis=-1, keepdims=True))
            alpha = jnp.exp2(m_old - m_new)
            ps = [jnp.exp2(st - m_new) for st in tiles]
            l_ref[r0:r0 + rc, :] = alpha * l_ref[r0:r0 + rc, :] + functools.reduce(jnp.add, ps)
            m_ref[r0:r0 + rc, :] = m_new
            alpha_ref[r0:r0 + rc, :] = alpha
            for t in range(n_tiles):
                p_t = ps[live.index(t)].astype(BF16) if t in live else jnp.zeros((rc, HEAD_DIM), BF16)
                p_ref[r0:r0 + rc, t * HEAD_DIM:(t + 1) * HEAD_DIM] = p_t

    def accumulate(j, buf):
        pv = jnp.dot(p_bufs[buf][...], v_ref[0, key_rows(j), :], preferred_element_type=F32)
        alpha = a_bufs[buf][...]
        for t in range(DIFF_V_DIM // HEAD_DIM):
            cols = slice(t * HEAD_DIM, (t + 1) * HEAD_DIM)
            acc_ref[:, cols] = acc_ref[:, cols] * alpha + pv[:, cols]

    def stage(j, buf):
        scores(j + 1, 1 - buf)
        accumulate(j - 1, 1 - buf)
        softmax(buf, diagonal=False)

    scores(0, 0)

    @pl.when(qi >= 1)
    def _():
        scores(1, 1)
        softmax(0, diagonal=False)

    def pair(t, carry):
        stage(2 * t + 1, 1)
        stage(2 * t + 2, 0)
        return carry

    lax.fori_loop(0, (qi - 1) // 2, pair, 0)

    @pl.when((qi >= 2) & (qi % 2 == 0))
    def _():
        stage(qi - 1, 1)

    @pl.when(qi == 0)
    def _():
        softmax(0, diagonal=True)
        accumulate(0, 0)

    @pl.when((qi >= 1) & (qi % 2 == 0))
    def _():
        accumulate(qi - 1, 1)
        softmax(0, diagonal=True)
        accumulate(qi, 0)

    @pl.when(qi % 2 == 1)
    def _():
        accumulate(qi - 1, 0)
        softmax(1, diagonal=True)
        accumulate(qi, 1)

    lam = (jnp.exp(jnp.sum(lq1_ref[...] * lk1_ref[...], axis=-1, keepdims=True))
           - jnp.exp(jnp.sum(lq2_ref[...] * lk2_ref[...], axis=-1, keepdims=True)) + lam_init)
    o = acc_ref[...] / jnp.sum(l_ref[...], axis=-1, keepdims=True)
    o = o[:tq] - lam * o[tq:]
    o = _group_rms(o) * (1.0 - lam_init) * gain_ref[0]
    o_ref[0] = o.astype(o_ref.dtype)


def _diff_attention(z3, lq1, lk1, lq2, lk2, gain_b, lam_init):
    b, s, _ = z3.shape
    tq = ATTN_TQ
    assert s % tq == 0
    qb, kb, vb = Q_OFF // DIFF_V_DIM, K_OFF // DIFF_V_DIM, V_OFF // DIFF_V_DIM
    vec = pl.BlockSpec((1, HEAD_DIM), lambda bi, h, i: (0, 0))
    return pl.pallas_call(
        functools.partial(_attn_kernel, tq=tq, lam_init=lam_init),
        grid=(b, N_GROUPS, s // tq),
        in_specs=[
            vec, vec, vec, vec,
            pl.BlockSpec((1, tq, DIFF_V_DIM), lambda bi, h, i: (bi, i, qb + h)),
            pl.BlockSpec((1, s, DIFF_V_DIM), lambda bi, h, i: (bi, 0, kb + h)),
            pl.BlockSpec((1, s, DIFF_V_DIM), lambda bi, h, i: (bi, 0, vb + h)),
            pl.BlockSpec((1, 1, DIFF_V_DIM), lambda bi, h, i: (h, 0, 0)),
        ],
        out_specs=pl.BlockSpec((1, tq, DIFF_V_DIM), lambda bi, h, i: (bi, i, h)),
        out_shape=jax.ShapeDtypeStruct((b, s, DIFF_WIDTH), BF16),
        scratch_shapes=[
            pltpu.VMEM((2 * tq, 2 * HEAD_DIM), BF16),
            pltpu.VMEM((2 * tq, HEAD_DIM), F32),
            pltpu.VMEM((2 * tq, HEAD_DIM), F32),
            pltpu.VMEM((2 * tq, DIFF_V_DIM), F32),
            pltpu.VMEM((2 * tq, tq), F32),
            pltpu.VMEM((2 * tq, tq), F32),
            pltpu.VMEM((2 * tq, tq), BF16),
            pltpu.VMEM((2 * tq, tq), BF16),
            pltpu.VMEM((2 * tq, HEAD_DIM), F32),
            pltpu.VMEM((2 * tq, HEAD_DIM), F32),
        ],
        compiler_params=_params("parallel", "parallel", "arbitrary"),
        name="diff_attention",
    )(lq1, lk1, lq2, lk2, z3, z3, z3, gain_b)


def _out_proj_kernel(x_ref, ma_ref, mb_ref, mc_ref, w_ref, o_ref, mix_ref):
    mix_ref[:, :GMLP_WIDTH] = ma_ref[...]
    mix_ref[:, GMLP_WIDTH:GMLP_WIDTH + DIFF_WIDTH] = mb_ref[...]
    mix_ref[:, GMLP_WIDTH + DIFF_WIDTH:] = mc_ref[...]
    o_ref[...] = x_ref[...] + jnp.dot(mix_ref[...], w_ref[...], preferred_element_type=F32)


def _out_proj(x2, mix_a, mix_b, mix_c, w):
    m, d = x2.shape
    tm = OUT_PROJ_TM
    assert m % tm == 0
    return pl.pallas_call(
        _out_proj_kernel,
        grid=(m // tm,),
        in_specs=[
            pl.BlockSpec((tm, d), lambda i: (i, 0)),
            pl.BlockSpec((tm, GMLP_WIDTH), lambda i: (i, 0)),
            pl.BlockSpec((tm, DIFF_WIDTH), lambda i: (i, 0)),
            pl.BlockSpec((tm, CONV_WIDTH), lambda i: (i, 0)),
            pl.BlockSpec(w.shape, lambda i: (0, 0)),
        ],
        out_specs=pl.BlockSpec((tm, d), lambda i: (i, 0)),
        out_shape=jax.ShapeDtypeStruct((m, d), F32),
        scratch_shapes=[pltpu.VMEM((tm, w.shape[0]), BF16)],
        compiler_params=_params("parallel"),
        name="out_proj",
    )(x2, mix_a, mix_b, mix_c, w)


def _ffn_kernel(x_ref, g_ref, wg_ref, wu_ref, wd_ref, fg_ref, o_ref, h_ref, *, final_norm):
    f = pl.program_id(1)

    @pl.when(f == 0)
    def _():
        x = x_ref[...]
        h_ref[...] = _rms_rows(x, g_ref[...]).astype(BF16)
        o_ref[...] = x

    h = h_ref[...]
    gate = jnp.dot(h, wg_ref[...], preferred_element_type=F32)
    up = jnp.dot(h, wu_ref[...], preferred_element_type=F32)
    act = (gate * jax.nn.sigmoid(gate) * up).astype(BF16)
    o_ref[...] += jnp.dot(act, wd_ref[...], preferred_element_type=F32)

    if final_norm:
        @pl.when(f == pl.num_programs(1) - 1)
        def _():
            o_ref[...] = _rms_rows(o_ref[...], fg_ref[...])


def _ffn(x2, gain, wg, wu, wd, final_gain, final_norm):
    m, d = x2.shape
    dff = wg.shape[1]
    tm, tf = FFN_TM, FFN_TF
    assert m % tm == 0 and dff % tf == 0
    return pl.pallas_call(
        functools.partial(_ffn_kernel, final_norm=final_norm),
        grid=(m // tm, dff // tf),
        in_specs=[
            pl.BlockSpec((tm, d), lambda i, f: (i, 0)),
            pl.BlockSpec((1, d), lambda i, f: (0, 0)),
            pl.BlockSpec((d, tf), lambda i, f: (0, f)),
            pl.BlockSpec((d, tf), lambda i, f: (0, f)),
            pl.BlockSpec((tf, d), lambda i, f: (f, 0)),
            pl.BlockSpec((1, d), lambda i, f: (0, 0)),
        ],
        out_specs=pl.BlockSpec((tm, d), lambda i, f: (i, 0)),
        out_shape=jax.ShapeDtypeStruct((m, d), F32),
        scratch_shapes=[pltpu.VMEM((tm, d), BF16)],
        compiler_params=_params("parallel", "arbitrary"),
        name="ffn",
    )(x2, gain, wg, wu, wd, final_gain)


def kernel(x, positions, attn_norm, w_in, gmlp_ln_g, gmlp_ln_b, gmlp_ws, gmlp_bs, lambda_q1, lambda_k1,
           lambda_q2, lambda_k2, conv_w, mix_norm, w_out, ffn_norm, w_gate, w_up, w_down, final_norm):
    b, s, d = x.shape
    m = b * s
    depth = w_in.shape[0]

    inv_freq = 1.0 / (ROPE_THETA ** (jnp.arange(0, HEAD_DIM, 2, dtype=F32) / HEAD_DIM))
    ang = positions.astype(F32).reshape(m, 1) * inv_freq
    cosf = jnp.concatenate([jnp.cos(ang), jnp.cos(ang)], axis=-1)
    sinf = jnp.concatenate([-jnp.sin(ang), jnp.sin(ang)], axis=-1)

    x2 = x.reshape(m, d)
    for l in range(depth):
        lam_init = 0.8 - 0.6 * math.exp(-0.3 * l)
        gain = mix_norm[l].astype(F32)
        z = _in_proj(x2, attn_norm[l].reshape(1, d), w_in[l].astype(BF16), cosf, sinf)
        z3 = z.reshape(b, s, z.shape[1])
        mix_a, mix_c = _local_mixers(
            z3, gmlp_ln_g[l], gmlp_ln_b[l], gmlp_ws[l], gmlp_bs[l].reshape(N_GROUPS, GMLP_CHUNK, 1), conv_w[l],
            gain[:GMLP_WIDTH].reshape(1, GMLP_WIDTH), gain[GMLP_WIDTH + DIFF_WIDTH:].reshape(1, CONV_WIDTH))
        mix_b = _diff_attention(
            z3, lambda_q1[l].reshape(1, HEAD_DIM), lambda_k1[l].reshape(1, HEAD_DIM),
            lambda_q2[l].reshape(1, HEAD_DIM), lambda_k2[l].reshape(1, HEAD_DIM),
            gain[GMLP_WIDTH:GMLP_WIDTH + DIFF_WIDTH].reshape(N_GROUPS, 1, DIFF_V_DIM), lam_init)
        x2 = _out_proj(x2, mix_a.reshape(m, GMLP_WIDTH), mix_b.reshape(m, DIFF_WIDTH),
                       mix_c.reshape(m, CONV_WIDTH), w_out[l].astype(BF16))
        x2 = _ffn(x2, ffn_norm[l].reshape(1, d), w_gate[l].astype(BF16), w_up[l].astype(BF16),
                  w_down[l].astype(BF16), final_norm.reshape(1, d), final_norm=(l == depth - 1))
    return x2.reshape(b, s, d)
```

```python
import functools
import math

import jax
import jax.numpy as jnp
from jax import lax
from jax.experimental import pallas as pl
from jax.experimental.pallas import tpu as pltpu

F32 = jnp.float32
BF16 = jnp.bfloat16

HEAD_DIM = 128
N_GROUPS = 4
GMLP_WIDTH = N_GROUPS * HEAD_DIM
GMLP_CHUNK = 128
DIFF_QK_WIDTH = N_GROUPS * 2 * HEAD_DIM
DIFF_V_DIM = 2 * HEAD_DIM
DIFF_WIDTH = N_GROUPS * DIFF_V_DIM
CONV_WIDTH = N_GROUPS * HEAD_DIM
CONV_K = 3
ROPE_THETA = 10000.0
RMS_EPS = 1e-6
LN_EPS = 1e-5
LOG2_E = math.log2(math.e)

Q_OFF = 2 * GMLP_WIDTH
K_OFF = Q_OFF + DIFF_QK_WIDTH
V_OFF = K_OFF + DIFF_QK_WIDTH
C_OFF = V_OFF + DIFF_WIDTH

VMEM_LIMIT_BYTES = 56 * 1024 * 1024

IN_PROJ_TM = 512
IN_PROJ_TN = 2816
LOCAL_ROWS = 512
ATTN_TQ = 512
ATTN_ROW_CHUNK = 32
OUT_PROJ_TM = 512
FFN_TM = 1024
FFN_TF = 512
CONV_HALO_ROWS = 8


def _params(*semantics):
    return pltpu.CompilerParams(dimension_semantics=semantics, vmem_limit_bytes=VMEM_LIMIT_BYTES)


def _rms_rows(x, gain):
    return x * lax.rsqrt(jnp.mean(x * x, axis=-1, keepdims=True) + RMS_EPS) * gain


def _group_rms(x):
    return x * lax.rsqrt(jnp.mean(x * x, axis=-1, keepdims=True) + RMS_EPS)


def _in_proj_kernel(x_ref, g_ref, w_ref, cos_ref, sin_ref, z_ref, h_ref, *, tn, n_col_tiles):
    j = pl.program_id(1)

    @pl.when(j == 0)
    def _():
        h_ref[...] = _rms_rows(x_ref[...], g_ref[...]).astype(BF16)

    def tile(col0):
        acc = jnp.dot(h_ref[...], w_ref[...], preferred_element_type=F32)
        cosf, sinf = cos_ref[...], sin_ref[...]
        is_rope = [Q_OFF <= col0 + c * HEAD_DIM < V_OFF for c in range(tn // HEAD_DIM)]
        c = 0
        while c < len(is_rope):
            if is_rope[c]:
                xc = acc[:, c * HEAD_DIM:(c + 1) * HEAD_DIM]
                r = xc * cosf + pltpu.roll(xc, HEAD_DIM // 2, 1) * sinf
                if col0 + c * HEAD_DIM < K_OFF:
                    r = r * (LOG2_E / math.sqrt(HEAD_DIM))
                z_ref[:, c * HEAD_DIM:(c + 1) * HEAD_DIM] = r.astype(z_ref.dtype)
                c += 1
            else:
                e = c
                while e < len(is_rope) and not is_rope[e]:
                    e += 1
                z_ref[:, c * HEAD_DIM:e * HEAD_DIM] = acc[:, c * HEAD_DIM:e * HEAD_DIM].astype(z_ref.dtype)
                c = e

    for jt in range(n_col_tiles):
        pl.when(j == jt)(functools.partial(tile, jt * tn))


def _in_proj(x2, gain, w, layer, cosf, sinf):
    m, d = x2.shape
    n = w.shape[2]
    tm, tn = IN_PROJ_TM, IN_PROJ_TN
    assert m % tm == 0 and n % tn == 0 and tn % HEAD_DIM == 0
    return pl.pallas_call(
        functools.partial(_in_proj_kernel, tn=tn, n_col_tiles=n // tn),
        grid=(m // tm, n // tn),
        in_specs=[
            pl.BlockSpec((tm, d), lambda i, j: (i, 0)),
            pl.BlockSpec((1, d), lambda i, j: (0, 0)),
            pl.BlockSpec((None, d, tn), lambda i, j: (layer, 0, j)),
            pl.BlockSpec((tm, HEAD_DIM), lambda i, j: (i, 0)),
            pl.BlockSpec((tm, HEAD_DIM), lambda i, j: (i, 0)),
        ],
        out_specs=pl.BlockSpec((tm, tn), lambda i, j: (i, j)),
        out_shape=jax.ShapeDtypeStruct((m, n), BF16),
        scratch_shapes=[pltpu.VMEM((tm, d), BF16)],
        compiler_params=_params("parallel", "arbitrary"),
        name="in_proj",
    )(x2, gain, w, cosf, sinf)


def _gelu(x):
    return 0.5 * x * (1.0 + lax.erf(x * math.sqrt(0.5)))


def _local_kernel(za_ref, bg_ref, cg_ref, hc_ref, cgh_ref, hch_ref, lng_ref, lnb_ref, ws_ref, bs_ref,
                  cw_ref, ga_ref, gc_ref, oa_ref, oc_ref, vn_ref, mixed_ref, *, rows):
    i = pl.program_id(1)

    groups = [slice(g * HEAD_DIM, (g + 1) * HEAD_DIM) for g in range(N_GROUPS)]
    for g, cols in enumerate(groups):
        v = _gelu(za_ref[0, :, GMLP_WIDTH + cols.start:GMLP_WIDTH + cols.stop].astype(F32))
        mu = jnp.mean(v, axis=-1, keepdims=True)
        vc = v - mu
        var = jnp.mean(vc * vc, axis=-1, keepdims=True)
        vn = vc * lax.rsqrt(var + LN_EPS) * lng_ref[g:g + 1, :] + lnb_ref[g:g + 1, :]
        vn_ref[:, cols] = vn.astype(BF16)
    t_idx = lax.broadcasted_iota(jnp.int32, (GMLP_CHUNK, GMLP_CHUNK), 0)
    s_idx = lax.broadcasted_iota(jnp.int32, (GMLP_CHUNK, GMLP_CHUNK), 1)
    for g, cols in enumerate(groups):
        w = jnp.where(s_idx <= t_idx, ws_ref[g], 0.0).astype(BF16)
        for c in range(rows // GMLP_CHUNK):
            chunk = slice(c * GMLP_CHUNK, (c + 1) * GMLP_CHUNK)
            mixed_ref[chunk, cols] = jnp.dot(w, vn_ref[chunk, cols], preferred_element_type=F32)
    for g, cols in enumerate(groups):
        u = _gelu(za_ref[0, :, cols].astype(F32))
        ya = _group_rms(u * (mixed_ref[:, cols] + bs_ref[g])) * ga_ref[:, cols]
        oa_ref[0, :, cols] = ya.astype(oa_ref.dtype)

    xh = cg_ref[0].astype(F32) * hc_ref[0].astype(F32)
    halo = cgh_ref[0].astype(F32) * hch_ref[0].astype(F32)
    halo = jnp.where(i == 0, 0.0, halo)
    row = lax.broadcasted_iota(jnp.int32, xh.shape, 0)
    prev1 = jnp.where(row == 0, halo[CONV_HALO_ROWS - 1:CONV_HALO_ROWS, :], pltpu.roll(xh, 1, 0))
    prev2 = jnp.where(row == 0, halo[CONV_HALO_ROWS - 2:CONV_HALO_ROWS - 1, :],
                      jnp.where(row == 1, halo[CONV_HALO_ROWS - 1:CONV_HALO_ROWS, :], pltpu.roll(xh, 2, 0)))
    y = cw_ref[0:1, :] * prev2 + cw_ref[1:2, :] * prev1 + cw_ref[2:3, :] * xh
    yc = bg_ref[0].astype(F32) * y
    for g in range(N_GROUPS):
        lo = g * HEAD_DIM
        oc_ref[0, :, lo:lo + HEAD_DIM] = (_group_rms(yc[:, lo:lo + HEAD_DIM])
                                          * gc_ref[:, lo:lo + HEAD_DIM]).astype(oc_ref.dtype)


def _local_mixers(z3, ln_g, ln_b, ws, bs, conv_w, gain_a, gain_c):
    b, s, _ = z3.shape
    rows = LOCAL_ROWS
    assert s % rows == 0 and rows % GMLP_CHUNK == 0 and C_OFF % CONV_WIDTH == 0
    cb = C_OFF // CONV_WIDTH
    hb = rows // CONV_HALO_ROWS
    full = lambda shape: pl.BlockSpec(shape, lambda bi, i: (0,) * len(shape))
    halo_map = lambda col: (lambda bi, i: (bi, jnp.maximum(i * hb - 1, 0), col))
    return pl.pallas_call(
        functools.partial(_local_kernel, rows=rows),
        grid=(b, s // rows),
        in_specs=[
            pl.BlockSpec((1, rows, 2 * GMLP_WIDTH), lambda bi, i: (bi, i, 0)),
            pl.BlockSpec((1, rows, CONV_WIDTH), lambda bi, i: (bi, i, cb)),
            pl.BlockSpec((1, rows, CONV_WIDTH), lambda bi, i: (bi, i, cb + 1)),
            pl.BlockSpec((1, rows, CONV_WIDTH), lambda bi, i: (bi, i, cb + 2)),
            pl.BlockSpec((1, CONV_HALO_ROWS, CONV_WIDTH), halo_map(cb + 1)),
            pl.BlockSpec((1, CONV_HALO_ROWS, CONV_WIDTH), halo_map(cb + 2)),
            full((N_GROUPS, HEAD_DIM)),
            full((N_GROUPS, HEAD_DIM)),
            full((N_GROUPS, GMLP_CHUNK, GMLP_CHUNK)),
            full((N_GROUPS, rows, 1)),
            full((CONV_K, CONV_WIDTH)),
            full((1, GMLP_WIDTH)),
            full((1, CONV_WIDTH)),
        ],
        out_specs=[
            pl.BlockSpec((1, rows, GMLP_WIDTH), lambda bi, i: (bi, i, 0)),
            pl.BlockSpec((1, rows, CONV_WIDTH), lambda bi, i: (bi, i, 0)),
        ],
        out_shape=[jax.ShapeDtypeStruct((b, s, GMLP_WIDTH), BF16),
                   jax.ShapeDtypeStruct((b, s, CONV_WIDTH), BF16)],
        scratch_shapes=[pltpu.VMEM((rows, GMLP_WIDTH), BF16),
                        pltpu.VMEM((rows, GMLP_WIDTH), F32)],
        compiler_params=_params("parallel", "arbitrary"),
        name="local_mixers",
    )(z3, z3, z3, z3, z3, z3, ln_g, ln_b, ws, bs, conv_w, gain_a, gain_c)


def _attn_kernel(lq1_ref, lk1_ref, lq2_ref, lk2_ref, q_ref, k_ref, v_ref, gain_ref, o_ref,
                 qs_ref, m_ref, l_ref, acc_ref, s0_ref, s1_ref, p0_ref, p1_ref, a0_ref, a1_ref, *, tq, lam_init):
    qi = pl.program_id(2)
    rc = ATTN_ROW_CHUNK
    n_tiles = tq // HEAD_DIM
    s_bufs, p_bufs, a_bufs = (s0_ref, s1_ref), (p0_ref, p1_ref), (a0_ref, a1_ref)

    q = q_ref[0]
    zero = jnp.zeros((tq, HEAD_DIM), q.dtype)
    qs_ref[:tq, :HEAD_DIM] = q[:, :HEAD_DIM]
    qs_ref[:tq, HEAD_DIM:] = zero
    qs_ref[tq:, :HEAD_DIM] = zero
    qs_ref[tq:, HEAD_DIM:] = q[:, HEAD_DIM:]
    m_ref[...] = jnp.full(m_ref.shape, -jnp.inf, F32)
    l_ref[...] = jnp.zeros(l_ref.shape, F32)
    acc_ref[...] = jnp.zeros(acc_ref.shape, F32)

    def key_rows(j):
        return pl.ds(pl.multiple_of(j * tq, tq), tq)

    def scores(j, buf):
        s_bufs[buf][...] = lax.dot_general(qs_ref[...], k_ref[0, key_rows(j), :], (((1,), (1,)), ((), ())),
                                           preferred_element_type=F32)

    def softmax(buf, diagonal):
        s_ref, p_ref, alpha_ref = s_bufs[buf], p_bufs[buf], a_bufs[buf]
        for c in range(2 * tq // rc):
            r0 = c * rc
            q0 = r0 % tq
            live = [t for t in range(n_tiles) if (not diagonal) or t * HEAD_DIM <= q0 + rc - 1]
            tiles = []
            for t in live:
                st = s_ref[r0:r0 + rc, t * HEAD_DIM:(t + 1) * HEAD_DIM]
                if diagonal and (t + 1) * HEAD_DIM - 1 > q0:
                    row = lax.broadcasted_iota(jnp.int32, (rc, HEAD_DIM), 0) + q0
                    col = lax.broadcasted_iota(jnp.int32, (rc, HEAD_DIM), 1) + t * HEAD_DIM
                    st = jnp.where(col <= row, st, -jnp.inf)
                tiles.append(st)
            m_old = m_ref[r0:r0 + rc, :]
            m_new = jnp.maximum(m_old, jnp.max(functools.reduce(jnp.maximum, tiles), axis=-1, keepdims=True))
            alpha = jnp.exp2(m_old - m_new)
            ps = [jnp.exp2(st - m_new) for st in tiles]
            l_ref[r0:r0 + rc, :] = alpha * l_ref[r0:r0 + rc, :] + functools.reduce(jnp.add, ps)
            m_ref[r0:r0 + rc, :] = m_new
            alpha_ref[r0:r0 + rc, :] = alpha
            for t in range(n_tiles):
                p_t = ps[live.index(t)].astype(BF16) if t in live else jnp.zeros((rc, HEAD_DIM), BF16)
                p_ref[r0:r0 + rc, t * HEAD_DIM:(t + 1) * HEAD_DIM] = p_t

    def accumulate(j, buf):
        pv = jnp.dot(p_bufs[buf][...], v_ref[0, key_rows(j), :], preferred_element_type=F32)
        alpha = a_bufs[buf][...]
        for t in range(DIFF_V_DIM // HEAD_DIM):
            cols = slice(t * HEAD_DIM, (t + 1) * HEAD_DIM)
            acc_ref[:, cols] = acc_ref[:, cols] * alpha + pv[:, cols]

    def stage(j, buf):
        scores(j + 1, 1 - buf)
        accumulate(j - 1, 1 - buf)
        softmax(buf, diagonal=False)

    scores(0, 0)

    @pl.when(qi >= 1)
    def _():
        scores(1, 1)
        softmax(0, diagonal=False)

    def pair(t, carry):
        stage(2 * t + 1, 1)
        stage(2 * t + 2, 0)
        return carry

    lax.fori_loop(0, (qi - 1) // 2, pair, 0)

    @pl.when((qi >= 2) & (qi % 2 == 0))
    def _():
        stage(qi - 1, 1)

    @pl.when(qi == 0)
    def _():
        softmax(0, diagonal=True)
        accumulate(0, 0)

    @pl.when((qi >= 1) & (qi % 2 == 0))
    def _():
        accumulate(qi - 1, 1)
        softmax(0, diagonal=True)
        accumulate(qi, 0)

    @pl.when(qi % 2 == 1)
    def _():
        accumulate(qi - 1, 0)
        softmax(1, diagonal=True)
        accumulate(qi, 1)

    lam = (jnp.exp(jnp.sum(lq1_ref[...] * lk1_ref[...], axis=-1, keepdims=True))
           - jnp.exp(jnp.sum(lq2_ref[...] * lk2_ref[...], axis=-1, keepdims=True)) + lam_init)
    o = acc_ref[...] / jnp.sum(l_ref[...], axis=-1, keepdims=True)
    o = o[:tq] - lam * o[tq:]
    o = _group_rms(o) * (1.0 - lam_init) * gain_ref[0]
    o_ref[0] = o.astype(o_ref.dtype)


def _diff_attention(z3, lq1, lk1, lq2, lk2, gain_b, lam_init):
    b, s, _ = z3.shape
    tq = ATTN_TQ
    assert s % tq == 0
    qb, kb, vb = Q_OFF // DIFF_V_DIM, K_OFF // DIFF_V_DIM, V_OFF // DIFF_V_DIM
    vec = pl.BlockSpec((1, HEAD_DIM), lambda bi, h, i: (0, 0))
    return pl.pallas_call(
        functools.partial(_attn_kernel, tq=tq, lam_init=lam_init),
        grid=(b, N_GROUPS, s // tq),
        in_specs=[
            vec, vec, vec, vec,
            pl.BlockSpec((1, tq, DIFF_V_DIM), lambda bi, h, i: (bi, i, qb + h)),
            pl.BlockSpec((1, s, DIFF_V_DIM), lambda bi, h, i: (bi, 0, kb + h)),
            pl.BlockSpec((1, s, DIFF_V_DIM), lambda bi, h, i: (bi, 0, vb + h)),
            pl.BlockSpec((1, 1, DIFF_V_DIM), lambda bi, h, i: (h, 0, 0)),
        ],
        out_specs=pl.BlockSpec((1, tq, DIFF_V_DIM), lambda bi, h, i: (bi, i, h)),
        out_shape=jax.ShapeDtypeStruct((b, s, DIFF_WIDTH), BF16),
        scratch_shapes=[
            pltpu.VMEM((2 * tq, 2 * HEAD_DIM), BF16),
            pltpu.VMEM((2 * tq, HEAD_DIM), F32),
            pltpu.VMEM((2 * tq, HEAD_DIM), F32),
            pltpu.VMEM((2 * tq, DIFF_V_DIM), F32),
            pltpu.VMEM((2 * tq, tq), F32),
            pltpu.VMEM((2 * tq, tq), F32),
            pltpu.VMEM((2 * tq, tq), BF16),
            pltpu.VMEM((2 * tq, tq), BF16),
            pltpu.VMEM((2 * tq, HEAD_DIM), F32),
            pltpu.VMEM((2 * tq, HEAD_DIM), F32),
        ],
        compiler_params=_params("parallel", "parallel", "arbitrary"),
        name="diff_attention",
    )(lq1, lk1, lq2, lk2, z3, z3, z3, gain_b)


def _out_proj_kernel(x_ref, ma_ref, mb_ref, mc_ref, w_ref, o_ref, mix_ref):
    mix_ref[:, :GMLP_WIDTH] = ma_ref[...]
    mix_ref[:, GMLP_WIDTH:GMLP_WIDTH + DIFF_WIDTH] = mb_ref[...]
    mix_ref[:, GMLP_WIDTH + DIFF_WIDTH:] = mc_ref[...]
    o_ref[...] = x_ref[...] + jnp.dot(mix_ref[...], w_ref[...], preferred_element_type=F32)


def _out_proj(x2, mix_a, mix_b, mix_c, w, layer):
    m, d = x2.shape
    tm = OUT_PROJ_TM
    assert m % tm == 0
    return pl.pallas_call(
        _out_proj_kernel,
        grid=(m // tm,),
        in_specs=[
            pl.BlockSpec((tm, d), lambda i: (i, 0)),
            pl.BlockSpec((tm, GMLP_WIDTH), lambda i: (i, 0)),
            pl.BlockSpec((tm, DIFF_WIDTH), lambda i: (i, 0)),
            pl.BlockSpec((tm, CONV_WIDTH), lambda i: (i, 0)),
            pl.BlockSpec((None,) + w.shape[1:], lambda i: (layer, 0, 0)),
        ],
        out_specs=pl.BlockSpec((tm, d), lambda i: (i, 0)),
        out_shape=jax.ShapeDtypeStruct((m, d), F32),
        scratch_shapes=[pltpu.VMEM((tm, w.shape[1]), BF16)],
        compiler_params=_params("parallel"),
        name="out_proj",
    )(x2, mix_a, mix_b, mix_c, w)


def _ffn_kernel(x_ref, g_ref, wg_ref, wu_ref, wd_ref, fg_ref, o_ref, h_ref, *, final_norm):
    f = pl.program_id(1)

    @pl.when(f == 0)
    def _():
        x = x_ref[...]
        h_ref[...] = _rms_rows(x, g_ref[...]).astype(BF16)
        o_ref[...] = x

    h = h_ref[...]
    gate = jnp.dot(h, wg_ref[...], preferred_element_type=F32)
    up = jnp.dot(h, wu_ref[...], preferred_element_type=F32)
    act = (gate * jax.nn.sigmoid(gate) * up).astype(BF16)
    o_ref[...] += jnp.dot(act, wd_ref[...], preferred_element_type=F32)

    if final_norm:
        @pl.when(f == pl.num_programs(1) - 1)
        def _():
            o_ref[...] = _rms_rows(o_ref[...], fg_ref[...])


def _ffn(x2, gain, wg, wu, wd, layer, final_gain, final_norm):
    m, d = x2.shape
    dff = wg.shape[2]
    tm, tf = FFN_TM, FFN_TF
    assert m % tm == 0 and dff % tf == 0
    return pl.pallas_call(
        functools.partial(_ffn_kernel, final_norm=final_norm),
        grid=(m // tm, dff // tf),
        in_specs=[
            pl.BlockSpec((tm, d), lambda i, f: (i, 0)),
            pl.BlockSpec((1, d), lambda i, f: (0, 0)),
            pl.BlockSpec((None, d, tf), lambda i, f: (layer, 0, f)),
            pl.BlockSpec((None, d, tf), lambda i, f: (layer, 0, f)),
            pl.BlockSpec((None, tf, d), lambda i, f: (layer, f, 0)),
            pl.BlockSpec((1, d), lambda i, f: (0, 0)),
        ],
        out_specs=pl.BlockSpec((tm, d), lambda i, f: (i, 0)),
        out_shape=jax.ShapeDtypeStruct((m, d), F32),
        scratch_shapes=[pltpu.VMEM((tm, d), BF16)],
        compiler_params=_params("parallel", "arbitrary"),
        name="ffn",
    )(x2, gain, wg, wu, wd, final_gain)


def kernel(x, positions, attn_norm, w_in, gmlp_ln_g, gmlp_ln_b, gmlp_ws, gmlp_bs, lambda_q1, lambda_k1,
           lambda_q2, lambda_k2, conv_w, mix_norm, w_out, ffn_norm, w_gate, w_up, w_down, final_norm):
    b, s, d = x.shape
    m = b * s
    depth = w_in.shape[0]

    inv_freq = 1.0 / (ROPE_THETA ** (jnp.arange(0, HEAD_DIM, 2, dtype=F32) / HEAD_DIM))
    ang = positions.astype(F32).reshape(m, 1) * inv_freq
    cosf = jnp.concatenate([jnp.cos(ang), jnp.cos(ang)], axis=-1)
    sinf = jnp.concatenate([-jnp.sin(ang), jnp.sin(ang)], axis=-1)

    w_in, w_out, w_gate, w_up, w_down = (w.astype(BF16) for w in (w_in, w_out, w_gate, w_up, w_down))
    x2 = x.reshape(m, d)
    for l in range(depth):
        lam_init = 0.8 - 0.6 * math.exp(-0.3 * l)
        gain = mix_norm[l].astype(F32)
        z = _in_proj(x2, attn_norm[l].reshape(1, d), w_in, l, cosf, sinf)
        z3 = z.reshape(b, s, z.shape[1])
        mix_a, mix_c = _local_mixers(
            z3, gmlp_ln_g[l], gmlp_ln_b[l], gmlp_ws[l], jnp.tile(gmlp_bs[l], (1, LOCAL_ROWS // GMLP_CHUNK)).reshape(N_GROUPS, LOCAL_ROWS, 1), conv_w[l],
            gain[:GMLP_WIDTH].reshape(1, GMLP_WIDTH), gain[GMLP_WIDTH + DIFF_WIDTH:].reshape(1, CONV_WIDTH))
        mix_b = _diff_attention(
            z3, lambda_q1[l].reshape(1, HEAD_DIM), lambda_k1[l].reshape(1, HEAD_DIM),
            lambda_q2[l].reshape(1, HEAD_DIM), lambda_k2[l].reshape(1, HEAD_DIM),
            gain[GMLP_WIDTH:GMLP_WIDTH + DIFF_WIDTH].reshape(N_GROUPS, 1, DIFF_V_DIM), lam_init)
        x2 = _out_proj(x2, mix_a.reshape(m, GMLP_WIDTH), mix_b.reshape(m, DIFF_WIDTH),
                       mix_c.reshape(m, CONV_WIDTH), w_out, l)
        x2 = _ffn(x2, ffn_norm[l].reshape(1, d), w_gate, w_up, w_down, l, final_norm.reshape(1, d),
                  final_norm=(l == depth - 1))
    return x2.reshape(b, s, d)
```

```python
import functools
import math

import jax
import jax.numpy as jnp
from jax import lax
from jax.experimental import pallas as pl
from jax.experimental.pallas import tpu as pltpu

F32 = jnp.float32
BF16 = jnp.bfloat16

HEAD_DIM = 128
N_GROUPS = 4
GMLP_WIDTH = N_GROUPS * HEAD_DIM
GMLP_CHUNK = 128
DIFF_QK_WIDTH = N_GROUPS * 2 * HEAD_DIM
DIFF_V_DIM = 2 * HEAD_DIM
DIFF_WIDTH = N_GROUPS * DIFF_V_DIM
CONV_WIDTH = N_GROUPS * HEAD_DIM
CONV_K = 3
ROPE_THETA = 10000.0
RMS_EPS = 1e-6
LN_EPS = 1e-5
LOG2_E = math.log2(math.e)

Q_OFF = 2 * GMLP_WIDTH
K_OFF = Q_OFF + DIFF_QK_WIDTH
V_OFF = K_OFF + DIFF_QK_WIDTH
C_OFF = V_OFF + DIFF_WIDTH

VMEM_LIMIT_BYTES = 56 * 1024 * 1024

IN_PROJ_TM = 512
IN_PROJ_TN = 2816
LOCAL_ROWS = 512
ATTN_TQ = 512
ATTN_ROW_CHUNK = 32
OUT_PROJ_TM = 512
FFN_TM = 1024
FFN_TF = 512
CONV_HALO_ROWS = 8


def _params(*semantics):
    return pltpu.CompilerParams(dimension_semantics=semantics, vmem_limit_bytes=VMEM_LIMIT_BYTES)


def _rms_rows(x, gain):
    return x * lax.rsqrt(jnp.mean(x * x, axis=-1, keepdims=True) + RMS_EPS) * gain


def _group_rms(x):
    return x * lax.rsqrt(jnp.mean(x * x, axis=-1, keepdims=True) + RMS_EPS)


def _in_proj_kernel(x_ref, g_ref, w_ref, cos_ref, sin_ref, z_ref, h_ref, *, tn, n_col_tiles):
    j = pl.program_id(1)

    @pl.when(j == 0)
    def _():
        h_ref[...] = _rms_rows(x_ref[...], g_ref[...]).astype(BF16)

    def tile(col0):
        acc = jnp.dot(h_ref[...], w_ref[...], preferred_element_type=F32)
        cosf, sinf = cos_ref[...], sin_ref[...]
        is_rope = [Q_OFF <= col0 + c * HEAD_DIM < V_OFF for c in range(tn // HEAD_DIM)]
        c = 0
        while c < len(is_rope):
            if is_rope[c]:
                xc = acc[:, c * HEAD_DIM:(c + 1) * HEAD_DIM]
                r = xc * cosf + pltpu.roll(xc, HEAD_DIM // 2, 1) * sinf
                if col0 + c * HEAD_DIM < K_OFF:
                    r = r * (LOG2_E / math.sqrt(HEAD_DIM))
                z_ref[:, c * HEAD_DIM:(c + 1) * HEAD_DIM] = r.astype(z_ref.dtype)
                c += 1
            else:
                e = c
                while e < len(is_rope) and not is_rope[e]:
                    e += 1
                z_ref[:, c * HEAD_DIM:e * HEAD_DIM] = acc[:, c * HEAD_DIM:e * HEAD_DIM].astype(z_ref.dtype)
                c = e

    for jt in range(n_col_tiles):
        pl.when(j == jt)(functools.partial(tile, jt * tn))


def _in_proj(x2, gain, w, layer, cosf, sinf):
    m, d = x2.shape
    n = w.shape[2]
    tm, tn = IN_PROJ_TM, IN_PROJ_TN
    assert m % tm == 0 and n % tn == 0 and tn % HEAD_DIM == 0
    return pl.pallas_call(
        functools.partial(_in_proj_kernel, tn=tn, n_col_tiles=n // tn),
        grid=(m // tm, n // tn),
        in_specs=[
            pl.BlockSpec((tm, d), lambda i, j: (i, 0)),
            pl.BlockSpec((1, d), lambda i, j: (0, 0)),
            pl.BlockSpec((None, d, tn), lambda i, j: (layer, 0, j)),
            pl.BlockSpec((tm, HEAD_DIM), lambda i, j: (i, 0)),
            pl.BlockSpec((tm, HEAD_DIM), lambda i, j: (i, 0)),
        ],
        out_specs=pl.BlockSpec((tm, tn), lambda i, j: (i, j)),
        out_shape=jax.ShapeDtypeStruct((m, n), BF16),
        scratch_shapes=[pltpu.VMEM((tm, d), BF16)],
        compiler_params=_params("parallel", "arbitrary"),
        name="in_proj",
    )(x2, gain, w, cosf, sinf)


def _gelu(x):
    return 0.5 * x * (1.0 + lax.erf(x * math.sqrt(0.5)))


def _local_kernel(za_ref, bg_ref, cg_ref, hc_ref, cgh_ref, hch_ref, lng_ref, lnb_ref, ws_ref, bs_ref,
                  cw_ref, ga_ref, gc_ref, oa_ref, oc_ref, vn_ref, mixed_ref, *, rows):
    i = pl.program_id(1)

    groups = [slice(g * HEAD_DIM, (g + 1) * HEAD_DIM) for g in range(N_GROUPS)]
    for g, cols in enumerate(groups):
        v = _gelu(za_ref[0, :, GMLP_WIDTH + cols.start:GMLP_WIDTH + cols.stop].astype(F32))
        mu = jnp.mean(v, axis=-1, keepdims=True)
        vc = v - mu
        var = jnp.mean(vc * vc, axis=-1, keepdims=True)
        vn = vc * lax.rsqrt(var + LN_EPS) * lng_ref[g:g + 1, :] + lnb_ref[g:g + 1, :]
        vn_ref[:, cols] = vn.astype(BF16)
    t_idx = lax.broadcasted_iota(jnp.int32, (GMLP_CHUNK, GMLP_CHUNK), 0)
    s_idx = lax.broadcasted_iota(jnp.int32, (GMLP_CHUNK, GMLP_CHUNK), 1)
    for g, cols in enumerate(groups):
        w = jnp.where(s_idx <= t_idx, ws_ref[g], 0.0).astype(BF16)
        for c in range(rows // GMLP_CHUNK):
            chunk = slice(c * GMLP_CHUNK, (c + 1) * GMLP_CHUNK)
            mixed_ref[chunk, cols] = jnp.dot(w, vn_ref[chunk, cols], preferred_element_type=F32)
    for g, cols in enumerate(groups):
        u = _gelu(za_ref[0, :, cols].astype(F32))
        ya = _group_rms(u * (mixed_ref[:, cols] + bs_ref[g])) * ga_ref[:, cols]
        oa_ref[0, :, cols] = ya.astype(oa_ref.dtype)

    xh = cg_ref[0].astype(F32) * hc_ref[0].astype(F32)
    halo = cgh_ref[0].astype(F32) * hch_ref[0].astype(F32)
    halo = jnp.where(i == 0, 0.0, halo)
    row = lax.broadcasted_iota(jnp.int32, xh.shape, 0)
    prev1 = jnp.where(row == 0, halo[CONV_HALO_ROWS - 1:CONV_HALO_ROWS, :], pltpu.roll(xh, 1, 0))
    prev2 = jnp.where(row == 0, halo[CONV_HALO_ROWS - 2:CONV_HALO_ROWS - 1, :],
                      jnp.where(row == 1, halo[CONV_HALO_ROWS - 1:CONV_HALO_ROWS, :], pltpu.roll(xh, 2, 0)))
    y = cw_ref[0:1, :] * prev2 + cw_ref[1:2, :] * prev1 + cw_ref[2:3, :] * xh
    yc = bg_ref[0].astype(F32) * y
    for g in range(N_GROUPS):
        lo = g * HEAD_DIM
        oc_ref[0, :, lo:lo + HEAD_DIM] = (_group_rms(yc[:, lo:lo + HEAD_DIM])
                                          * gc_ref[:, lo:lo + HEAD_DIM]).astype(oc_ref.dtype)


def _local_mixers(z3, ln_g, ln_b, ws, bs, conv_w, gain_a, gain_c):
    b, s, _ = z3.shape
    rows = LOCAL_ROWS
    assert s % rows == 0 and rows % GMLP_CHUNK == 0 and C_OFF % CONV_WIDTH == 0
    cb = C_OFF // CONV_WIDTH
    hb = rows // CONV_HALO_ROWS
    full = lambda shape: pl.BlockSpec(shape, lambda bi, i: (0,) * len(shape))
    halo_map = lambda col: (lambda bi, i: (bi, jnp.maximum(i * hb - 1, 0), col))
    return pl.pallas_call(
        functools.partial(_local_kernel, rows=rows),
        grid=(b, s // rows),
        in_specs=[
            pl.BlockSpec((1, rows, 2 * GMLP_WIDTH), lambda bi, i: (bi, i, 0)),
            pl.BlockSpec((1, rows, CONV_WIDTH), lambda bi, i: (bi, i, cb)),
            pl.BlockSpec((1, rows, CONV_WIDTH), lambda bi, i: (bi, i, cb + 1)),
            pl.BlockSpec((1, rows, CONV_WIDTH), lambda bi, i: (bi, i, cb + 2)),
            pl.BlockSpec((1, CONV_HALO_ROWS, CONV_WIDTH), halo_map(cb + 1)),
            pl.BlockSpec((1, CONV_HALO_ROWS, CONV_WIDTH), halo_map(cb + 2)),
            full((N_GROUPS, HEAD_DIM)),
            full((N_GROUPS, HEAD_DIM)),
            full((N_GROUPS, GMLP_CHUNK, GMLP_CHUNK)),
            full((N_GROUPS, rows, 1)),
            full((CONV_K, CONV_WIDTH)),
            full((1, GMLP_WIDTH)),
            full((1, CONV_WIDTH)),
        ],
        out_specs=[
            pl.BlockSpec((1, rows, GMLP_WIDTH), lambda bi, i: (bi, i, 0)),
            pl.BlockSpec((1, rows, CONV_WIDTH), lambda bi, i: (bi, i, 0)),
        ],
        out_shape=[jax.ShapeDtypeStruct((b, s, GMLP_WIDTH), BF16),
                   jax.ShapeDtypeStruct((b, s, CONV_WIDTH), BF16)],
        scratch_shapes=[pltpu.VMEM((rows, GMLP_WIDTH), BF16),
                        pltpu.VMEM((rows, GMLP_WIDTH), F32)],
        compiler_params=_params("parallel", "arbitrary"),
        name="local_mixers",
    )(z3, z3, z3, z3, z3, z3, ln_g, ln_b, ws, bs, conv_w, gain_a, gain_c)


def _attn_kernel(lq1_ref, lk1_ref, lq2_ref, lk2_ref, q_ref, k_ref, v_ref, gain_ref, o_ref,
                 qs_ref, m_ref, l_ref, acc_ref, *bufs, tq, n_q, n_buf, lam_init):
    rc = ATTN_ROW_CHUNK
    n_tiles = tq // HEAD_DIM
    s_bufs, p_bufs, a_bufs = bufs[:n_buf], bufs[n_buf:2 * n_buf], bufs[2 * n_buf:]

    q = q_ref[0]
    zero = jnp.zeros((tq, HEAD_DIM), q.dtype)
    qs_ref[:tq, :HEAD_DIM] = q[:, :HEAD_DIM]
    qs_ref[:tq, HEAD_DIM:] = zero
    qs_ref[tq:, :HEAD_DIM] = zero
    qs_ref[tq:, HEAD_DIM:] = q[:, HEAD_DIM:]

    def key_rows(j):
        return slice(j * tq, (j + 1) * tq)

    def scores(j, buf):
        s_bufs[buf][...] = lax.dot_general(qs_ref[...], k_ref[0, key_rows(j), :], (((1,), (1,)), ((), ())),
                                           preferred_element_type=F32)

    def softmax(buf, diagonal, first):
        s_ref, p_ref, alpha_ref = s_bufs[buf], p_bufs[buf], a_bufs[buf]
        for c in range(2 * tq // rc):
            rows = slice(c * rc, (c + 1) * rc)
            q0 = rows.start % tq
            live = [t for t in range(n_tiles) if (not diagonal) or t * HEAD_DIM <= q0 + rc - 1]
            tiles = []
            for t in live:
                st = s_ref[rows, t * HEAD_DIM:(t + 1) * HEAD_DIM]
                if diagonal and (t + 1) * HEAD_DIM - 1 > q0:
                    row = lax.broadcasted_iota(jnp.int32, (rc, HEAD_DIM), 0) + q0
                    col = lax.broadcasted_iota(jnp.int32, (rc, HEAD_DIM), 1) + t * HEAD_DIM
                    st = jnp.where(col <= row, st, -jnp.inf)
                tiles.append(st)
            m_blk = jnp.max(functools.reduce(jnp.maximum, tiles), axis=-1, keepdims=True)
            if first:
                m_new = jnp.broadcast_to(m_blk, (rc, HEAD_DIM))
                ps = [jnp.exp2(st - m_new) for st in tiles]
                l_ref[rows, :] = functools.reduce(jnp.add, ps)
            else:
                m_old = m_ref[rows, :]
                m_new = jnp.maximum(m_old, m_blk)
                alpha = jnp.exp2(m_old - m_new)
                ps = [jnp.exp2(st - m_new) for st in tiles]
                l_ref[rows, :] = alpha * l_ref[rows, :] + functools.reduce(jnp.add, ps)
                alpha_ref[rows, :] = alpha
            m_ref[rows, :] = m_new
            for t in range(n_tiles):
                p_t = ps[live.index(t)].astype(BF16) if t in live else jnp.zeros((rc, HEAD_DIM), BF16)
                p_ref[rows, t * HEAD_DIM:(t + 1) * HEAD_DIM] = p_t

    def accumulate(j, buf, first):
        pv = jnp.dot(p_bufs[buf][...], v_ref[0, key_rows(j), :], preferred_element_type=F32)
        if first:
            acc_ref[...] = pv
        else:
            alpha = a_bufs[buf][...]
            for t in range(DIFF_V_DIM // HEAD_DIM):
                cols = slice(t * HEAD_DIM, (t + 1) * HEAD_DIM)
                acc_ref[:, cols] = acc_ref[:, cols] * alpha + pv[:, cols]

    def query_block(qi):
        scores(0, 0)
        for j in range(qi + 1):
            if j < qi:
                scores(j + 1, (j + 1) % n_buf)
            if j >= 1:
                accumulate(j - 1, (j - 1) % n_buf, first=(j == 1))
            softmax(j % n_buf, diagonal=(j == qi), first=(j == 0))
        accumulate(qi, qi % n_buf, first=(qi == 0))

        lam = (jnp.exp(jnp.sum(lq1_ref[...] * lk1_ref[...], axis=-1, keepdims=True))
               - jnp.exp(jnp.sum(lq2_ref[...] * lk2_ref[...], axis=-1, keepdims=True)) + lam_init)
        o = acc_ref[...] * (1.0 / jnp.sum(l_ref[...], axis=-1, keepdims=True))
        o = o[:tq] - lam * o[tq:]
        o = _group_rms(o) * (1.0 - lam_init) * gain_ref[0]
        o_ref[0] = o.astype(o_ref.dtype)

    for qi in range(n_q):
        pl.when(pl.program_id(2) == qi)(functools.partial(query_block, qi))


def _diff_attention(z3, lq1, lk1, lq2, lk2, gain_b, lam_init):
    b, s, _ = z3.shape
    tq = ATTN_TQ
    assert s % tq == 0
    n_buf = s // tq
    qb, kb, vb = Q_OFF // DIFF_V_DIM, K_OFF // DIFF_V_DIM, V_OFF // DIFF_V_DIM
    vec = pl.BlockSpec((1, HEAD_DIM), lambda bi, h, i: (0, 0))
    return pl.pallas_call(
        functools.partial(_attn_kernel, tq=tq, n_q=s // tq, n_buf=n_buf, lam_init=lam_init),
        grid=(b, N_GROUPS, s // tq),
        in_specs=[
            vec, vec, vec, vec,
            pl.BlockSpec((1, tq, DIFF_V_DIM), lambda bi, h, i: (bi, i, qb + h)),
            pl.BlockSpec((1, s, DIFF_V_DIM), lambda bi, h, i: (bi, 0, kb + h)),
            pl.BlockSpec((1, s, DIFF_V_DIM), lambda bi, h, i: (bi, 0, vb + h)),
            pl.BlockSpec((1, 1, DIFF_V_DIM), lambda bi, h, i: (h, 0, 0)),
        ],
        out_specs=pl.BlockSpec((1, tq, DIFF_V_DIM), lambda bi, h, i: (bi, i, h)),
        out_shape=jax.ShapeDtypeStruct((b, s, DIFF_WIDTH), BF16),
        scratch_shapes=[
            pltpu.VMEM((2 * tq, 2 * HEAD_DIM), BF16),
            pltpu.VMEM((2 * tq, HEAD_DIM), F32),
            pltpu.VMEM((2 * tq, HEAD_DIM), F32),
            pltpu.VMEM((2 * tq, DIFF_V_DIM), F32),
        ] + [pltpu.VMEM((2 * tq, tq), F32)] * n_buf
          + [pltpu.VMEM((2 * tq, tq), BF16)] * n_buf
          + [pltpu.VMEM((2 * tq, HEAD_DIM), F32)] * n_buf,
        compiler_params=_params("parallel", "parallel", "arbitrary"),
        name="diff_attention",
    )(lq1, lk1, lq2, lk2, z3, z3, z3, gain_b)


def _out_proj_kernel(x_ref, ma_ref, mb_ref, mc_ref, w_ref, o_ref, mix_ref):
    mix_ref[:, :GMLP_WIDTH] = ma_ref[...]
    mix_ref[:, GMLP_WIDTH:GMLP_WIDTH + DIFF_WIDTH] = mb_ref[...]
    mix_ref[:, GMLP_WIDTH + DIFF_WIDTH:] = mc_ref[...]
    o_ref[...] = x_ref[...] + jnp.dot(mix_ref[...], w_ref[...], preferred_element_type=F32)


def _out_proj(x2, mix_a, mix_b, mix_c, w, layer):
    m, d = x2.shape
    tm = OUT_PROJ_TM
    assert m % tm == 0
    return pl.pallas_call(
        _out_proj_kernel,
        grid=(m // tm,),
        in_specs=[
            pl.BlockSpec((tm, d), lambda i: (i, 0)),
            pl.BlockSpec((tm, GMLP_WIDTH), lambda i: (i, 0)),
            pl.BlockSpec((tm, DIFF_WIDTH), lambda i: (i, 0)),
            pl.BlockSpec((tm, CONV_WIDTH), lambda i: (i, 0)),
            pl.BlockSpec((None,) + w.shape[1:], lambda i: (layer, 0, 0)),
        ],
        out_specs=pl.BlockSpec((tm, d), lambda i: (i, 0)),
        out_shape=jax.ShapeDtypeStruct((m, d), F32),
        scratch_shapes=[pltpu.VMEM((tm, w.shape[1]), BF16)],
        compiler_params=_params("parallel"),
        name="out_proj",
    )(x2, mix_a, mix_b, mix_c, w)


def _ffn_kernel(x_ref, g_ref, wg_ref, wu_ref, wd_ref, fg_ref, o_ref, h_ref, *, final_norm):
    f = pl.program_id(1)

    @pl.when(f == 0)
    def _():
        x = x_ref[...]
        h_ref[...] = _rms_rows(x, g_ref[...]).astype(BF16)
        o_ref[...] = x

    h = h_ref[...]
    gate = jnp.dot(h, wg_ref[...], preferred_element_type=F32)
    up = jnp.dot(h, wu_ref[...], preferred_element_type=F32)
    act = (gate * jax.nn.sigmoid(gate) * up).astype(BF16)
    o_ref[...] += jnp.dot(act, wd_ref[...], preferred_element_type=F32)

    if final_norm:
        @pl.when(f == pl.num_programs(1) - 1)
        def _():
            o_ref[...] = _rms_rows(o_ref[...], fg_ref[...])


def _ffn(x2, gain, wg, wu, wd, layer, final_gain, final_norm):
    m, d = x2.shape
    dff = wg.shape[2]
    tm, tf = FFN_TM, FFN_TF
    assert m % tm == 0 and dff % tf == 0
    return pl.pallas_call(
        functools.partial(_ffn_kernel, final_norm=final_norm),
        grid=(m // tm, dff // tf),
        in_specs=[
            pl.BlockSpec((tm, d), lambda i, f: (i, 0)),
            pl.BlockSpec((1, d), lambda i, f: (0, 0)),
            pl.BlockSpec((None, d, tf), lambda i, f: (layer, 0, f)),
            pl.BlockSpec((None, d, tf), lambda i, f: (layer, 0, f)),
            pl.BlockSpec((None, tf, d), lambda i, f: (layer, f, 0)),
            pl.BlockSpec((1, d), lambda i, f: (0, 0)),
        ],
        out_specs=pl.BlockSpec((tm, d), lambda i, f: (i, 0)),
        out_shape=jax.ShapeDtypeStruct((m, d), F32),
        scratch_shapes=[pltpu.VMEM((tm, d), BF16)],
        compiler_params=_params("parallel", "arbitrary"),
        name="ffn",
    )(x2, gain, wg, wu, wd, final_gain)


def kernel(x, positions, attn_norm, w_in, gmlp_ln_g, gmlp_ln_b, gmlp_ws, gmlp_bs, lambda_q1, lambda_k1,
           lambda_q2, lambda_k2, conv_w, mix_norm, w_out, ffn_norm, w_gate, w_up, w_down, final_norm):
    b, s, d = x.shape
    m = b * s
    depth = w_in.shape[0]

    inv_freq = 1.0 / (ROPE_THETA ** (jnp.arange(0, HEAD_DIM, 2, dtype=F32) / HEAD_DIM))
    ang = positions.astype(F32).reshape(m, 1) * inv_freq
    cosf = jnp.concatenate([jnp.cos(ang), jnp.cos(ang)], axis=-1)
    sinf = jnp.concatenate([-jnp.sin(ang), jnp.sin(ang)], axis=-1)

    w_in, w_out, w_gate, w_up, w_down = (w.astype(BF16) for w in (w_in, w_out, w_gate, w_up, w_down))
    x2 = x.reshape(m, d)
    for l in range(depth):
        lam_init = 0.8 - 0.6 * math.exp(-0.3 * l)
        gain = mix_norm[l].astype(F32)
        z = _in_proj(x2, attn_norm[l].reshape(1, d), w_in, l, cosf, sinf)
        z3 = z.reshape(b, s, z.shape[1])
        mix_a, mix_c = _local_mixers(
            z3, gmlp_ln_g[l], gmlp_ln_b[l], gmlp_ws[l], jnp.tile(gmlp_bs[l], (1, LOCAL_ROWS // GMLP_CHUNK)).reshape(N_GROUPS, LOCAL_ROWS, 1), conv_w[l],
            gain[:GMLP_WIDTH].reshape(1, GMLP_WIDTH), gain[GMLP_WIDTH + DIFF_WIDTH:].reshape(1, CONV_WIDTH))
        mix_b = _diff_attention(
            z3, lambda_q1[l].reshape(1, HEAD_DIM), lambda_k1[l].reshape(1, HEAD_DIM),
            lambda_q2[l].reshape(1, HEAD_DIM), lambda_k2[l].reshape(1, HEAD_DIM),
            gain[GMLP_WIDTH:GMLP_WIDTH + DIFF_WIDTH].reshape(N_GROUPS, 1, DIFF_V_DIM), lam_init)
        x2 = _out_proj(x2, mix_a.reshape(m, GMLP_WIDTH), mix_b.reshape(m, DIFF_WIDTH),
                       mix_c.reshape(m, CONV_WIDTH), w_out, l)
        x2 = _ffn(x2, ffn_norm[l].reshape(1, d), w_gate, w_up, w_down, l, final_norm.reshape(1, d),
                  final_norm=(l == depth - 1))
    return x2.reshape(b, s, d)
```

```python
import functools
import math

import jax
import jax.numpy as jnp
from jax import lax
from jax.experimental import pallas as pl
from jax.experimental.pallas import tpu as pltpu

F32 = jnp.float32
BF16 = jnp.bfloat16

HEAD_DIM = 128
N_GROUPS = 4
GMLP_WIDTH = N_GROUPS * HEAD_DIM
GMLP_CHUNK = 128
DIFF_QK_WIDTH = N_GROUPS * 2 * HEAD_DIM
DIFF_V_DIM = 2 * HEAD_DIM
DIFF_WIDTH = N_GROUPS * DIFF_V_DIM
CONV_WIDTH = N_GROUPS * HEAD_DIM
CONV_K = 3
ROPE_THETA = 10000.0
RMS_EPS = 1e-6
LN_EPS = 1e-5
LOG2_E = math.log2(math.e)

Q_OFF = 2 * GMLP_WIDTH
K_OFF = Q_OFF + DIFF_QK_WIDTH
V_OFF = K_OFF + DIFF_QK_WIDTH
C_OFF = V_OFF + DIFF_WIDTH

VMEM_LIMIT_BYTES = 56 * 1024 * 1024

IN_PROJ_TM = 512
IN_PROJ_TN = 2816
LOCAL_ROWS = 512
ATTN_TQ = 512
ATTN_ROW_CHUNK = 32
OUT_PROJ_TM = 512
FFN_TM = 1024
FFN_TF = 512
CONV_HALO_ROWS = 8


def _params(*semantics):
    return pltpu.CompilerParams(dimension_semantics=semantics, vmem_limit_bytes=VMEM_LIMIT_BYTES)


def _rms_rows(x, gain):
    return x * lax.rsqrt(jnp.mean(x * x, axis=-1, keepdims=True) + RMS_EPS) * gain


def _group_rms(x):
    return x * lax.rsqrt(jnp.mean(x * x, axis=-1, keepdims=True) + RMS_EPS)


def _gelu(x):
    return 0.5 * x * (1.0 + lax.erf(x * math.sqrt(0.5)))


def _in_proj_kernel(x_ref, g_ref, w_ref, ang_ref, sign_ref, z_ref, h_ref, cos_ref, sin_ref, *, tn, n_col_tiles):
    j = pl.program_id(1)

    def tile(col0):
        first = col0 == (n_col_tiles - 1) * tn
        if first:
            h_ref[...] = _rms_rows(x_ref[...], g_ref[...]).astype(BF16)
        acc = jnp.dot(h_ref[...], w_ref[...], preferred_element_type=F32)
        if first:
            cos_ref[...] = jnp.cos(ang_ref[...])
            sin_ref[...] = jnp.sin(ang_ref[...]) * sign_ref[...]
        cosf, sinf = cos_ref[...], sin_ref[...]
        for c in range(tn // HEAD_DIM):
            col = col0 + c * HEAD_DIM
            r = acc[:, c * HEAD_DIM:(c + 1) * HEAD_DIM]
            if col < Q_OFF:
                r = _gelu(r)
            elif col < V_OFF:
                r = r * cosf + pltpu.roll(r, HEAD_DIM // 2, 1) * sinf
                if col < K_OFF:
                    r = r * (LOG2_E / math.sqrt(HEAD_DIM))
            z_ref[:, c * HEAD_DIM:(c + 1) * HEAD_DIM] = r.astype(z_ref.dtype)

    for jt in range(n_col_tiles):
        pl.when(j == n_col_tiles - 1 - jt)(functools.partial(tile, jt * tn))


def _in_proj(x2, gain, w, layer, ang, sign):
    m, d = x2.shape
    n = w.shape[2]
    tm, tn = IN_PROJ_TM, IN_PROJ_TN
    assert m % tm == 0 and n % tn == 0 and tn % HEAD_DIM == 0
    return pl.pallas_call(
        functools.partial(_in_proj_kernel, tn=tn, n_col_tiles=n // tn),
        grid=(m // tm, n // tn),
        in_specs=[
            pl.BlockSpec((tm, d), lambda i, j: (i, 0)),
            pl.BlockSpec((1, d), lambda i, j: (0, 0)),
            pl.BlockSpec((None, d, tn), lambda i, j: (layer, 0, n // tn - 1 - j)),
            pl.BlockSpec((tm, HEAD_DIM), lambda i, j: (i, 0)),
            pl.BlockSpec((1, HEAD_DIM), lambda i, j: (0, 0)),
        ],
        out_specs=pl.BlockSpec((tm, tn), lambda i, j: (i, n // tn - 1 - j)),
        out_shape=jax.ShapeDtypeStruct((m, n), BF16),
        scratch_shapes=[pltpu.VMEM((tm, d), BF16),
                        pltpu.VMEM((tm, HEAD_DIM), F32),
                        pltpu.VMEM((tm, HEAD_DIM), F32)],
        compiler_params=_params("parallel", "arbitrary"),
        name="in_proj",
    )(x2, gain, w, ang, sign)


def _local_kernel(za_ref, bg_ref, cg_ref, hc_ref, cgh_ref, hch_ref, lng_ref, lnb_ref, ws_ref, bs_ref,
                  cw_ref, ga_ref, gc_ref, oa_ref, oc_ref, vn_ref, mixed_ref, *, rows):
    i = pl.program_id(1)

    groups = [slice(g * HEAD_DIM, (g + 1) * HEAD_DIM) for g in range(N_GROUPS)]
    for g, cols in enumerate(groups):
        v = za_ref[0, :, GMLP_WIDTH + cols.start:GMLP_WIDTH + cols.stop].astype(F32)
        mu = jnp.mean(v, axis=-1, keepdims=True)
        vc = v - mu
        var = jnp.mean(vc * vc, axis=-1, keepdims=True)
        vn = vc * lax.rsqrt(var + LN_EPS) * lng_ref[g:g + 1, :] + lnb_ref[g:g + 1, :]
        vn_ref[:, cols] = vn.astype(BF16)
    t_idx = lax.broadcasted_iota(jnp.int32, (GMLP_CHUNK, GMLP_CHUNK), 0)
    s_idx = lax.broadcasted_iota(jnp.int32, (GMLP_CHUNK, GMLP_CHUNK), 1)
    for g, cols in enumerate(groups):
        w = jnp.where(s_idx <= t_idx, ws_ref[g], 0.0).astype(BF16)
        for c in range(rows // GMLP_CHUNK):
            chunk = slice(c * GMLP_CHUNK, (c + 1) * GMLP_CHUNK)
            mixed_ref[chunk, cols] = jnp.dot(w, vn_ref[chunk, cols], preferred_element_type=F32)
    for g, cols in enumerate(groups):
        u = za_ref[0, :, cols].astype(F32)
        ya = _group_rms(u * (mixed_ref[:, cols] + bs_ref[g])) * ga_ref[:, cols]
        oa_ref[0, :, cols] = ya.astype(oa_ref.dtype)

    xh = cg_ref[0].astype(F32) * hc_ref[0].astype(F32)
    halo = cgh_ref[0].astype(F32) * hch_ref[0].astype(F32)
    halo = jnp.where(i == 0, 0.0, halo)
    row = lax.broadcasted_iota(jnp.int32, xh.shape, 0)
    prev1 = jnp.where(row == 0, halo[CONV_HALO_ROWS - 1:CONV_HALO_ROWS, :], pltpu.roll(xh, 1, 0))
    prev2 = jnp.where(row == 0, halo[CONV_HALO_ROWS - 2:CONV_HALO_ROWS - 1, :],
                      jnp.where(row == 1, halo[CONV_HALO_ROWS - 1:CONV_HALO_ROWS, :], pltpu.roll(xh, 2, 0)))
    y = cw_ref[0:1, :] * prev2 + cw_ref[1:2, :] * prev1 + cw_ref[2:3, :] * xh
    yc = bg_ref[0].astype(F32) * y
    for g in range(N_GROUPS):
        lo = g * HEAD_DIM
        oc_ref[0, :, lo:lo + HEAD_DIM] = (_group_rms(yc[:, lo:lo + HEAD_DIM])
                                          * gc_ref[:, lo:lo + HEAD_DIM]).astype(oc_ref.dtype)


def _local_mixers(z3, ln_g, ln_b, ws, bs, conv_w, gain_a, gain_c):
    b, s, _ = z3.shape
    rows = LOCAL_ROWS
    assert s % rows == 0 and rows % GMLP_CHUNK == 0 and C_OFF % CONV_WIDTH == 0
    cb = C_OFF // CONV_WIDTH
    hb = rows // CONV_HALO_ROWS
    full = lambda shape: pl.BlockSpec(shape, lambda bi, i: (0,) * len(shape))
    halo_map = lambda col: (lambda bi, i: (bi, jnp.maximum(i * hb - 1, 0), col))
    return pl.pallas_call(
        functools.partial(_local_kernel, rows=rows),
        grid=(b, s // rows),
        in_specs=[
            pl.BlockSpec((1, rows, 2 * GMLP_WIDTH), lambda bi, i: (bi, i, 0)),
            pl.BlockSpec((1, rows, CONV_WIDTH), lambda bi, i: (bi, i, cb)),
            pl.BlockSpec((1, rows, CONV_WIDTH), lambda bi, i: (bi, i, cb + 1)),
            pl.BlockSpec((1, rows, CONV_WIDTH), lambda bi, i: (bi, i, cb + 2)),
            pl.BlockSpec((1, CONV_HALO_ROWS, CONV_WIDTH), halo_map(cb + 1)),
            pl.BlockSpec((1, CONV_HALO_ROWS, CONV_WIDTH), halo_map(cb + 2)),
            full((N_GROUPS, HEAD_DIM)),
            full((N_GROUPS, HEAD_DIM)),
            full((N_GROUPS, GMLP_CHUNK, GMLP_CHUNK)),
            full((N_GROUPS, rows, 1)),
            full((CONV_K, CONV_WIDTH)),
            full((1, GMLP_WIDTH)),
            full((1, CONV_WIDTH)),
        ],
        out_specs=[
            pl.BlockSpec((1, rows, GMLP_WIDTH), lambda bi, i: (bi, i, 0)),
            pl.BlockSpec((1, rows, CONV_WIDTH), lambda bi, i: (bi, i, 0)),
        ],
        out_shape=[jax.ShapeDtypeStruct((b, s, GMLP_WIDTH), BF16),
                   jax.ShapeDtypeStruct((b, s, CONV_WIDTH), BF16)],
        scratch_shapes=[pltpu.VMEM((rows, GMLP_WIDTH), BF16),
                        pltpu.VMEM((rows, GMLP_WIDTH), F32)],
        compiler_params=_params("parallel", "arbitrary"),
        name="local_mixers",
    )(z3, z3, z3, z3, z3, z3, ln_g, ln_b, ws, bs, conv_w, gain_a, gain_c)


def _attn_kernel(lq1_ref, lk1_ref, lq2_ref, lk2_ref, q_ref, k_ref, v_ref, gain_ref, o_ref,
                 qs_ref, m_ref, l_ref, acc_ref, *bufs, tq, n_q, n_buf, lam_init):
    rc = ATTN_ROW_CHUNK
    n_tiles = tq // HEAD_DIM
    s_bufs, p_bufs, a_bufs = bufs[:n_buf], bufs[n_buf:2 * n_buf], bufs[2 * n_buf:]

    q = q_ref[0]
    zero = jnp.zeros((tq, HEAD_DIM), q.dtype)
    qs_ref[:tq, :HEAD_DIM] = q[:, :HEAD_DIM]
    qs_ref[:tq, HEAD_DIM:] = zero
    qs_ref[tq:, :HEAD_DIM] = zero
    qs_ref[tq:, HEAD_DIM:] = q[:, HEAD_DIM:]

    def key_rows(j):
        return slice(j * tq, (j + 1) * tq)

    def scores(j, buf):
        s_bufs[buf][...] = lax.dot_general(qs_ref[...], k_ref[0, key_rows(j), :], (((1,), (1,)), ((), ())),
                                           preferred_element_type=F32)

    def softmax(buf, diagonal, first):
        s_ref, p_ref, alpha_ref = s_bufs[buf], p_bufs[buf], a_bufs[buf]
        for c in range(2 * tq // rc):
            rows = slice(c * rc, (c + 1) * rc)
            q0 = rows.start % tq
            live = [t for t in range(n_tiles) if (not diagonal) or t * HEAD_DIM <= q0 + rc - 1]
            tiles = []
            for t in live:
                st = s_ref[rows, t * HEAD_DIM:(t + 1) * HEAD_DIM]
                if diagonal and (t + 1) * HEAD_DIM - 1 > q0:
                    row = lax.broadcasted_iota(jnp.int32, (rc, HEAD_DIM), 0) + q0
                    col = lax.broadcasted_iota(jnp.int32, (rc, HEAD_DIM), 1) + t * HEAD_DIM
                    st = jnp.where(col <= row, st, -jnp.inf)
                tiles.append(st)
            m_blk = jnp.max(functools.reduce(jnp.maximum, tiles), axis=-1, keepdims=True)
            if first:
                m_new = jnp.broadcast_to(m_blk, (rc, HEAD_DIM))
                ps = [jnp.exp2(st - m_new) for st in tiles]
                l_ref[rows, :] = functools.reduce(jnp.add, ps)
            else:
                m_old = m_ref[rows, :]
                m_new = jnp.maximum(m_old, m_blk)
                alpha = jnp.exp2(m_old - m_new)
                ps = [jnp.exp2(st - m_new) for st in tiles]
                l_ref[rows, :] = alpha * l_ref[rows, :] + functools.reduce(jnp.add, ps)
                alpha_ref[rows, :] = alpha
            m_ref[rows, :] = m_new
            for t in range(n_tiles):
                p_t = ps[live.index(t)].astype(BF16) if t in live else jnp.zeros((rc, HEAD_DIM), BF16)
                p_ref[rows, t * HEAD_DIM:(t + 1) * HEAD_DIM] = p_t

    def accumulate(j, buf, first):
        pv = jnp.dot(p_bufs[buf][...], v_ref[0, key_rows(j), :], preferred_element_type=F32)
        if first:
            acc_ref[...] = pv
        else:
            alpha = a_bufs[buf][...]
            for t in range(DIFF_V_DIM // HEAD_DIM):
                cols = slice(t * HEAD_DIM, (t + 1) * HEAD_DIM)
                acc_ref[:, cols] = acc_ref[:, cols] * alpha + pv[:, cols]

    def query_block(qi):
        scores(0, 0)
        for j in range(qi + 1):
            if j < qi:
                scores(j + 1, (j + 1) % n_buf)
            if j >= 1:
                accumulate(j - 1, (j - 1) % n_buf, first=(j == 1))
            softmax(j % n_buf, diagonal=(j == qi), first=(j == 0))
        accumulate(qi, qi % n_buf, first=(qi == 0))

        lam = (jnp.exp(jnp.sum(lq1_ref[...] * lk1_ref[...], axis=-1, keepdims=True))
               - jnp.exp(jnp.sum(lq2_ref[...] * lk2_ref[...], axis=-1, keepdims=True)) + lam_init)
        o = acc_ref[...] * (1.0 / jnp.sum(l_ref[...], axis=-1, keepdims=True))
        o = o[:tq] - lam * o[tq:]
        o = _group_rms(o) * (1.0 - lam_init) * gain_ref[0]
        o_ref[0] = o.astype(o_ref.dtype)

    for qi in range(n_q):
        pl.when(pl.program_id(2) == qi)(functools.partial(query_block, qi))


def _diff_attention(z3, lq1, lk1, lq2, lk2, gain_b, lam_init):
    b, s, _ = z3.shape
    tq = ATTN_TQ
    assert s % tq == 0
    n_buf = s // tq
    qb, kb, vb = Q_OFF // DIFF_V_DIM, K_OFF // DIFF_V_DIM, V_OFF // DIFF_V_DIM
    vec = pl.BlockSpec((1, HEAD_DIM), lambda bi, h, i: (0, 0))
    return pl.pallas_call(
        functools.partial(_attn_kernel, tq=tq, n_q=s // tq, n_buf=n_buf, lam_init=lam_init),
        grid=(b, N_GROUPS, s // tq),
        in_specs=[
            vec, vec, vec, vec,
            pl.BlockSpec((1, tq, DIFF_V_DIM), lambda bi, h, i: (bi, i, qb + h)),
            pl.BlockSpec((1, s, DIFF_V_DIM), lambda bi, h, i: (bi, 0, kb + h)),
            pl.BlockSpec((1, s, DIFF_V_DIM), lambda bi, h, i: (bi, 0, vb + h)),
            pl.BlockSpec((1, 1, DIFF_V_DIM), lambda bi, h, i: (h, 0, 0)),
        ],
        out_specs=pl.BlockSpec((1, tq, DIFF_V_DIM), lambda bi, h, i: (bi, i, h)),
        out_shape=jax.ShapeDtypeStruct((b, s, DIFF_WIDTH), BF16),
        scratch_shapes=[
            pltpu.VMEM((2 * tq, 2 * HEAD_DIM), BF16),
            pltpu.VMEM((2 * tq, HEAD_DIM), F32),
            pltpu.VMEM((2 * tq, HEAD_DIM), F32),
            pltpu.VMEM((2 * tq, DIFF_V_DIM), F32),
        ] + [pltpu.VMEM((2 * tq, tq), F32)] * n_buf
          + [pltpu.VMEM((2 * tq, tq), BF16)] * n_buf
          + [pltpu.VMEM((2 * tq, HEAD_DIM), F32)] * n_buf,
        compiler_params=_params("parallel", "parallel", "arbitrary"),
        name="diff_attention",
    )(lq1, lk1, lq2, lk2, z3, z3, z3, gain_b)


def _out_proj_kernel(x_ref, ma_ref, mb_ref, mc_ref, w_ref, o_ref, mix_ref):
    mix_ref[:, :GMLP_WIDTH] = ma_ref[...]
    mix_ref[:, GMLP_WIDTH:GMLP_WIDTH + DIFF_WIDTH] = mb_ref[...]
    mix_ref[:, GMLP_WIDTH + DIFF_WIDTH:] = mc_ref[...]
    o_ref[...] = x_ref[...] + jnp.dot(mix_ref[...], w_ref[...], preferred_element_type=F32)


def _out_proj(x2, mix_a, mix_b, mix_c, w, layer):
    m, d = x2.shape
    tm = OUT_PROJ_TM
    assert m % tm == 0
    return pl.pallas_call(
        _out_proj_kernel,
        grid=(m // tm,),
        in_specs=[
            pl.BlockSpec((tm, d), lambda i: (i, 0)),
            pl.BlockSpec((tm, GMLP_WIDTH), lambda i: (i, 0)),
            pl.BlockSpec((tm, DIFF_WIDTH), lambda i: (i, 0)),
            pl.BlockSpec((tm, CONV_WIDTH), lambda i: (i, 0)),
            pl.BlockSpec((None,) + w.shape[1:], lambda i: (layer, 0, 0)),
        ],
        out_specs=pl.BlockSpec((tm, d), lambda i: (i, 0)),
        out_shape=jax.ShapeDtypeStruct((m, d), F32),
        scratch_shapes=[pltpu.VMEM((tm, w.shape[1]), BF16)],
        compiler_params=_params("parallel"),
        name="out_proj",
    )(x2, mix_a, mix_b, mix_c, w)


def _ffn_kernel(x_ref, g_ref, wg_ref, wu_ref, wd_ref, fg_ref, o_ref, h_ref, *, final_norm):
    f = pl.program_id(1)

    @pl.when(f == 0)
    def _():
        x = x_ref[...]
        h_ref[...] = _rms_rows(x, g_ref[...]).astype(BF16)
        o_ref[...] = x

    h = h_ref[...]
    gate = jnp.dot(h, wg_ref[...], preferred_element_type=F32)
    up = jnp.dot(h, wu_ref[...], preferred_element_type=F32)
    act = (gate * jax.nn.sigmoid(gate) * up).astype(BF16)
    o_ref[...] += jnp.dot(act, wd_ref[...], preferred_element_type=F32)

    if final_norm:
        @pl.when(f == pl.num_programs(1) - 1)
        def _():
            o_ref[...] = _rms_rows(o_ref[...], fg_ref[...])


def _ffn(x2, gain, wg, wu, wd, layer, final_gain, final_norm):
    m, d = x2.shape
    dff = wg.shape[2]
    tm, tf = FFN_TM, FFN_TF
    assert m % tm == 0 and dff % tf == 0
    return pl.pallas_call(
        functools.partial(_ffn_kernel, final_norm=final_norm),
        grid=(m // tm, dff // tf),
        in_specs=[
            pl.BlockSpec((tm, d), lambda i, f: (i, 0)),
            pl.BlockSpec((1, d), lambda i, f: (0, 0)),
            pl.BlockSpec((None, d, tf), lambda i, f: (layer, 0, f)),
            pl.BlockSpec((None, d, tf), lambda i, f: (layer, 0, f)),
            pl.BlockSpec((None, tf, d), lambda i, f: (layer, f, 0)),
            pl.BlockSpec((1, d), lambda i, f: (0, 0)),
        ],
        out_specs=pl.BlockSpec((tm, d), lambda i, f: (i, 0)),
        out_shape=jax.ShapeDtypeStruct((m, d), F32),
        scratch_shapes=[pltpu.VMEM((tm, d), BF16)],
        compiler_params=_params("parallel", "arbitrary"),
        name="ffn",
    )(x2, gain, wg, wu, wd, final_gain)


def kernel(x, positions, attn_norm, w_in, gmlp_ln_g, gmlp_ln_b, gmlp_ws, gmlp_bs, lambda_q1, lambda_k1,
           lambda_q2, lambda_k2, conv_w, mix_norm, w_out, ffn_norm, w_gate, w_up, w_down, final_norm):
    b, s, d = x.shape
    m = b * s
    depth = w_in.shape[0]

    inv_freq = 1.0 / (ROPE_THETA ** (jnp.arange(0, HEAD_DIM, 2, dtype=F32) / HEAD_DIM))
    ang = positions.astype(F32).reshape(m, 1) * jnp.concatenate([inv_freq, inv_freq]).reshape(1, HEAD_DIM)
    sign = jnp.concatenate([-jnp.ones((HEAD_DIM // 2,), F32), jnp.ones((HEAD_DIM // 2,), F32)]).reshape(1, HEAD_DIM)

    w_in, w_out, w_gate, w_up, w_down = (w.astype(BF16) for w in (w_in, w_out, w_gate, w_up, w_down))
    x2 = x.reshape(m, d)
    for l in range(depth):
        lam_init = 0.8 - 0.6 * math.exp(-0.3 * l)
        gain = mix_norm[l].astype(F32)
        z = _in_proj(x2, attn_norm[l].reshape(1, d), w_in, l, ang, sign)
        z3 = z.reshape(b, s, z.shape[1])
        mix_a, mix_c = _local_mixers(
            z3, gmlp_ln_g[l], gmlp_ln_b[l], gmlp_ws[l], jnp.tile(gmlp_bs[l], (1, LOCAL_ROWS // GMLP_CHUNK)).reshape(N_GROUPS, LOCAL_ROWS, 1), conv_w[l],
            gain[:GMLP_WIDTH].reshape(1, GMLP_WIDTH), gain[GMLP_WIDTH + DIFF_WIDTH:].reshape(1, CONV_WIDTH))
        mix_b = _diff_attention(
            z3, lambda_q1[l].reshape(1, HEAD_DIM), lambda_k1[l].reshape(1, HEAD_DIM),
            lambda_q2[l].reshape(1, HEAD_DIM), lambda_k2[l].reshape(1, HEAD_DIM),
            gain[GMLP_WIDTH:GMLP_WIDTH + DIFF_WIDTH].reshape(N_GROUPS, 1, DIFF_V_DIM), lam_init)
        x2 = _out_proj(x2, mix_a.reshape(m, GMLP_WIDTH), mix_b.reshape(m, DIFF_WIDTH),
                       mix_c.reshape(m, CONV_WIDTH), w_out, l)
        x2 = _ffn(x2, ffn_norm[l].reshape(1, d), w_gate, w_up, w_down, l, final_norm.reshape(1, d),
                  final_norm=(l == depth - 1))
    return x2.reshape(b, s, d)
```

```python
import functools
import math

import jax
import jax.numpy as jnp
from jax import lax
from jax.experimental import pallas as pl
from jax.experimental.pallas import tpu as pltpu

F32 = jnp.float32
BF16 = jnp.bfloat16

HEAD_DIM = 128
N_GROUPS = 4
GMLP_WIDTH = N_GROUPS * HEAD_DIM
GMLP_CHUNK = 128
DIFF_QK_WIDTH = N_GROUPS * 2 * HEAD_DIM
DIFF_V_DIM = 2 * HEAD_DIM
DIFF_WIDTH = N_GROUPS * DIFF_V_DIM
CONV_WIDTH = N_GROUPS * HEAD_DIM
CONV_K = 3
ROPE_THETA = 10000.0
RMS_EPS = 1e-6
LN_EPS = 1e-5
LOG2_E = math.log2(math.e)

Q_OFF = 2 * GMLP_WIDTH
K_OFF = Q_OFF + DIFF_QK_WIDTH
V_OFF = K_OFF + DIFF_QK_WIDTH
C_OFF = V_OFF + DIFF_WIDTH

VMEM_LIMIT_BYTES = 56 * 1024 * 1024

IN_PROJ_TM = 512
IN_PROJ_TN = 2816
ATTN_TQ = 512
ATTN_ROW_CHUNK = 32
OUT_PROJ_TM = 512
FFN_TM = 1024
FFN_TF = 512
CONV_HALO_ROWS = 8


def _params(*semantics):
    return pltpu.CompilerParams(dimension_semantics=semantics, vmem_limit_bytes=VMEM_LIMIT_BYTES)


def _rms_rows(x, gain):
    return x * lax.rsqrt(jnp.mean(x * x, axis=-1, keepdims=True) + RMS_EPS) * gain


def _group_rms(x):
    return x * lax.rsqrt(jnp.mean(x * x, axis=-1, keepdims=True) + RMS_EPS)


def _gelu(x):
    return 0.5 * x * (1.0 + lax.erf(x * math.sqrt(0.5)))


def _in_proj_kernel(x_ref, g_ref, w_ref, ang_ref, sign_ref, z_ref, h_ref, cos_ref, sin_ref, *, tn, n_col_tiles):
    j = pl.program_id(1)

    def tile(col0):
        first = col0 == (n_col_tiles - 1) * tn
        if first:
            h_ref[...] = _rms_rows(x_ref[...], g_ref[...]).astype(BF16)
        acc = jnp.dot(h_ref[...], w_ref[...], preferred_element_type=F32)
        if first:
            cos_ref[...] = jnp.cos(ang_ref[...])
            sin_ref[...] = jnp.sin(ang_ref[...]) * sign_ref[...]
        cosf, sinf = cos_ref[...], sin_ref[...]
        for c in range(tn // HEAD_DIM):
            col = col0 + c * HEAD_DIM
            r = acc[:, c * HEAD_DIM:(c + 1) * HEAD_DIM]
            if col < Q_OFF:
                r = _gelu(r)
            elif col < V_OFF:
                r = r * cosf + pltpu.roll(r, HEAD_DIM // 2, 1) * sinf
                if col < K_OFF:
                    r = r * (LOG2_E / math.sqrt(HEAD_DIM))
            z_ref[:, c * HEAD_DIM:(c + 1) * HEAD_DIM] = r.astype(z_ref.dtype)

    for jt in range(n_col_tiles):
        pl.when(j == n_col_tiles - 1 - jt)(functools.partial(tile, jt * tn))


def _in_proj(x2, gain, w, layer, ang, sign):
    m, d = x2.shape
    n = w.shape[2]
    tm, tn = IN_PROJ_TM, IN_PROJ_TN
    assert m % tm == 0 and n % tn == 0 and tn % HEAD_DIM == 0
    return pl.pallas_call(
        functools.partial(_in_proj_kernel, tn=tn, n_col_tiles=n // tn),
        grid=(m // tm, n // tn),
        in_specs=[
            pl.BlockSpec((tm, d), lambda i, j: (i, 0)),
            pl.BlockSpec((1, d), lambda i, j: (0, 0)),
            pl.BlockSpec((None, d, tn), lambda i, j: (layer, 0, n // tn - 1 - j)),
            pl.BlockSpec((tm, HEAD_DIM), lambda i, j: (i, 0)),
            pl.BlockSpec((1, HEAD_DIM), lambda i, j: (0, 0)),
        ],
        out_specs=pl.BlockSpec((tm, tn), lambda i, j: (i, n // tn - 1 - j)),
        out_shape=jax.ShapeDtypeStruct((m, n), BF16),
        scratch_shapes=[pltpu.VMEM((tm, d), BF16),
                        pltpu.VMEM((tm, HEAD_DIM), F32),
                        pltpu.VMEM((tm, HEAD_DIM), F32)],
        compiler_params=_params("parallel", "arbitrary"),
        name="in_proj",
    )(x2, gain, w, ang, sign)


def _gmlp_mixer(za_ref, lng_ref, lnb_ref, ws_ref, bs_ref, ga_ref, vn_ref, mixed_ref, out_ref, rows):
    groups = [slice(g * HEAD_DIM, (g + 1) * HEAD_DIM) for g in range(N_GROUPS)]
    for g, cols in enumerate(groups):
        v = za_ref[:, GMLP_WIDTH + cols.start:GMLP_WIDTH + cols.stop].astype(F32)
        mu = jnp.mean(v, axis=-1, keepdims=True)
        vc = v - mu
        var = jnp.mean(vc * vc, axis=-1, keepdims=True)
        vn = vc * lax.rsqrt(var + LN_EPS) * lng_ref[g:g + 1, :] + lnb_ref[g:g + 1, :]
        vn_ref[:, cols] = vn.astype(BF16)
    t_idx = lax.broadcasted_iota(jnp.int32, (GMLP_CHUNK, GMLP_CHUNK), 0)
    s_idx = lax.broadcasted_iota(jnp.int32, (GMLP_CHUNK, GMLP_CHUNK), 1)
    for g, cols in enumerate(groups):
        w = jnp.where(s_idx <= t_idx, ws_ref[g], 0.0).astype(BF16)
        for c in range(rows // GMLP_CHUNK):
            chunk = slice(c * GMLP_CHUNK, (c + 1) * GMLP_CHUNK)
            mixed_ref[chunk, cols] = jnp.dot(w, vn_ref[chunk, cols], preferred_element_type=F32)
    for g, cols in enumerate(groups):
        u = za_ref[:, cols].astype(F32)
        ya = _group_rms(u * (mixed_ref[:, cols] + bs_ref[g])) * ga_ref[:, cols]
        out_ref[:, cols] = ya.astype(out_ref.dtype)


def _short_conv_mixer(bg_ref, cg_ref, hc_ref, cgh_ref, hch_ref, cw_ref, gc_ref, out_ref, at_seq_start):
    xh = cg_ref[...].astype(F32) * hc_ref[...].astype(F32)
    halo = cgh_ref[...].astype(F32) * hch_ref[...].astype(F32)
    halo = jnp.where(at_seq_start, 0.0, halo)
    row = lax.broadcasted_iota(jnp.int32, xh.shape, 0)
    prev1 = jnp.where(row == 0, halo[CONV_HALO_ROWS - 1:CONV_HALO_ROWS, :], pltpu.roll(xh, 1, 0))
    prev2 = jnp.where(row == 0, halo[CONV_HALO_ROWS - 2:CONV_HALO_ROWS - 1, :],
                      jnp.where(row == 1, halo[CONV_HALO_ROWS - 1:CONV_HALO_ROWS, :], pltpu.roll(xh, 2, 0)))
    y = cw_ref[0:1, :] * prev2 + cw_ref[1:2, :] * prev1 + cw_ref[2:3, :] * xh
    yc = bg_ref[...].astype(F32) * y
    for g in range(N_GROUPS):
        cols = slice(g * HEAD_DIM, (g + 1) * HEAD_DIM)
        out_ref[:, cols] = (_group_rms(yc[:, cols]) * gc_ref[:, cols]).astype(out_ref.dtype)


def _mix_out_kernel(x_ref, za_ref, bg_ref, cg_ref, hc_ref, cgh_ref, hch_ref, mb_ref, lng_ref, lnb_ref, ws_ref,
                    bs_ref, cw_ref, ga_ref, gc_ref, w0_ref, w1_ref, w2_ref, w3_ref, o_ref, vn_ref, mixed_ref,
                    ma_ref, mc_ref, *, rows, blocks_per_seq):
    acc = jnp.dot(mb_ref[:, :GMLP_WIDTH], w1_ref[...], preferred_element_type=F32)
    acc = acc + jnp.dot(mb_ref[:, GMLP_WIDTH:], w2_ref[...], preferred_element_type=F32)
    _gmlp_mixer(za_ref, lng_ref, lnb_ref, ws_ref, bs_ref, ga_ref, vn_ref, mixed_ref, ma_ref, rows)
    _short_conv_mixer(bg_ref, cg_ref, hc_ref, cgh_ref, hch_ref, cw_ref, gc_ref, mc_ref,
                      pl.program_id(0) % blocks_per_seq == 0)
    acc = acc + jnp.dot(ma_ref[...], w0_ref[...], preferred_element_type=F32)
    acc = acc + jnp.dot(mc_ref[...], w3_ref[...], preferred_element_type=F32)
    o_ref[...] = x_ref[...] + acc


def _mix_out_proj(x2, z, mix_b, ln_g, ln_b, ws, bs, conv_w, gain_a, gain_c, w, layer, seq_len):
    m, d = x2.shape
    rows = OUT_PROJ_TM
    assert m % rows == 0 and seq_len % rows == 0 and rows % GMLP_CHUNK == 0 and C_OFF % CONV_WIDTH == 0
    assert DIFF_WIDTH == 2 * GMLP_WIDTH and GMLP_WIDTH == CONV_WIDTH
    cb = C_OFF // CONV_WIDTH
    hb = rows // CONV_HALO_ROWS
    full = lambda shape: pl.BlockSpec(shape, lambda i: (0,) * len(shape))
    zcol = lambda width, col: pl.BlockSpec((rows, width), lambda i: (i, col))
    halo = lambda col: pl.BlockSpec((CONV_HALO_ROWS, CONV_WIDTH), lambda i: (jnp.maximum(i * hb - 1, 0), col))
    wrows = lambda blk: pl.BlockSpec((None, GMLP_WIDTH, d), lambda i: (layer, blk, 0))
    return pl.pallas_call(
        functools.partial(_mix_out_kernel, rows=rows, blocks_per_seq=seq_len // rows),
        grid=(m // rows,),
        in_specs=[
            pl.BlockSpec((rows, d), lambda i: (i, 0)),
            zcol(2 * GMLP_WIDTH, 0), zcol(CONV_WIDTH, cb), zcol(CONV_WIDTH, cb + 1), zcol(CONV_WIDTH, cb + 2),
            halo(cb + 1), halo(cb + 2),
            pl.BlockSpec((rows, DIFF_WIDTH), lambda i: (i, 0)),
            full((N_GROUPS, HEAD_DIM)), full((N_GROUPS, HEAD_DIM)),
            full((N_GROUPS, GMLP_CHUNK, GMLP_CHUNK)), full((N_GROUPS, rows, 1)),
            full((CONV_K, CONV_WIDTH)), full((1, GMLP_WIDTH)), full((1, CONV_WIDTH)),
            wrows(0), wrows(1), wrows(2), wrows(3),
        ],
        out_specs=pl.BlockSpec((rows, d), lambda i: (i, 0)),
        out_shape=jax.ShapeDtypeStruct((m, d), F32),
        scratch_shapes=[pltpu.VMEM((rows, GMLP_WIDTH), BF16),
                        pltpu.VMEM((rows, GMLP_WIDTH), F32),
                        pltpu.VMEM((rows, GMLP_WIDTH), BF16),
                        pltpu.VMEM((rows, CONV_WIDTH), BF16)],
        compiler_params=_params("parallel"),
        name="mix_out_proj",
    )(x2, z, z, z, z, z, z, mix_b, ln_g, ln_b, ws, bs, conv_w, gain_a, gain_c, w, w, w, w)


def _attn_kernel(lq1_ref, lk1_ref, lq2_ref, lk2_ref, q_ref, k_ref, v_ref, gain_ref, o_ref,
                 qs_ref, m_ref, l_ref, acc_ref, *bufs, tq, n_q, n_buf, lam_init):
    rc = ATTN_ROW_CHUNK
    n_tiles = tq // HEAD_DIM
    p_bufs, a_bufs = bufs[:n_buf], bufs[n_buf:]

    q = q_ref[0]
    zero = jnp.zeros((tq, HEAD_DIM), q.dtype)
    qs_ref[:tq, :HEAD_DIM] = q[:, :HEAD_DIM]
    qs_ref[:tq, HEAD_DIM:] = zero
    qs_ref[tq:, :HEAD_DIM] = zero
    qs_ref[tq:, HEAD_DIM:] = q[:, HEAD_DIM:]

    def key_rows(j):
        return slice(j * tq, (j + 1) * tq)

    def scores(j):
        return lax.dot_general(qs_ref[...], k_ref[0, key_rows(j), :], (((1,), (1,)), ((), ())),
                               preferred_element_type=F32)

    def softmax(s, buf, diagonal, first):
        p_ref, alpha_ref = p_bufs[buf], a_bufs[buf]
        for c in range(2 * tq // rc):
            rows = slice(c * rc, (c + 1) * rc)
            q0 = rows.start % tq
            live = [t for t in range(n_tiles) if (not diagonal) or t * HEAD_DIM <= q0 + rc - 1]
            tiles = []
            for t in live:
                st = s[rows, t * HEAD_DIM:(t + 1) * HEAD_DIM]
                if diagonal and (t + 1) * HEAD_DIM - 1 > q0:
                    row = lax.broadcasted_iota(jnp.int32, (rc, HEAD_DIM), 0) + q0
                    col = lax.broadcasted_iota(jnp.int32, (rc, HEAD_DIM), 1) + t * HEAD_DIM
                    st = jnp.where(col <= row, st, -jnp.inf)
                tiles.append(st)
            m_blk = jnp.max(functools.reduce(jnp.maximum, tiles), axis=-1, keepdims=True)
            if first:
                m_new = jnp.broadcast_to(m_blk, (rc, HEAD_DIM))
                ps = [jnp.exp2(st - m_new) for st in tiles]
                l_ref[rows, :] = functools.reduce(jnp.add, ps)
            else:
                m_old = m_ref[rows, :]
                m_new = jnp.maximum(m_old, m_blk)
                alpha = jnp.exp2(m_old - m_new)
                ps = [jnp.exp2(st - m_new) for st in tiles]
                l_ref[rows, :] = alpha * l_ref[rows, :] + functools.reduce(jnp.add, ps)
                alpha_ref[rows, :] = alpha
            m_ref[rows, :] = m_new
            for t in range(n_tiles):
                p_t = ps[live.index(t)].astype(BF16) if t in live else jnp.zeros((rc, HEAD_DIM), BF16)
                p_ref[rows, t * HEAD_DIM:(t + 1) * HEAD_DIM] = p_t

    def accumulate(j, buf, first):
        pv = jnp.dot(p_bufs[buf][...], v_ref[0, key_rows(j), :], preferred_element_type=F32)
        if first:
            acc_ref[...] = pv
        else:
            alpha = a_bufs[buf][...]
            for t in range(DIFF_V_DIM // HEAD_DIM):
                cols = slice(t * HEAD_DIM, (t + 1) * HEAD_DIM)
                acc_ref[:, cols] = acc_ref[:, cols] * alpha + pv[:, cols]

    def query_block(qi):
        s_next = scores(0)
        for j in range(qi + 1):
            s = s_next
            if j < qi:
                s_next = scores(j + 1)
            if j >= 1:
                accumulate(j - 1, (j - 1) % n_buf, first=(j == 1))
            softmax(s, j % n_buf, diagonal=(j == qi), first=(j == 0))
        accumulate(qi, qi % n_buf, first=(qi == 0))

        lam = (jnp.exp(jnp.sum(lq1_ref[...] * lk1_ref[...], axis=-1, keepdims=True))
               - jnp.exp(jnp.sum(lq2_ref[...] * lk2_ref[...], axis=-1, keepdims=True)) + lam_init)
        o = acc_ref[...] * (1.0 / jnp.sum(l_ref[...], axis=-1, keepdims=True))
        o = o[:tq] - lam * o[tq:]
        o = _group_rms(o) * (1.0 - lam_init) * gain_ref[0]
        o_ref[0] = o.astype(o_ref.dtype)

    for qi in range(n_q):
        pl.when(pl.program_id(2) == qi)(functools.partial(query_block, qi))


def _diff_attention(z3, lq1, lk1, lq2, lk2, gain_b, lam_init):
    b, s, _ = z3.shape
    tq = ATTN_TQ
    assert s % tq == 0
    n_buf = s // tq
    qb, kb, vb = Q_OFF // DIFF_V_DIM, K_OFF // DIFF_V_DIM, V_OFF // DIFF_V_DIM
    vec = pl.BlockSpec((1, HEAD_DIM), lambda bi, h, i: (0, 0))
    return pl.pallas_call(
        functools.partial(_attn_kernel, tq=tq, n_q=s // tq, n_buf=n_buf, lam_init=lam_init),
        grid=(b, N_GROUPS, s // tq),
        in_specs=[
            vec, vec, vec, vec,
            pl.BlockSpec((1, tq, DIFF_V_DIM), lambda bi, h, i: (bi, i, qb + h)),
            pl.BlockSpec((1, s, DIFF_V_DIM), lambda bi, h, i: (bi, 0, kb + h)),
            pl.BlockSpec((1, s, DIFF_V_DIM), lambda bi, h, i: (bi, 0, vb + h)),
            pl.BlockSpec((1, 1, DIFF_V_DIM), lambda bi, h, i: (h, 0, 0)),
        ],
        out_specs=pl.BlockSpec((1, tq, DIFF_V_DIM), lambda bi, h, i: (bi, i, h)),
        out_shape=jax.ShapeDtypeStruct((b, s, DIFF_WIDTH), BF16),
        scratch_shapes=[
            pltpu.VMEM((2 * tq, 2 * HEAD_DIM), BF16),
            pltpu.VMEM((2 * tq, HEAD_DIM), F32),
            pltpu.VMEM((2 * tq, HEAD_DIM), F32),
            pltpu.VMEM((2 * tq, DIFF_V_DIM), F32),
        ] + [pltpu.VMEM((2 * tq, tq), BF16)] * n_buf
          + [pltpu.VMEM((2 * tq, HEAD_DIM), F32)] * n_buf,
        compiler_params=_params("parallel", "parallel", "arbitrary"),
        name="diff_attention",
    )(lq1, lk1, lq2, lk2, z3, z3, z3, gain_b)


def _ffn_kernel(x_ref, g_ref, wg_ref, wu_ref, wd_ref, fg_ref, o_ref, h_ref, *, final_norm):
    f = pl.program_id(1)

    @pl.when(f == 0)
    def _():
        x = x_ref[...]
        h_ref[...] = _rms_rows(x, g_ref[...]).astype(BF16)
        o_ref[...] = x

    h = h_ref[...]
    gate = jnp.dot(h, wg_ref[...], preferred_element_type=F32)
    up = jnp.dot(h, wu_ref[...], preferred_element_type=F32)
    act = (gate * jax.nn.sigmoid(gate) * up).astype(BF16)
    o_ref[...] += jnp.dot(act, wd_ref[...], preferred_element_type=F32)

    if final_norm:
        @pl.when(f == pl.num_programs(1) - 1)
        def _():
            o_ref[...] = _rms_rows(o_ref[...], fg_ref[...])


def _ffn(x2, gain, wg, wu, wd, layer, final_gain, final_norm):
    m, d = x2.shape
    dff = wg.shape[2]
    tm, tf = FFN_TM, FFN_TF
    assert m % tm == 0 and dff % tf == 0
    return pl.pallas_call(
        functools.partial(_ffn_kernel, final_norm=final_norm),
        grid=(m // tm, dff // tf),
        in_specs=[
            pl.BlockSpec((tm, d), lambda i, f: (i, 0)),
            pl.BlockSpec((1, d), lambda i, f: (0, 0)),
            pl.BlockSpec((None, d, tf), lambda i, f: (layer, 0, f)),
            pl.BlockSpec((None, d, tf), lambda i, f: (layer, 0, f)),
            pl.BlockSpec((None, tf, d), lambda i, f: (layer, f, 0)),
            pl.BlockSpec((1, d), lambda i, f: (0, 0)),
        ],
        out_specs=pl.BlockSpec((tm, d), lambda i, f: (i, 0)),
        out_shape=jax.ShapeDtypeStruct((m, d), F32),
        scratch_shapes=[pltpu.VMEM((tm, d), BF16)],
        compiler_params=_params("parallel", "arbitrary"),
        name="ffn",
    )(x2, gain, wg, wu, wd, final_gain)


def kernel(x, positions, attn_norm, w_in, gmlp_ln_g, gmlp_ln_b, gmlp_ws, gmlp_bs, lambda_q1, lambda_k1,
           lambda_q2, lambda_k2, conv_w, mix_norm, w_out, ffn_norm, w_gate, w_up, w_down, final_norm):
    b, s, d = x.shape
    m = b * s
    depth = w_in.shape[0]

    inv_freq = 1.0 / (ROPE_THETA ** (jnp.arange(0, HEAD_DIM, 2, dtype=F32) / HEAD_DIM))
    ang = positions.astype(F32).reshape(m, 1) * jnp.concatenate([inv_freq, inv_freq]).reshape(1, HEAD_DIM)
    sign = jnp.concatenate([-jnp.ones((HEAD_DIM // 2,), F32), jnp.ones((HEAD_DIM // 2,), F32)]).reshape(1, HEAD_DIM)

    w_in, w_out, w_gate, w_up, w_down = (w.astype(BF16) for w in (w_in, w_out, w_gate, w_up, w_down))
    x2 = x.reshape(m, d)
    for l in range(depth):
        lam_init = 0.8 - 0.6 * math.exp(-0.3 * l)
        gain = mix_norm[l].astype(F32)
        z = _in_proj(x2, attn_norm[l].reshape(1, d), w_in, l, ang, sign)
        z3 = z.reshape(b, s, z.shape[1])
        mix_b = _diff_attention(
            z3, lambda_q1[l].reshape(1, HEAD_DIM), lambda_k1[l].reshape(1, HEAD_DIM),
            lambda_q2[l].reshape(1, HEAD_DIM), lambda_k2[l].reshape(1, HEAD_DIM),
            gain[GMLP_WIDTH:GMLP_WIDTH + DIFF_WIDTH].reshape(N_GROUPS, 1, DIFF_V_DIM), lam_init)
        x2 = _mix_out_proj(
            x2, z, mix_b.reshape(m, DIFF_WIDTH), gmlp_ln_g[l], gmlp_ln_b[l], gmlp_ws[l],
            jnp.tile(gmlp_bs[l], (1, OUT_PROJ_TM // GMLP_CHUNK)).reshape(N_GROUPS, OUT_PROJ_TM, 1), conv_w[l],
            gain[:GMLP_WIDTH].reshape(1, GMLP_WIDTH), gain[GMLP_WIDTH + DIFF_WIDTH:].reshape(1, CONV_WIDTH),
            w_out, l, s)
        x2 = _ffn(x2, ffn_norm[l].reshape(1, d), w_gate, w_up, w_down, l, final_norm.reshape(1, d),
                  final_norm=(l == depth - 1))
    return x2.reshape(b, s, d)
```

```python
import functools
import math

import jax
import jax.numpy as jnp
from jax import lax
from jax.experimental import pallas as pl
from jax.experimental.pallas import tpu as pltpu

F32 = jnp.float32
BF16 = jnp.bfloat16

HEAD_DIM = 128
N_GROUPS = 4
GMLP_WIDTH = N_GROUPS * HEAD_DIM
GMLP_CHUNK = 128
DIFF_QK_WIDTH = N_GROUPS * 2 * HEAD_DIM
DIFF_V_DIM = 2 * HEAD_DIM
DIFF_WIDTH = N_GROUPS * DIFF_V_DIM
CONV_WIDTH = N_GROUPS * HEAD_DIM
CONV_K = 3
ROPE_THETA = 10000.0
RMS_EPS = 1e-6
LN_EPS = 1e-5
LOG2_E = math.log2(math.e)

Q_OFF = 2 * GMLP_WIDTH
K_OFF = Q_OFF + DIFF_QK_WIDTH
V_OFF = K_OFF + DIFF_QK_WIDTH
C_OFF = V_OFF + DIFF_WIDTH

VMEM_LIMIT_BYTES = 56 * 1024 * 1024

IN_PROJ_TM = 512
IN_PROJ_TN = 2816
ATTN_TQ = 512
ATTN_ROW_CHUNK = 32
OUT_PROJ_TM = 512
FFN_TM = 1024
FFN_TF = 512
CONV_HALO_ROWS = 8


def _params(*semantics):
    return pltpu.CompilerParams(dimension_semantics=semantics, vmem_limit_bytes=VMEM_LIMIT_BYTES)


def _rms_rows(x, gain):
    return x * lax.rsqrt(jnp.mean(x * x, axis=-1, keepdims=True) + RMS_EPS) * gain


def _group_rms(x):
    return x * lax.rsqrt(jnp.mean(x * x, axis=-1, keepdims=True) + RMS_EPS)


def _gelu(x):
    return 0.5 * x * (1.0 + lax.erf(x * math.sqrt(0.5)))


def _in_proj_kernel(x_ref, g_ref, w_ref, ang_ref, sign_ref, z_ref, h_ref, cos_ref, sin_ref, *, tn, n_col_tiles):
    j = pl.program_id(1)

    def tile(col0):
        first = col0 == (n_col_tiles - 1) * tn
        if first:
            h_ref[...] = _rms_rows(x_ref[...], g_ref[...]).astype(BF16)
        acc = jnp.dot(h_ref[...], w_ref[...], preferred_element_type=F32)
        if first:
            cos_ref[...] = jnp.cos(ang_ref[...])
            sin_ref[...] = jnp.sin(ang_ref[...]) * sign_ref[...]
        cosf, sinf = cos_ref[...], sin_ref[...]
        for c in range(tn // HEAD_DIM):
            col = col0 + c * HEAD_DIM
            r = acc[:, c * HEAD_DIM:(c + 1) * HEAD_DIM]
            if col < Q_OFF:
                r = _gelu(r)
            elif col < V_OFF:
                r = r * cosf + pltpu.roll(r, HEAD_DIM // 2, 1) * sinf
                if col < K_OFF:
                    r = r * (LOG2_E / math.sqrt(HEAD_DIM))
            z_ref[:, c * HEAD_DIM:(c + 1) * HEAD_DIM] = r.astype(z_ref.dtype)

    for jt in range(n_col_tiles):
        pl.when(j == n_col_tiles - 1 - jt)(functools.partial(tile, jt * tn))


def _in_proj(x2, gain, w, layer, ang, sign):
    m, d = x2.shape
    n = w.shape[2]
    tm, tn = IN_PROJ_TM, IN_PROJ_TN
    assert m % tm == 0 and n % tn == 0 and tn % HEAD_DIM == 0
    return pl.pallas_call(
        functools.partial(_in_proj_kernel, tn=tn, n_col_tiles=n // tn),
        grid=(m // tm, n // tn),
        in_specs=[
            pl.BlockSpec((tm, d), lambda i, j: (i, 0)),
            pl.BlockSpec((1, d), lambda i, j: (0, 0)),
            pl.BlockSpec((None, d, tn), lambda i, j: (layer, 0, n // tn - 1 - j)),
            pl.BlockSpec((tm, HEAD_DIM), lambda i, j: (i, 0)),
            pl.BlockSpec((1, HEAD_DIM), lambda i, j: (0, 0)),
        ],
        out_specs=pl.BlockSpec((tm, tn), lambda i, j: (i, n // tn - 1 - j)),
        out_shape=jax.ShapeDtypeStruct((m, n), BF16),
        scratch_shapes=[pltpu.VMEM((tm, d), BF16),
                        pltpu.VMEM((tm, HEAD_DIM), F32),
                        pltpu.VMEM((tm, HEAD_DIM), F32)],
        compiler_params=_params("parallel", "arbitrary"),
        name="in_proj",
    )(x2, gain, w, ang, sign)


def _gmlp_mixer(za_ref, lng_ref, lnb_ref, ws_ref, bs_ref, ga_ref, vn_ref, mixed_ref, out_ref, rows):
    groups = [slice(g * HEAD_DIM, (g + 1) * HEAD_DIM) for g in range(N_GROUPS)]
    for g, cols in enumerate(groups):
        v = za_ref[:, GMLP_WIDTH + cols.start:GMLP_WIDTH + cols.stop].astype(F32)
        mu = jnp.mean(v, axis=-1, keepdims=True)
        vc = v - mu
        var = jnp.mean(vc * vc, axis=-1, keepdims=True)
        vn = vc * lax.rsqrt(var + LN_EPS) * lng_ref[g:g + 1, :] + lnb_ref[g:g + 1, :]
        vn_ref[:, cols] = vn.astype(BF16)
    t_idx = lax.broadcasted_iota(jnp.int32, (GMLP_CHUNK, GMLP_CHUNK), 0)
    s_idx = lax.broadcasted_iota(jnp.int32, (GMLP_CHUNK, GMLP_CHUNK), 1)
    for g, cols in enumerate(groups):
        w = jnp.where(s_idx <= t_idx, ws_ref[g], 0.0).astype(BF16)
        for c in range(rows // GMLP_CHUNK):
            chunk = slice(c * GMLP_CHUNK, (c + 1) * GMLP_CHUNK)
            mixed_ref[chunk, cols] = jnp.dot(w, vn_ref[chunk, cols], preferred_element_type=F32)
    for g, cols in enumerate(groups):
        u = za_ref[:, cols].astype(F32)
        ya = _group_rms(u * (mixed_ref[:, cols] + bs_ref[g])) * ga_ref[:, cols]
        out_ref[:, cols] = ya.astype(out_ref.dtype)


def _short_conv_mixer(bg_ref, cg_ref, hc_ref, cgh_ref, hch_ref, cw_ref, gc_ref, out_ref, at_seq_start):
    xh = cg_ref[...].astype(F32) * hc_ref[...].astype(F32)
    halo = cgh_ref[...].astype(F32) * hch_ref[...].astype(F32)
    halo = jnp.where(at_seq_start, 0.0, halo)
    row = lax.broadcasted_iota(jnp.int32, xh.shape, 0)
    prev1 = jnp.where(row == 0, halo[CONV_HALO_ROWS - 1:CONV_HALO_ROWS, :], pltpu.roll(xh, 1, 0))
    prev2 = jnp.where(row == 0, halo[CONV_HALO_ROWS - 2:CONV_HALO_ROWS - 1, :],
                      jnp.where(row == 1, halo[CONV_HALO_ROWS - 1:CONV_HALO_ROWS, :], pltpu.roll(xh, 2, 0)))
    y = cw_ref[0:1, :] * prev2 + cw_ref[1:2, :] * prev1 + cw_ref[2:3, :] * xh
    yc = bg_ref[...].astype(F32) * y
    for g in range(N_GROUPS):
        cols = slice(g * HEAD_DIM, (g + 1) * HEAD_DIM)
        out_ref[:, cols] = (_group_rms(yc[:, cols]) * gc_ref[:, cols]).astype(out_ref.dtype)


def _mix_out_kernel(x_ref, za_ref, bg_ref, cg_ref, hc_ref, cgh_ref, hch_ref, mb_ref, lng_ref, lnb_ref, ws_ref,
                    bs_ref, cw_ref, ga_ref, gc_ref, w0_ref, w1_ref, w2_ref, w3_ref, o_ref, vn_ref, mixed_ref,
                    ma_ref, mc_ref, *, rows, blocks_per_seq):
    acc = jnp.dot(mb_ref[:, :GMLP_WIDTH], w1_ref[...], preferred_element_type=F32)
    acc = acc + jnp.dot(mb_ref[:, GMLP_WIDTH:], w2_ref[...], preferred_element_type=F32)
    _gmlp_mixer(za_ref, lng_ref, lnb_ref, ws_ref, bs_ref, ga_ref, vn_ref, mixed_ref, ma_ref, rows)
    _short_conv_mixer(bg_ref, cg_ref, hc_ref, cgh_ref, hch_ref, cw_ref, gc_ref, mc_ref,
                      pl.program_id(0) % blocks_per_seq == 0)
    acc = acc + jnp.dot(ma_ref[...], w0_ref[...], preferred_element_type=F32)
    acc = acc + jnp.dot(mc_ref[...], w3_ref[...], preferred_element_type=F32)
    o_ref[...] = x_ref[...] + acc


def _mix_out_proj(x2, z, mix_b, ln_g, ln_b, ws, bs, conv_w, gain_a, gain_c, w, layer, seq_len):
    m, d = x2.shape
    rows = OUT_PROJ_TM
    assert m % rows == 0 and seq_len % rows == 0 and rows % GMLP_CHUNK == 0 and C_OFF % CONV_WIDTH == 0
    assert DIFF_WIDTH == 2 * GMLP_WIDTH and GMLP_WIDTH == CONV_WIDTH
    cb = C_OFF // CONV_WIDTH
    hb = rows // CONV_HALO_ROWS
    full = lambda shape: pl.BlockSpec(shape, lambda i: (0,) * len(shape))
    zcol = lambda width, col: pl.BlockSpec((rows, width), lambda i: (i, col))
    halo = lambda col: pl.BlockSpec((CONV_HALO_ROWS, CONV_WIDTH), lambda i: (jnp.maximum(i * hb - 1, 0), col))
    wrows = lambda blk: pl.BlockSpec((None, GMLP_WIDTH, d), lambda i: (layer, blk, 0))
    return pl.pallas_call(
        functools.partial(_mix_out_kernel, rows=rows, blocks_per_seq=seq_len // rows),
        grid=(m // rows,),
        in_specs=[
            pl.BlockSpec((rows, d), lambda i: (i, 0)),
            zcol(2 * GMLP_WIDTH, 0), zcol(CONV_WIDTH, cb), zcol(CONV_WIDTH, cb + 1), zcol(CONV_WIDTH, cb + 2),
            halo(cb + 1), halo(cb + 2),
            pl.BlockSpec((rows, DIFF_WIDTH), lambda i: (i, 0)),
            full((N_GROUPS, HEAD_DIM)), full((N_GROUPS, HEAD_DIM)),
            full((N_GROUPS, GMLP_CHUNK, GMLP_CHUNK)), full((N_GROUPS, rows, 1)),
            full((CONV_K, CONV_WIDTH)), full((1, GMLP_WIDTH)), full((1, CONV_WIDTH)),
            wrows(0), wrows(1), wrows(2), wrows(3),
        ],
        out_specs=pl.BlockSpec((rows, d), lambda i: (i, 0)),
        out_shape=jax.ShapeDtypeStruct((m, d), F32),
        scratch_shapes=[pltpu.VMEM((rows, GMLP_WIDTH), BF16),
                        pltpu.VMEM((rows, GMLP_WIDTH), F32),
                        pltpu.VMEM((rows, GMLP_WIDTH), BF16),
                        pltpu.VMEM((rows, CONV_WIDTH), BF16)],
        compiler_params=_params("parallel"),
        name="mix_out_proj",
    )(x2, z, z, z, z, z, z, mix_b, ln_g, ln_b, ws, bs, conv_w, gain_a, gain_c, w, w, w, w)


def _attn_kernel(lq1_ref, lk1_ref, lq2_ref, lk2_ref, q_ref, k_ref, v_ref, gain_ref, o_ref,
                 m_ref, l_ref, acc_ref, *bufs, tq, n_q, n_buf, lam_init):
    rc = ATTN_ROW_CHUNK
    n_tiles = tq // HEAD_DIM
    p_bufs, a_bufs = bufs[:n_buf], bufs[n_buf:]

    def key_rows(j):
        return slice(j * tq, (j + 1) * tq)

    def scores(j):
        return [lax.dot_general(q_ref[0, :, mp * HEAD_DIM:(mp + 1) * HEAD_DIM],
                                k_ref[0, key_rows(j), mp * HEAD_DIM:(mp + 1) * HEAD_DIM],
                                (((1,), (1,)), ((), ())), preferred_element_type=F32) for mp in range(2)]

    def softmax(s_maps, buf, diagonal, first):
        p_ref, alpha_ref = p_bufs[buf], a_bufs[buf]
        for c in range(2 * tq // rc):
            rows = slice(c * rc, (c + 1) * rc)
            q0 = rows.start % tq
            s = s_maps[rows.start // tq]
            s_rows = slice(q0, q0 + rc)
            live = [t for t in range(n_tiles) if (not diagonal) or t * HEAD_DIM <= q0 + rc - 1]
            tiles = []
            for t in live:
                st = s[s_rows, t * HEAD_DIM:(t + 1) * HEAD_DIM]
                if diagonal and (t + 1) * HEAD_DIM - 1 > q0:
                    row = lax.broadcasted_iota(jnp.int32, (rc, HEAD_DIM), 0) + q0
                    col = lax.broadcasted_iota(jnp.int32, (rc, HEAD_DIM), 1) + t * HEAD_DIM
                    st = jnp.where(col <= row, st, -jnp.inf)
                tiles.append(st)
            m_blk = jnp.max(functools.reduce(jnp.maximum, tiles), axis=-1, keepdims=True)
            if first:
                m_new = jnp.broadcast_to(m_blk, (rc, HEAD_DIM))
                ps = [jnp.exp2(st - m_new) for st in tiles]
                l_ref[rows, :] = functools.reduce(jnp.add, ps)
            else:
                m_old = m_ref[rows, :]
                m_new = jnp.maximum(m_old, m_blk)
                alpha = jnp.exp2(m_old - m_new)
                ps = [jnp.exp2(st - m_new) for st in tiles]
                l_ref[rows, :] = alpha * l_ref[rows, :] + functools.reduce(jnp.add, ps)
                alpha_ref[rows, :] = alpha
            m_ref[rows, :] = m_new
            for t in range(n_tiles):
                p_t = ps[live.index(t)].astype(BF16) if t in live else jnp.zeros((rc, HEAD_DIM), BF16)
                p_ref[rows, t * HEAD_DIM:(t + 1) * HEAD_DIM] = p_t

    def accumulate(j, buf, first):
        pv = jnp.dot(p_bufs[buf][...], v_ref[0, key_rows(j), :], preferred_element_type=F32)
        if first:
            acc_ref[...] = pv
        else:
            alpha = a_bufs[buf][...]
            for t in range(DIFF_V_DIM // HEAD_DIM):
                cols = slice(t * HEAD_DIM, (t + 1) * HEAD_DIM)
                acc_ref[:, cols] = acc_ref[:, cols] * alpha + pv[:, cols]

    def query_block(qi):
        s_next = scores(0)
        for j in range(qi + 1):
            s = s_next
            if j < qi:
                s_next = scores(j + 1)
            if j >= 1:
                accumulate(j - 1, (j - 1) % n_buf, first=(j == 1))
            softmax(s, j % n_buf, diagonal=(j == qi), first=(j == 0))
        accumulate(qi, qi % n_buf, first=(qi == 0))

        lam = (jnp.exp(jnp.sum(lq1_ref[...] * lk1_ref[...], axis=-1, keepdims=True))
               - jnp.exp(jnp.sum(lq2_ref[...] * lk2_ref[...], axis=-1, keepdims=True)) + lam_init)
        o = acc_ref[...] * (1.0 / jnp.sum(l_ref[...], axis=-1, keepdims=True))
        o = o[:tq] - lam * o[tq:]
        o = _group_rms(o) * (1.0 - lam_init) * gain_ref[0]
        o_ref[0] = o.astype(o_ref.dtype)

    for qi in range(n_q):
        pl.when(pl.program_id(2) == qi)(functools.partial(query_block, qi))


def _diff_attention(z3, lq1, lk1, lq2, lk2, gain_b, lam_init):
    b, s, _ = z3.shape
    tq = ATTN_TQ
    assert s % tq == 0
    n_buf = s // tq
    qb, kb, vb = Q_OFF // DIFF_V_DIM, K_OFF // DIFF_V_DIM, V_OFF // DIFF_V_DIM
    vec = pl.BlockSpec((1, HEAD_DIM), lambda bi, h, i: (0, 0))
    return pl.pallas_call(
        functools.partial(_attn_kernel, tq=tq, n_q=s // tq, n_buf=n_buf, lam_init=lam_init),
        grid=(b, N_GROUPS, s // tq),
        in_specs=[
            vec, vec, vec, vec,
            pl.BlockSpec((1, tq, DIFF_V_DIM), lambda bi, h, i: (bi, i, qb + h)),
            pl.BlockSpec((1, s, DIFF_V_DIM), lambda bi, h, i: (bi, 0, kb + h)),
            pl.BlockSpec((1, s, DIFF_V_DIM), lambda bi, h, i: (bi, 0, vb + h)),
            pl.BlockSpec((1, 1, DIFF_V_DIM), lambda bi, h, i: (h, 0, 0)),
        ],
        out_specs=pl.BlockSpec((1, tq, DIFF_V_DIM), lambda bi, h, i: (bi, i, h)),
        out_shape=jax.ShapeDtypeStruct((b, s, DIFF_WIDTH), BF16),
        scratch_shapes=[
            pltpu.VMEM((2 * tq, HEAD_DIM), F32),
            pltpu.VMEM((2 * tq, HEAD_DIM), F32),
            pltpu.VMEM((2 * tq, DIFF_V_DIM), F32),
        ] + [pltpu.VMEM((2 * tq, tq), BF16)] * n_buf
          + [pltpu.VMEM((2 * tq, HEAD_DIM), F32)] * n_buf,
        compiler_params=_params("parallel", "parallel", "arbitrary"),
        name="diff_attention",
    )(lq1, lk1, lq2, lk2, z3, z3, z3, gain_b)


def _ffn_kernel(x_ref, g_ref, wg_ref, wu_ref, wd_ref, fg_ref, o_ref, h_ref, *, final_norm):
    f = pl.program_id(1)
    last_f = pl.num_programs(1) - 1

    def step(first, last):
        if first:
            h_ref[...] = _rms_rows(x_ref[...], g_ref[...]).astype(BF16)
        h = h_ref[...]
        gate = jnp.dot(h, wg_ref[...], preferred_element_type=F32)
        up = jnp.dot(h, wu_ref[...], preferred_element_type=F32)
        act = (gate * jax.nn.sigmoid(gate) * up).astype(BF16)
        out = (x_ref[...] if first else o_ref[...]) + jnp.dot(act, wd_ref[...], preferred_element_type=F32)
        if last and final_norm:
            out = _rms_rows(out, fg_ref[...])
        o_ref[...] = out

    pl.when(f == 0)(functools.partial(step, True, False))
    if final_norm:
        pl.when((f > 0) & (f < last_f))(functools.partial(step, False, False))
        pl.when(f == last_f)(functools.partial(step, False, True))
    else:
        pl.when(f > 0)(functools.partial(step, False, False))


def _ffn(x2, gain, wg, wu, wd, layer, final_gain, final_norm):
    m, d = x2.shape
    dff = wg.shape[2]
    tm, tf = FFN_TM, FFN_TF
    assert m % tm == 0 and dff % tf == 0
    return pl.pallas_call(
        functools.partial(_ffn_kernel, final_norm=final_norm),
        grid=(m // tm, dff // tf),
        in_specs=[
            pl.BlockSpec((tm, d), lambda i, f: (i, 0)),
            pl.BlockSpec((1, d), lambda i, f: (0, 0)),
            pl.BlockSpec((None, d, tf), lambda i, f: (layer, 0, f)),
            pl.BlockSpec((None, d, tf), lambda i, f: (layer, 0, f)),
            pl.BlockSpec((None, tf, d), lambda i, f: (layer, f, 0)),
            pl.BlockSpec((1, d), lambda i, f: (0, 0)),
        ],
        out_specs=pl.BlockSpec((tm, d), lambda i, f: (i, 0)),
        out_shape=jax.ShapeDtypeStruct((m, d), F32),
        scratch_shapes=[pltpu.VMEM((tm, d), BF16)],
        compiler_params=_params("parallel", "arbitrary"),
        name="ffn",
    )(x2, gain, wg, wu, wd, final_gain)


def kernel(x, positions, attn_norm, w_in, gmlp_ln_g, gmlp_ln_b, gmlp_ws, gmlp_bs, lambda_q1, lambda_k1,
           lambda_q2, lambda_k2, conv_w, mix_norm, w_out, ffn_norm, w_gate, w_up, w_down, final_norm):
    b, s, d = x.shape
    m = b * s
    depth = w_in.shape[0]

    inv_freq = 1.0 / (ROPE_THETA ** (jnp.arange(0, HEAD_DIM, 2, dtype=F32) / HEAD_DIM))
    ang = positions.astype(F32).reshape(m, 1) * jnp.concatenate([inv_freq, inv_freq]).reshape(1, HEAD_DIM)
    sign = jnp.concatenate([-jnp.ones((HEAD_DIM // 2,), F32), jnp.ones((HEAD_DIM // 2,), F32)]).reshape(1, HEAD_DIM)

    w_in, w_out, w_gate, w_up, w_down = (w.astype(BF16) for w in (w_in, w_out, w_gate, w_up, w_down))
    x2 = x.reshape(m, d)
    for l in range(depth):
        lam_init = 0.8 - 0.6 * math.exp(-0.3 * l)
        gain = mix_norm[l].astype(F32)
        z = _in_proj(x2, attn_norm[l].reshape(1, d), w_in, l, ang, sign)
        z3 = z.reshape(b, s, z.shape[1])
        mix_b = _diff_attention(
            z3, lambda_q1[l].reshape(1, HEAD_DIM), lambda_k1[l].reshape(1, HEAD_DIM),
            lambda_q2[l].reshape(1, HEAD_DIM), lambda_k2[l].reshape(1, HEAD_DIM),
            gain[GMLP_WIDTH:GMLP_WIDTH + DIFF_WIDTH].reshape(N_GROUPS, 1, DIFF_V_DIM), lam_init)
        x2 = _mix_out_proj(
            x2, z, mix_b.reshape(m, DIFF_WIDTH), gmlp_ln_g[l], gmlp_ln_b[l], gmlp_ws[l],
            jnp.tile(gmlp_bs[l], (1, OUT_PROJ_TM // GMLP_CHUNK)).reshape(N_GROUPS, OUT_PROJ_TM, 1), conv_w[l],
            gain[:GMLP_WIDTH].reshape(1, GMLP_WIDTH), gain[GMLP_WIDTH + DIFF_WIDTH:].reshape(1, CONV_WIDTH),
            w_out, l, s)
        x2 = _ffn(x2, ffn_norm[l].reshape(1, d), w_gate, w_up, w_down, l, final_norm.reshape(1, d),
                  final_norm=(l == depth - 1))
    return x2.reshape(b, s, d)
```

```python
import functools
import math

import jax
import jax.numpy as jnp
from jax import lax
from jax.experimental import pallas as pl
from jax.experimental.pallas import tpu as pltpu

F32 = jnp.float32
BF16 = jnp.bfloat16

HEAD_DIM = 128
N_GROUPS = 4
GMLP_WIDTH = N_GROUPS * HEAD_DIM
GMLP_CHUNK = 128
DIFF_QK_WIDTH = N_GROUPS * 2 * HEAD_DIM
DIFF_V_DIM = 2 * HEAD_DIM
DIFF_WIDTH = N_GROUPS * DIFF_V_DIM
CONV_WIDTH = N_GROUPS * HEAD_DIM
CONV_K = 3
ROPE_THETA = 10000.0
RMS_EPS = 1e-6
LN_EPS = 1e-5
LOG2_E = math.log2(math.e)

Q_OFF = 2 * GMLP_WIDTH
K_OFF = Q_OFF + DIFF_QK_WIDTH
V_OFF = K_OFF + DIFF_QK_WIDTH
C_OFF = V_OFF + DIFF_WIDTH

VMEM_LIMIT_BYTES = 56 * 1024 * 1024

IN_PROJ_TM = 512
IN_PROJ_TN = 2816
ATTN_TQ = 512
ATTN_ROW_CHUNK = 32
OUT_PROJ_TM = 512
FFN_TM = 1024
FFN_TF = 512
CONV_HALO_ROWS = 8


def _params(*semantics):
    return pltpu.CompilerParams(dimension_semantics=semantics, vmem_limit_bytes=VMEM_LIMIT_BYTES)


def _rms_rows(x, gain):
    return x * lax.rsqrt(jnp.mean(x * x, axis=-1, keepdims=True) + RMS_EPS) * gain


def _group_rms(x):
    return x * lax.rsqrt(jnp.mean(x * x, axis=-1, keepdims=True) + RMS_EPS)


def _gelu(x):
    return 0.5 * x * (1.0 + lax.erf(x * math.sqrt(0.5)))


def _in_proj_kernel(x_ref, g_ref, w_ref, ang_ref, sign_ref, z_ref, h_ref, cos_ref, sin_ref, *, tn, n_col_tiles):
    j = pl.program_id(1)

    def tile(col0):
        first = col0 == (n_col_tiles - 1) * tn
        if first:
            h_ref[...] = _rms_rows(x_ref[...], g_ref[...]).astype(BF16)
        acc = jnp.dot(h_ref[...], w_ref[...], preferred_element_type=F32)
        if first:
            cos_ref[...] = jnp.cos(ang_ref[...])
            sin_ref[...] = jnp.sin(ang_ref[...]) * sign_ref[...]
        cosf, sinf = cos_ref[...], sin_ref[...]
        for c in range(tn // HEAD_DIM):
            col = col0 + c * HEAD_DIM
            r = acc[:, c * HEAD_DIM:(c + 1) * HEAD_DIM]
            if col < Q_OFF:
                r = _gelu(r)
            elif col < V_OFF:
                r = r * cosf + pltpu.roll(r, HEAD_DIM // 2, 1) * sinf
                if col < K_OFF:
                    r = r * (LOG2_E / math.sqrt(HEAD_DIM))
            z_ref[:, c * HEAD_DIM:(c + 1) * HEAD_DIM] = r.astype(z_ref.dtype)

    for jt in range(n_col_tiles):
        pl.when(j == n_col_tiles - 1 - jt)(functools.partial(tile, jt * tn))


def _in_proj(x2, gain, w, layer, ang, sign):
    m, d = x2.shape
    n = w.shape[2]
    tm, tn = IN_PROJ_TM, IN_PROJ_TN
    assert m % tm == 0 and n % tn == 0 and tn % HEAD_DIM == 0
    return pl.pallas_call(
        functools.partial(_in_proj_kernel, tn=tn, n_col_tiles=n // tn),
        grid=(m // tm, n // tn),
        in_specs=[
            pl.BlockSpec((tm, d), lambda i, j: (i, 0)),
            pl.BlockSpec((1, d), lambda i, j: (0, 0)),
            pl.BlockSpec((None, d, tn), lambda i, j: (layer, 0, n // tn - 1 - j)),
            pl.BlockSpec((tm, HEAD_DIM), lambda i, j: (i, 0)),
            pl.BlockSpec((1, HEAD_DIM), lambda i, j: (0, 0)),
        ],
        out_specs=pl.BlockSpec((tm, tn), lambda i, j: (i, n // tn - 1 - j)),
        out_shape=jax.ShapeDtypeStruct((m, n), BF16),
        scratch_shapes=[pltpu.VMEM((tm, d), BF16),
                        pltpu.VMEM((tm, HEAD_DIM), F32),
                        pltpu.VMEM((tm, HEAD_DIM), F32)],
        compiler_params=_params("parallel", "arbitrary"),
        name="in_proj",
    )(x2, gain, w, ang, sign)


def _gmlp_mixer(za_ref, lng_ref, lnb_ref, ws_ref, bs_ref, ga_ref, vn_ref, mixed_ref, out_ref, rows):
    groups = [slice(g * HEAD_DIM, (g + 1) * HEAD_DIM) for g in range(N_GROUPS)]
    for g, cols in enumerate(groups):
        v = za_ref[:, GMLP_WIDTH + cols.start:GMLP_WIDTH + cols.stop].astype(F32)
        mu = jnp.mean(v, axis=-1, keepdims=True)
        vc = v - mu
        var = jnp.mean(vc * vc, axis=-1, keepdims=True)
        vn = vc * lax.rsqrt(var + LN_EPS) * lng_ref[g:g + 1, :] + lnb_ref[g:g + 1, :]
        vn_ref[:, cols] = vn.astype(BF16)
    t_idx = lax.broadcasted_iota(jnp.int32, (GMLP_CHUNK, GMLP_CHUNK), 0)
    s_idx = lax.broadcasted_iota(jnp.int32, (GMLP_CHUNK, GMLP_CHUNK), 1)
    for g, cols in enumerate(groups):
        w = jnp.where(s_idx <= t_idx, ws_ref[g], 0.0).astype(BF16)
        for c in range(rows // GMLP_CHUNK):
            chunk = slice(c * GMLP_CHUNK, (c + 1) * GMLP_CHUNK)
            mixed_ref[chunk, cols] = jnp.dot(w, vn_ref[chunk, cols], preferred_element_type=F32)
    for g, cols in enumerate(groups):
        u = za_ref[:, cols].astype(F32)
        ya = _group_rms(u * (mixed_ref[:, cols] + bs_ref[g])) * ga_ref[:, cols]
        out_ref[:, cols] = ya.astype(out_ref.dtype)


def _short_conv_mixer(bg_ref, cg_ref, hc_ref, cgh_ref, hch_ref, cw_ref, gc_ref, out_ref, at_seq_start):
    xh = cg_ref[...].astype(F32) * hc_ref[...].astype(F32)
    halo = cgh_ref[...].astype(F32) * hch_ref[...].astype(F32)
    halo = jnp.where(at_seq_start, 0.0, halo)
    row = lax.broadcasted_iota(jnp.int32, xh.shape, 0)
    prev1 = jnp.where(row == 0, halo[CONV_HALO_ROWS - 1:CONV_HALO_ROWS, :], pltpu.roll(xh, 1, 0))
    prev2 = jnp.where(row == 0, halo[CONV_HALO_ROWS - 2:CONV_HALO_ROWS - 1, :],
                      jnp.where(row == 1, halo[CONV_HALO_ROWS - 1:CONV_HALO_ROWS, :], pltpu.roll(xh, 2, 0)))
    y = cw_ref[0:1, :] * prev2 + cw_ref[1:2, :] * prev1 + cw_ref[2:3, :] * xh
    yc = bg_ref[...].astype(F32) * y
    for g in range(N_GROUPS):
        cols = slice(g * HEAD_DIM, (g + 1) * HEAD_DIM)
        out_ref[:, cols] = (_group_rms(yc[:, cols]) * gc_ref[:, cols]).astype(out_ref.dtype)


def _mix_out_kernel(x_ref, za_ref, bg_ref, cg_ref, hc_ref, cgh_ref, hch_ref, mb_ref, lng_ref, lnb_ref, ws_ref,
                    bs_ref, cw_ref, ga_ref, gc_ref, w0_ref, w1_ref, w2_ref, w3_ref, o_ref, vn_ref, mixed_ref,
                    ma_ref, mc_ref, *, rows, blocks_per_seq):
    acc = jnp.dot(mb_ref[:, :GMLP_WIDTH], w1_ref[...], preferred_element_type=F32)
    acc = acc + jnp.dot(mb_ref[:, GMLP_WIDTH:], w2_ref[...], preferred_element_type=F32)
    _gmlp_mixer(za_ref, lng_ref, lnb_ref, ws_ref, bs_ref, ga_ref, vn_ref, mixed_ref, ma_ref, rows)
    _short_conv_mixer(bg_ref, cg_ref, hc_ref, cgh_ref, hch_ref, cw_ref, gc_ref, mc_ref,
                      pl.program_id(0) % blocks_per_seq == 0)
    acc = acc + jnp.dot(ma_ref[...], w0_ref[...], preferred_element_type=F32)
    acc = acc + jnp.dot(mc_ref[...], w3_ref[...], preferred_element_type=F32)
    o_ref[...] = x_ref[...] + acc


def _mix_out_proj(x2, z, mix_b, ln_g, ln_b, ws, bs, conv_w, gain_a, gain_c, w, layer, seq_len):
    m, d = x2.shape
    rows = OUT_PROJ_TM
    assert m % rows == 0 and seq_len % rows == 0 and rows % GMLP_CHUNK == 0 and C_OFF % CONV_WIDTH == 0
    assert DIFF_WIDTH == 2 * GMLP_WIDTH and GMLP_WIDTH == CONV_WIDTH
    cb = C_OFF // CONV_WIDTH
    hb = rows // CONV_HALO_ROWS
    full = lambda shape: pl.BlockSpec(shape, lambda i: (0,) * len(shape))
    zcol = lambda width, col: pl.BlockSpec((rows, width), lambda i: (i, col))
    halo = lambda col: pl.BlockSpec((CONV_HALO_ROWS, CONV_WIDTH), lambda i: (jnp.maximum(i * hb - 1, 0), col))
    wrows = lambda blk: pl.BlockSpec((None, GMLP_WIDTH, d), lambda i: (layer, blk, 0))
    return pl.pallas_call(
        functools.partial(_mix_out_kernel, rows=rows, blocks_per_seq=seq_len // rows),
        grid=(m // rows,),
        in_specs=[
            pl.BlockSpec((rows, d), lambda i: (i, 0)),
            zcol(2 * GMLP_WIDTH, 0), zcol(CONV_WIDTH, cb), zcol(CONV_WIDTH, cb + 1), zcol(CONV_WIDTH, cb + 2),
            halo(cb + 1), halo(cb + 2),
            pl.BlockSpec((rows, DIFF_WIDTH), lambda i: (i, 0)),
            full((N_GROUPS, HEAD_DIM)), full((N_GROUPS, HEAD_DIM)),
            full((N_GROUPS, GMLP_CHUNK, GMLP_CHUNK)), full((N_GROUPS, rows, 1)),
            full((CONV_K, CONV_WIDTH)), full((1, GMLP_WIDTH)), full((1, CONV_WIDTH)),
            wrows(0), wrows(1), wrows(2), wrows(3),
        ],
        out_specs=pl.BlockSpec((rows, d), lambda i: (i, 0)),
        out_shape=jax.ShapeDtypeStruct((m, d), F32),
        scratch_shapes=[pltpu.VMEM((rows, GMLP_WIDTH), BF16),
                        pltpu.VMEM((rows, GMLP_WIDTH), F32),
                        pltpu.VMEM((rows, GMLP_WIDTH), BF16),
                        pltpu.VMEM((rows, CONV_WIDTH), BF16)],
        compiler_params=_params("parallel"),
        name="mix_out_proj",
    )(x2, z, z, z, z, z, z, mix_b, ln_g, ln_b, ws, bs, conv_w, gain_a, gain_c, w, w, w, w)


def _attn_kernel(lq1_ref, lk1_ref, lq2_ref, lk2_ref, q_ref, k_ref, v_ref, gain_ref, o_ref,
                 m_ref, l_ref, acc_ref, *bufs, tq, n_q, n_buf, lam_init):
    rc = ATTN_ROW_CHUNK
    n_tiles = tq // HEAD_DIM
    p_bufs, a_bufs = bufs[:n_buf], bufs[n_buf:]

    def key_rows(j):
        return slice(j * tq, (j + 1) * tq)

    def scores(j):
        return [lax.dot_general(q_ref[0, :, mp * HEAD_DIM:(mp + 1) * HEAD_DIM],
                                k_ref[0, key_rows(j), mp * HEAD_DIM:(mp + 1) * HEAD_DIM],
                                (((1,), (1,)), ((), ())), preferred_element_type=F32) for mp in range(2)]

    def softmax(s_maps, buf, diagonal, first):
        p_ref, alpha_ref = p_bufs[buf], a_bufs[buf]
        for c in range(2 * tq // rc):
            rows = slice(c * rc, (c + 1) * rc)
            q0 = rows.start % tq
            s = s_maps[rows.start // tq]
            s_rows = slice(q0, q0 + rc)
            live = [t for t in range(n_tiles) if (not diagonal) or t * HEAD_DIM <= q0 + rc - 1]
            tiles = []
            for t in live:
                st = s[s_rows, t * HEAD_DIM:(t + 1) * HEAD_DIM]
                if diagonal and (t + 1) * HEAD_DIM - 1 > q0:
                    row = lax.broadcasted_iota(jnp.int32, (rc, HEAD_DIM), 0) + q0
                    col = lax.broadcasted_iota(jnp.int32, (rc, HEAD_DIM), 1) + t * HEAD_DIM
                    st = jnp.where(col <= row, st, -jnp.inf)
                tiles.append(st)
            m_blk = jnp.max(functools.reduce(jnp.maximum, tiles), axis=-1, keepdims=True)
            if first:
                m_new = jnp.broadcast_to(m_blk, (rc, HEAD_DIM))
                ps = [jnp.exp2((st - m_new).astype(BF16)) for st in tiles]
                l_ref[rows, :] = functools.reduce(jnp.add, ps).astype(F32)
            else:
                m_old = m_ref[rows, :]
                m_new = jnp.maximum(m_old, m_blk)
                alpha = jnp.exp2(m_old - m_new)
                ps = [jnp.exp2((st - m_new).astype(BF16)) for st in tiles]
                l_ref[rows, :] = alpha * l_ref[rows, :] + functools.reduce(jnp.add, ps).astype(F32)
                alpha_ref[rows, :] = alpha
            m_ref[rows, :] = m_new
            for t in range(n_tiles):
                p_t = ps[live.index(t)] if t in live else jnp.zeros((rc, HEAD_DIM), BF16)
                p_ref[rows, t * HEAD_DIM:(t + 1) * HEAD_DIM] = p_t

    def accumulate(j, buf, first):
        pv = jnp.dot(p_bufs[buf][...], v_ref[0, key_rows(j), :], preferred_element_type=F32)
        if first:
            acc_ref[...] = pv
        else:
            alpha = a_bufs[buf][...]
            for t in range(DIFF_V_DIM // HEAD_DIM):
                cols = slice(t * HEAD_DIM, (t + 1) * HEAD_DIM)
                acc_ref[:, cols] = acc_ref[:, cols] * alpha + pv[:, cols]

    def query_block(qi):
        s_next = scores(0)
        for j in range(qi + 1):
            s = s_next
            if j < qi:
                s_next = scores(j + 1)
            if j >= 1:
                accumulate(j - 1, (j - 1) % n_buf, first=(j == 1))
            softmax(s, j % n_buf, diagonal=(j == qi), first=(j == 0))
        accumulate(qi, qi % n_buf, first=(qi == 0))

        lam = (jnp.exp(jnp.sum(lq1_ref[...] * lk1_ref[...], axis=-1, keepdims=True))
               - jnp.exp(jnp.sum(lq2_ref[...] * lk2_ref[...], axis=-1, keepdims=True)) + lam_init)
        o = acc_ref[...] * (1.0 / jnp.sum(l_ref[...], axis=-1, keepdims=True))
        o = o[:tq] - lam * o[tq:]
        o = _group_rms(o) * (1.0 - lam_init) * gain_ref[0]
        o_ref[0] = o.astype(o_ref.dtype)

    for qi in range(n_q):
        pl.when(pl.program_id(2) == qi)(functools.partial(query_block, qi))


def _diff_attention(z3, lq1, lk1, lq2, lk2, gain_b, lam_init):
    b, s, _ = z3.shape
    tq = ATTN_TQ
    assert s % tq == 0
    n_buf = s // tq
    qb, kb, vb = Q_OFF // DIFF_V_DIM, K_OFF // DIFF_V_DIM, V_OFF // DIFF_V_DIM
    vec = pl.BlockSpec((1, HEAD_DIM), lambda bi, h, i: (0, 0))
    return pl.pallas_call(
        functools.partial(_attn_kernel, tq=tq, n_q=s // tq, n_buf=n_buf, lam_init=lam_init),
        grid=(b, N_GROUPS, s // tq),
        in_specs=[
            vec, vec, vec, vec,
            pl.BlockSpec((1, tq, DIFF_V_DIM), lambda bi, h, i: (bi, i, qb + h)),
            pl.BlockSpec((1, s, DIFF_V_DIM), lambda bi, h, i: (bi, 0, kb + h)),
            pl.BlockSpec((1, s, DIFF_V_DIM), lambda bi, h, i: (bi, 0, vb + h)),
            pl.BlockSpec((1, 1, DIFF_V_DIM), lambda bi, h, i: (h, 0, 0)),
        ],
        out_specs=pl.BlockSpec((1, tq, DIFF_V_DIM), lambda bi, h, i: (bi, i, h)),
        out_shape=jax.ShapeDtypeStruct((b, s, DIFF_WIDTH), BF16),
        scratch_shapes=[
            pltpu.VMEM((2 * tq, HEAD_DIM), F32),
            pltpu.VMEM((2 * tq, HEAD_DIM), F32),
            pltpu.VMEM((2 * tq, DIFF_V_DIM), F32),
        ] + [pltpu.VMEM((2 * tq, tq), BF16)] * n_buf
          + [pltpu.VMEM((2 * tq, HEAD_DIM), F32)] * n_buf,
        compiler_params=_params("parallel", "parallel", "arbitrary"),
        name="diff_attention",
    )(lq1, lk1, lq2, lk2, z3, z3, z3, gain_b)


def _ffn_kernel(x_ref, g_ref, wg_ref, wu_ref, wd_ref, fg_ref, o_ref, h_ref, *, final_norm):
    f = pl.program_id(1)
    last_f = pl.num_programs(1) - 1

    def step(first, last):
        if first:
            h_ref[...] = _rms_rows(x_ref[...], g_ref[...]).astype(BF16)
        h = h_ref[...]
        gate = jnp.dot(h, wg_ref[...], preferred_element_type=F32)
        up = jnp.dot(h, wu_ref[...], preferred_element_type=F32)
        act = (gate * jax.nn.sigmoid(gate) * up).astype(BF16)
        out = (x_ref[...] if first else o_ref[...]) + jnp.dot(act, wd_ref[...], preferred_element_type=F32)
        if last and final_norm:
            out = _rms_rows(out, fg_ref[...])
        o_ref[...] = out

    pl.when(f == 0)(functools.partial(step, True, False))
    if final_norm:
        pl.when((f > 0) & (f < last_f))(functools.partial(step, False, False))
        pl.when(f == last_f)(functools.partial(step, False, True))
    else:
        pl.when(f > 0)(functools.partial(step, False, False))


def _ffn(x2, gain, wg, wu, wd, layer, final_gain, final_norm):
    m, d = x2.shape
    dff = wg.shape[2]
    tm, tf = FFN_TM, FFN_TF
    assert m % tm == 0 and dff % tf == 0
    return pl.pallas_call(
        functools.partial(_ffn_kernel, final_norm=final_norm),
        grid=(m // tm, dff // tf),
        in_specs=[
            pl.BlockSpec((tm, d), lambda i, f: (i, 0)),
            pl.BlockSpec((1, d), lambda i, f: (0, 0)),
            pl.BlockSpec((None, d, tf), lambda i, f: (layer, 0, f)),
            pl.BlockSpec((None, d, tf), lambda i, f: (layer, 0, f)),
            pl.BlockSpec((None, tf, d), lambda i, f: (layer, f, 0)),
            pl.BlockSpec((1, d), lambda i, f: (0, 0)),
        ],
        out_specs=pl.BlockSpec((tm, d), lambda i, f: (i, 0)),
        out_shape=jax.ShapeDtypeStruct((m, d), F32),
        scratch_shapes=[pltpu.VMEM((tm, d), BF16)],
        compiler_params=_params("parallel", "arbitrary"),
        name="ffn",
    )(x2, gain, wg, wu, wd, final_gain)


def kernel(x, positions, attn_norm, w_in, gmlp_ln_g, gmlp_ln_b, gmlp_ws, gmlp_bs, lambda_q1, lambda_k1,
           lambda_q2, lambda_k2, conv_w, mix_norm, w_out, ffn_norm, w_gate, w_up, w_down, final_norm):
    b, s, d = x.shape
    m = b * s
    depth = w_in.shape[0]

    inv_freq = 1.0 / (ROPE_THETA ** (jnp.arange(0, HEAD_DIM, 2, dtype=F32) / HEAD_DIM))
    ang = positions.astype(F32).reshape(m, 1) * jnp.concatenate([inv_freq, inv_freq]).reshape(1, HEAD_DIM)
    sign = jnp.concatenate([-jnp.ones((HEAD_DIM // 2,), F32), jnp.ones((HEAD_DIM // 2,), F32)]).reshape(1, HEAD_DIM)

    w_in, w_out, w_gate, w_up, w_down = (w.astype(BF16) for w in (w_in, w_out, w_gate, w_up, w_down))
    x2 = x.reshape(m, d)
    for l in range(depth):
        lam_init = 0.8 - 0.6 * math.exp(-0.3 * l)
        gain = mix_norm[l].astype(F32)
        z = _in_proj(x2, attn_norm[l].reshape(1, d), w_in, l, ang, sign)
        z3 = z.reshape(b, s, z.shape[1])
        mix_b = _diff_attention(
            z3, lambda_q1[l].reshape(1, HEAD_DIM), lambda_k1[l].reshape(1, HEAD_DIM),
            lambda_q2[l].reshape(1, HEAD_DIM), lambda_k2[l].reshape(1, HEAD_DIM),
            gain[GMLP_WIDTH:GMLP_WIDTH + DIFF_WIDTH].reshape(N_GROUPS, 1, DIFF_V_DIM), lam_init)
        x2 = _mix_out_proj(
            x2, z, mix_b.reshape(m, DIFF_WIDTH), gmlp_ln_g[l], gmlp_ln_b[l], gmlp_ws[l],
            jnp.tile(gmlp_bs[l], (1, OUT_PROJ_TM // GMLP_CHUNK)).reshape(N_GROUPS, OUT_PROJ_TM, 1), conv_w[l],
            gain[:GMLP_WIDTH].reshape(1, GMLP_WIDTH), gain[GMLP_WIDTH + DIFF_WIDTH:].reshape(1, CONV_WIDTH),
            w_out, l, s)
        x2 = _ffn(x2, ffn_norm[l].reshape(1, d), w_gate, w_up, w_down, l, final_norm.reshape(1, d),
                  final_norm=(l == depth - 1))
    return x2.reshape(b, s, d)
```

```python
import functools
import math

import jax
import jax.numpy as jnp
from jax import lax
from jax.experimental import pallas as pl
from jax.experimental.pallas import tpu as pltpu

F32 = jnp.float32
BF16 = jnp.bfloat16

HEAD_DIM = 128
N_GROUPS = 4
GMLP_WIDTH = N_GROUPS * HEAD_DIM
GMLP_CHUNK = 128
DIFF_QK_WIDTH = N_GROUPS * 2 * HEAD_DIM
DIFF_V_DIM = 2 * HEAD_DIM
DIFF_WIDTH = N_GROUPS * DIFF_V_DIM
CONV_WIDTH = N_GROUPS * HEAD_DIM
CONV_K = 3
ROPE_THETA = 10000.0
RMS_EPS = 1e-6
LN_EPS = 1e-5
LOG2_E = math.log2(math.e)

Q_OFF = 2 * GMLP_WIDTH
K_OFF = Q_OFF + DIFF_QK_WIDTH
V_OFF = K_OFF + DIFF_QK_WIDTH
C_OFF = V_OFF + DIFF_WIDTH

VMEM_LIMIT_BYTES = 56 * 1024 * 1024

IN_PROJ_TM = 512
IN_PROJ_TN = 2816
ATTN_TQ = 512
CAST_SLAB_ROWS = 16
ATTN_ROW_CHUNK = 32
OUT_PROJ_TM = 512
FFN_TM = 1024
FFN_TF = 512
CONV_HALO_ROWS = 8


def _params(*semantics):
    return pltpu.CompilerParams(dimension_semantics=semantics, vmem_limit_bytes=VMEM_LIMIT_BYTES)


def _rms_rows(x, gain):
    return x * lax.rsqrt(jnp.mean(x * x, axis=-1, keepdims=True) + RMS_EPS) * gain


def _group_rms(x):
    return x * lax.rsqrt(jnp.mean(x * x, axis=-1, keepdims=True) + RMS_EPS)


def _gelu(x):
    return 0.5 * x * (1.0 + lax.erf(x * math.sqrt(0.5)))


def _in_proj_kernel(x_ref, g_ref, w_ref, ang_ref, sign_ref, z_ref, h_ref, cos_ref, sin_ref, *, tn, n_col_tiles):
    j = pl.program_id(1)

    def tile(col0):
        first = col0 == (n_col_tiles - 1) * tn
        if first:
            h_ref[...] = _rms_rows(x_ref[...], g_ref[...]).astype(BF16)
        acc = jnp.dot(h_ref[...], w_ref[...], preferred_element_type=F32)
        if first:
            cos_ref[...] = jnp.cos(ang_ref[...])
            sin_ref[...] = jnp.sin(ang_ref[...]) * sign_ref[...]
        cosf, sinf = cos_ref[...], sin_ref[...]
        for c in range(tn // HEAD_DIM):
            col = col0 + c * HEAD_DIM
            r = acc[:, c * HEAD_DIM:(c + 1) * HEAD_DIM]
            if col < Q_OFF:
                r = _gelu(r)
            elif col < V_OFF:
                r = r * cosf + pltpu.roll(r, HEAD_DIM // 2, 1) * sinf
                if col < K_OFF:
                    r = r * (LOG2_E / math.sqrt(HEAD_DIM))
            z_ref[:, c * HEAD_DIM:(c + 1) * HEAD_DIM] = r.astype(z_ref.dtype)

    for jt in range(n_col_tiles):
        pl.when(j == n_col_tiles - 1 - jt)(functools.partial(tile, jt * tn))


def _in_proj(x2, gain, w, layer, ang, sign):
    m, d = x2.shape
    n = w.shape[2]
    tm, tn = IN_PROJ_TM, IN_PROJ_TN
    assert m % tm == 0 and n % tn == 0 and tn % HEAD_DIM == 0
    return pl.pallas_call(
        functools.partial(_in_proj_kernel, tn=tn, n_col_tiles=n // tn),
        grid=(m // tm, n // tn),
        in_specs=[
            pl.BlockSpec((tm, d), lambda i, j: (i, 0)),
            pl.BlockSpec((1, d), lambda i, j: (0, 0)),
            pl.BlockSpec((None, d, tn), lambda i, j: (layer, 0, n // tn - 1 - j)),
            pl.BlockSpec((tm, HEAD_DIM), lambda i, j: (i, 0)),
            pl.BlockSpec((1, HEAD_DIM), lambda i, j: (0, 0)),
        ],
        out_specs=pl.BlockSpec((tm, tn), lambda i, j: (i, n // tn - 1 - j)),
        out_shape=jax.ShapeDtypeStruct((m, n), BF16),
        scratch_shapes=[pltpu.VMEM((tm, d), BF16),
                        pltpu.VMEM((tm, HEAD_DIM), F32),
                        pltpu.VMEM((tm, HEAD_DIM), F32)],
        compiler_params=_params("parallel", "arbitrary"),
        name="in_proj",
    )(x2, gain, w, ang, sign)


def _gmlp_mixer(za_ref, lng_ref, lnb_ref, ws_ref, bs_ref, ga_ref, vn_ref, mixed_ref, out_ref, rows):
    groups = [slice(g * HEAD_DIM, (g + 1) * HEAD_DIM) for g in range(N_GROUPS)]
    for g, cols in enumerate(groups):
        v = za_ref[:, GMLP_WIDTH + cols.start:GMLP_WIDTH + cols.stop].astype(F32)
        mu = jnp.mean(v, axis=-1, keepdims=True)
        vc = v - mu
        var = jnp.mean(vc * vc, axis=-1, keepdims=True)
        vn = vc * lax.rsqrt(var + LN_EPS) * lng_ref[g:g + 1, :] + lnb_ref[g:g + 1, :]
        vn_ref[:, cols] = vn.astype(BF16)
    t_idx = lax.broadcasted_iota(jnp.int32, (GMLP_CHUNK, GMLP_CHUNK), 0)
    s_idx = lax.broadcasted_iota(jnp.int32, (GMLP_CHUNK, GMLP_CHUNK), 1)
    for g, cols in enumerate(groups):
        w = jnp.where(s_idx <= t_idx, ws_ref[g], 0.0).astype(BF16)
        for c in range(rows // GMLP_CHUNK):
            chunk = slice(c * GMLP_CHUNK, (c + 1) * GMLP_CHUNK)
            mixed_ref[chunk, cols] = jnp.dot(w, vn_ref[chunk, cols], preferred_element_type=F32)
    for g, cols in enumerate(groups):
        u = za_ref[:, cols].astype(F32)
        ya = _group_rms(u * (mixed_ref[:, cols] + bs_ref[g])) * ga_ref[:, cols]
        out_ref[:, cols] = ya.astype(out_ref.dtype)


def _short_conv_mixer(bg_ref, cg_ref, hc_ref, cgh_ref, hch_ref, cw_ref, gc_ref, out_ref, at_seq_start):
    xh = cg_ref[...].astype(F32) * hc_ref[...].astype(F32)
    halo = cgh_ref[...].astype(F32) * hch_ref[...].astype(F32)
    halo = jnp.where(at_seq_start, 0.0, halo)
    row = lax.broadcasted_iota(jnp.int32, xh.shape, 0)
    prev1 = jnp.where(row == 0, halo[CONV_HALO_ROWS - 1:CONV_HALO_ROWS, :], pltpu.roll(xh, 1, 0))
    prev2 = jnp.where(row == 0, halo[CONV_HALO_ROWS - 2:CONV_HALO_ROWS - 1, :],
                      jnp.where(row == 1, halo[CONV_HALO_ROWS - 1:CONV_HALO_ROWS, :], pltpu.roll(xh, 2, 0)))
    y = cw_ref[0:1, :] * prev2 + cw_ref[1:2, :] * prev1 + cw_ref[2:3, :] * xh
    yc = bg_ref[...].astype(F32) * y
    for g in range(N_GROUPS):
        cols = slice(g * HEAD_DIM, (g + 1) * HEAD_DIM)
        out_ref[:, cols] = (_group_rms(yc[:, cols]) * gc_ref[:, cols]).astype(out_ref.dtype)


def _mix_out_kernel(x_ref, za_ref, bg_ref, cg_ref, hc_ref, cgh_ref, hch_ref, mb_ref, lng_ref, lnb_ref, ws_ref,
                    bs_ref, cw_ref, ga_ref, gc_ref, w0_ref, w1_ref, w2_ref, w3_ref, o_ref, vn_ref, mixed_ref,
                    ma_ref, mc_ref, *, rows, blocks_per_seq):
    acc = jnp.dot(mb_ref[:, :GMLP_WIDTH], w1_ref[...], preferred_element_type=F32)
    acc = acc + jnp.dot(mb_ref[:, GMLP_WIDTH:], w2_ref[...], preferred_element_type=F32)
    _gmlp_mixer(za_ref, lng_ref, lnb_ref, ws_ref, bs_ref, ga_ref, vn_ref, mixed_ref, ma_ref, rows)
    _short_conv_mixer(bg_ref, cg_ref, hc_ref, cgh_ref, hch_ref, cw_ref, gc_ref, mc_ref,
                      pl.program_id(0) % blocks_per_seq == 0)
    acc = acc + jnp.dot(ma_ref[...], w0_ref[...], preferred_element_type=F32)
    acc = acc + jnp.dot(mc_ref[...], w3_ref[...], preferred_element_type=F32)
    o_ref[...] = x_ref[...] + acc


def _mix_out_proj(x2, z, mix_b, ln_g, ln_b, ws, bs, conv_w, gain_a, gain_c, w, layer, seq_len):
    m, d = x2.shape
    rows = OUT_PROJ_TM
    assert m % rows == 0 and seq_len % rows == 0 and rows % GMLP_CHUNK == 0 and C_OFF % CONV_WIDTH == 0
    assert DIFF_WIDTH == 2 * GMLP_WIDTH and GMLP_WIDTH == CONV_WIDTH
    cb = C_OFF // CONV_WIDTH
    hb = rows // CONV_HALO_ROWS
    full = lambda shape: pl.BlockSpec(shape, lambda i: (0,) * len(shape))
    zcol = lambda width, col: pl.BlockSpec((rows, width), lambda i: (i, col))
    halo = lambda col: pl.BlockSpec((CONV_HALO_ROWS, CONV_WIDTH), lambda i: (jnp.maximum(i * hb - 1, 0), col))
    wrows = lambda blk: pl.BlockSpec((None, GMLP_WIDTH, d), lambda i: (layer, blk, 0))
    return pl.pallas_call(
        functools.partial(_mix_out_kernel, rows=rows, blocks_per_seq=seq_len // rows),
        grid=(m // rows,),
        in_specs=[
            pl.BlockSpec((rows, d), lambda i: (i, 0)),
            zcol(2 * GMLP_WIDTH, 0), zcol(CONV_WIDTH, cb), zcol(CONV_WIDTH, cb + 1), zcol(CONV_WIDTH, cb + 2),
            halo(cb + 1), halo(cb + 2),
            pl.BlockSpec((rows, DIFF_WIDTH), lambda i: (i, 0)),
            full((N_GROUPS, HEAD_DIM)), full((N_GROUPS, HEAD_DIM)),
            full((N_GROUPS, GMLP_CHUNK, GMLP_CHUNK)), full((N_GROUPS, rows, 1)),
            full((CONV_K, CONV_WIDTH)), full((1, GMLP_WIDTH)), full((1, CONV_WIDTH)),
            wrows(0), wrows(1), wrows(2), wrows(3),
        ],
        out_specs=pl.BlockSpec((rows, d), lambda i: (i, 0)),
        out_shape=jax.ShapeDtypeStruct((m, d), F32),
        scratch_shapes=[pltpu.VMEM((rows, GMLP_WIDTH), BF16),
                        pltpu.VMEM((rows, GMLP_WIDTH), F32),
                        pltpu.VMEM((rows, GMLP_WIDTH), BF16),
                        pltpu.VMEM((rows, CONV_WIDTH), BF16)],
        compiler_params=_params("parallel"),
        name="mix_out_proj",
    )(x2, z, z, z, z, z, z, mix_b, ln_g, ln_b, ws, bs, conv_w, gain_a, gain_c, w, w, w, w)


def _attn_kernel(lq1_ref, lk1_ref, lq2_ref, lk2_ref, q_ref, k_ref, v_ref, gain_ref, *rest, tq, n_q, n_buf, n_cast,
                 lam_init):
    cast_src, (o_ref, *cast_dst) = rest[:n_cast], rest[n_cast:2 * n_cast + 1]
    m_ref, l_ref, acc_ref, *bufs = rest[2 * n_cast + 1:]
    for src_ref, dst_ref in zip(cast_src, cast_dst):
        dst_ref[...] = src_ref[...].astype(dst_ref.dtype)

    rc = ATTN_ROW_CHUNK
    n_tiles = tq // HEAD_DIM
    p_bufs, a_bufs = bufs[:n_buf], bufs[n_buf:]

    def key_rows(j):
        return slice(j * tq, (j + 1) * tq)

    def scores(j):
        return [lax.dot_general(q_ref[0, :, mp * HEAD_DIM:(mp + 1) * HEAD_DIM],
                                k_ref[0, key_rows(j), mp * HEAD_DIM:(mp + 1) * HEAD_DIM],
                                (((1,), (1,)), ((), ())), preferred_element_type=F32) for mp in range(2)]

    def softmax(s_maps, buf, diagonal, first):
        p_ref, alpha_ref = p_bufs[buf], a_bufs[buf]
        for c in range(2 * tq // rc):
            rows = slice(c * rc, (c + 1) * rc)
            q0 = rows.start % tq
            s = s_maps[rows.start // tq]
            s_rows = slice(q0, q0 + rc)
            live = [t for t in range(n_tiles) if (not diagonal) or t * HEAD_DIM <= q0 + rc - 1]
            tiles = []
            for t in live:
                st = s[s_rows, t * HEAD_DIM:(t + 1) * HEAD_DIM]
                if diagonal and (t + 1) * HEAD_DIM - 1 > q0:
                    row = lax.broadcasted_iota(jnp.int32, (rc, HEAD_DIM), 0) + q0
                    col = lax.broadcasted_iota(jnp.int32, (rc, HEAD_DIM), 1) + t * HEAD_DIM
                    st = jnp.where(col <= row, st, -jnp.inf)
                tiles.append(st)
            m_blk = jnp.max(functools.reduce(jnp.maximum, tiles), axis=-1, keepdims=True)
            if first:
                m_new = jnp.broadcast_to(m_blk, (rc, HEAD_DIM))
                ps = [jnp.exp2(st - m_new) for st in tiles]
                l_ref[rows, :] = functools.reduce(jnp.add, ps)
            else:
                m_old = m_ref[rows, :]
                m_new = jnp.maximum(m_old, m_blk)
                alpha = jnp.exp2(m_old - m_new)
                ps = [jnp.exp2(st - m_new) for st in tiles]
                l_ref[rows, :] = alpha * l_ref[rows, :] + functools.reduce(jnp.add, ps)
                alpha_ref[rows, :] = alpha
            m_ref[rows, :] = m_new
            for t in range(n_tiles):
                p_t = ps[live.index(t)].astype(BF16) if t in live else jnp.zeros((rc, HEAD_DIM), BF16)
                p_ref[rows, t * HEAD_DIM:(t + 1) * HEAD_DIM] = p_t

    def accumulate(j, buf, first):
        pv = jnp.dot(p_bufs[buf][...], v_ref[0, key_rows(j), :], preferred_element_type=F32)
        if first:
            acc_ref[...] = pv
        else:
            alpha = a_bufs[buf][...]
            for t in range(DIFF_V_DIM // HEAD_DIM):
                cols = slice(t * HEAD_DIM, (t + 1) * HEAD_DIM)
                acc_ref[:, cols] = acc_ref[:, cols] * alpha + pv[:, cols]

    def query_block(qi):
        s_next = scores(0)
        for j in range(qi + 1):
            s = s_next
            if j < qi:
                s_next = scores(j + 1)
            if j >= 1:
                accumulate(j - 1, (j - 1) % n_buf, first=(j == 1))
            softmax(s, j % n_buf, diagonal=(j == qi), first=(j == 0))
        accumulate(qi, qi % n_buf, first=(qi == 0))

        lam = (jnp.exp(jnp.sum(lq1_ref[...] * lk1_ref[...], axis=-1, keepdims=True))
               - jnp.exp(jnp.sum(lq2_ref[...] * lk2_ref[...], axis=-1, keepdims=True)) + lam_init)
        o = acc_ref[...] * (1.0 / jnp.sum(l_ref[...], axis=-1, keepdims=True))
        o = o[:tq] - lam * o[tq:]
        o = _group_rms(o) * (1.0 - lam_init) * gain_ref[0]
        o_ref[0] = o.astype(o_ref.dtype)

    for qi in range(n_q):
        pl.when(pl.program_id(2) == qi)(functools.partial(query_block, qi))


def _diff_attention(z3, lq1, lk1, lq2, lk2, gain_b, lam_init, cast_weights):
    b, s, _ = z3.shape
    tq = ATTN_TQ
    assert s % tq == 0
    n_steps = b * N_GROUPS * (s // tq)
    cast_in_specs, cast_out_specs, cast_out_shapes = [], [], []
    for w, layer in cast_weights:
        _, r, c = w.shape
        slab = r // n_steps if r % n_steps == 0 and (r // n_steps) % CAST_SLAB_ROWS == 0 else CAST_SLAB_ROWS
        assert r % slab == 0 and n_steps % (r // slab) == 0
        rep = n_steps // (r // slab)
        step = lambda bi, h, i, rep=rep: ((bi * N_GROUPS + h) * (s // tq) + i) // rep
        cast_in_specs.append(pl.BlockSpec((None, slab, c), lambda bi, h, i, layer=layer, step=step: (layer, step(bi, h, i), 0)))
        cast_out_specs.append(pl.BlockSpec((slab, c), lambda bi, h, i, step=step: (step(bi, h, i), 0)))
        cast_out_shapes.append(jax.ShapeDtypeStruct((r, c), BF16))
    n_buf = s // tq
    qb, kb, vb = Q_OFF // DIFF_V_DIM, K_OFF // DIFF_V_DIM, V_OFF // DIFF_V_DIM
    vec = pl.BlockSpec((1, HEAD_DIM), lambda bi, h, i: (0, 0))
    outs = pl.pallas_call(
        functools.partial(_attn_kernel, tq=tq, n_q=s // tq, n_buf=n_buf, n_cast=len(cast_weights),
                          lam_init=lam_init),
        grid=(b, N_GROUPS, s // tq),
        in_specs=[
            vec, vec, vec, vec,
            pl.BlockSpec((1, tq, DIFF_V_DIM), lambda bi, h, i: (bi, i, qb + h)),
            pl.BlockSpec((1, s, DIFF_V_DIM), lambda bi, h, i: (bi, 0, kb + h)),
            pl.BlockSpec((1, s, DIFF_V_DIM), lambda bi, h, i: (bi, 0, vb + h)),
            pl.BlockSpec((1, 1, DIFF_V_DIM), lambda bi, h, i: (h, 0, 0)),
        ] + cast_in_specs,
        out_specs=[pl.BlockSpec((1, tq, DIFF_V_DIM), lambda bi, h, i: (bi, i, h))] + cast_out_specs,
        out_shape=[jax.ShapeDtypeStruct((b, s, DIFF_WIDTH), BF16)] + cast_out_shapes,
        scratch_shapes=[
            pltpu.VMEM((2 * tq, HEAD_DIM), F32),
            pltpu.VMEM((2 * tq, HEAD_DIM), F32),
            pltpu.VMEM((2 * tq, DIFF_V_DIM), F32),
        ] + [pltpu.VMEM((2 * tq, tq), BF16)] * n_buf
          + [pltpu.VMEM((2 * tq, HEAD_DIM), F32)] * n_buf,
        compiler_params=_params("arbitrary", "arbitrary", "arbitrary"),
        name="diff_attention",
    )(lq1, lk1, lq2, lk2, z3, z3, z3, gain_b, *[w for w, _ in cast_weights])
    return outs[0], outs[1:]


def _ffn_kernel(x_ref, g_ref, wg_ref, wu_ref, wd_ref, fg_ref, o_ref, h_ref, *, final_norm):
    f = pl.program_id(1)
    last_f = pl.num_programs(1) - 1

    def step(first, last):
        if first:
            h_ref[...] = _rms_rows(x_ref[...], g_ref[...]).astype(BF16)
        h = h_ref[...]
        gate = jnp.dot(h, wg_ref[...], preferred_element_type=F32)
        up = jnp.dot(h, wu_ref[...], preferred_element_type=F32)
        act = (gate * jax.nn.sigmoid(gate) * up).astype(BF16)
        out = (x_ref[...] if first else o_ref[...]) + jnp.dot(act, wd_ref[...], preferred_element_type=F32)
        if last and final_norm:
            out = _rms_rows(out, fg_ref[...])
        o_ref[...] = out

    pl.when(f == 0)(functools.partial(step, True, False))
    if final_norm:
        pl.when((f > 0) & (f < last_f))(functools.partial(step, False, False))
        pl.when(f == last_f)(functools.partial(step, False, True))
    else:
        pl.when(f > 0)(functools.partial(step, False, False))


def _ffn(x2, gain, wg, wu, wd, layer, final_gain, final_norm):
    m, d = x2.shape
    dff = wg.shape[2]
    tm, tf = FFN_TM, FFN_TF
    assert m % tm == 0 and dff % tf == 0
    return pl.pallas_call(
        functools.partial(_ffn_kernel, final_norm=final_norm),
        grid=(m // tm, dff // tf),
        in_specs=[
            pl.BlockSpec((tm, d), lambda i, f: (i, 0)),
            pl.BlockSpec((1, d), lambda i, f: (0, 0)),
            pl.BlockSpec((None, d, tf), lambda i, f: (layer, 0, f)),
            pl.BlockSpec((None, d, tf), lambda i, f: (layer, 0, f)),
            pl.BlockSpec((None, tf, d), lambda i, f: (layer, f, 0)),
            pl.BlockSpec((1, d), lambda i, f: (0, 0)),
        ],
        out_specs=pl.BlockSpec((tm, d), lambda i, f: (i, 0)),
        out_shape=jax.ShapeDtypeStruct((m, d), F32),
        scratch_shapes=[pltpu.VMEM((tm, d), BF16)],
        compiler_params=_params("parallel", "arbitrary"),
        name="ffn",
    )(x2, gain, wg, wu, wd, final_gain)


def kernel(x, positions, attn_norm, w_in, gmlp_ln_g, gmlp_ln_b, gmlp_ws, gmlp_bs, lambda_q1, lambda_k1,
           lambda_q2, lambda_k2, conv_w, mix_norm, w_out, ffn_norm, w_gate, w_up, w_down, final_norm):
    b, s, d = x.shape
    m = b * s
    depth = w_in.shape[0]

    inv_freq = 1.0 / (ROPE_THETA ** (jnp.arange(0, HEAD_DIM, 2, dtype=F32) / HEAD_DIM))
    ang = positions.astype(F32).reshape(m, 1) * jnp.concatenate([inv_freq, inv_freq]).reshape(1, HEAD_DIM)
    sign = jnp.concatenate([-jnp.ones((HEAD_DIM // 2,), F32), jnp.ones((HEAD_DIM // 2,), F32)]).reshape(1, HEAD_DIM)

    dff = w_gate.shape[2]
    w_in_l, w_out_l = w_in[0].astype(BF16), w_out[0].astype(BF16)
    w_down_v = w_down.reshape(depth, d, dff)
    x2 = x.reshape(m, d)
    for l in range(depth):
        lam_init = 0.8 - 0.6 * math.exp(-0.3 * l)
        gain = mix_norm[l].astype(F32)
        z = _in_proj(x2, attn_norm[l].reshape(1, d), w_in_l[None], 0, ang, sign)
        z3 = z.reshape(b, s, z.shape[1])
        mix_b, casts = _diff_attention(
            z3, lambda_q1[l].reshape(1, HEAD_DIM), lambda_k1[l].reshape(1, HEAD_DIM),
            lambda_q2[l].reshape(1, HEAD_DIM), lambda_k2[l].reshape(1, HEAD_DIM),
            gain[GMLP_WIDTH:GMLP_WIDTH + DIFF_WIDTH].reshape(N_GROUPS, 1, DIFF_V_DIM), lam_init,
            [(w_gate, l), (w_up, l), (w_down_v, l)] + ([(w_in, l + 1), (w_out, l + 1)] if l + 1 < depth else []))
        w_gate_l, w_up_l, w_down_l = casts[0], casts[1], casts[2].reshape(dff, d)
        x2 = _mix_out_proj(
            x2, z, mix_b.reshape(m, DIFF_WIDTH), gmlp_ln_g[l], gmlp_ln_b[l], gmlp_ws[l],
            jnp.tile(gmlp_bs[l], (1, OUT_PROJ_TM // GMLP_CHUNK)).reshape(N_GROUPS, OUT_PROJ_TM, 1), conv_w[l],
            gain[:GMLP_WIDTH].reshape(1, GMLP_WIDTH), gain[GMLP_WIDTH + DIFF_WIDTH:].reshape(1, CONV_WIDTH),
            w_out_l[None], 0, s)
        x2 = _ffn(x2, ffn_norm[l].reshape(1, d), w_gate_l[None], w_up_l[None], w_down_l[None], 0,
                  final_norm.reshape(1, d), final_norm=(l == depth - 1))
        if l + 1 < depth:
            w_in_l, w_out_l = casts[3], casts[4]
    return x2.reshape(b, s, d)
```

```python
import functools
import math

import jax
import jax.numpy as jnp
from jax import lax
from jax.experimental import pallas as pl
from jax.experimental.pallas import tpu as pltpu

F32 = jnp.float32
BF16 = jnp.bfloat16

HEAD_DIM = 128
N_GROUPS = 4
GMLP_WIDTH = N_GROUPS * HEAD_DIM
GMLP_CHUNK = 128
DIFF_QK_WIDTH = N_GROUPS * 2 * HEAD_DIM
DIFF_V_DIM = 2 * HEAD_DIM
DIFF_WIDTH = N_GROUPS * DIFF_V_DIM
CONV_WIDTH = N_GROUPS * HEAD_DIM
CONV_K = 3
ROPE_THETA = 10000.0
RMS_EPS = 1e-6
LN_EPS = 1e-5
LOG2_E = math.log2(math.e)

Q_OFF = 2 * GMLP_WIDTH
K_OFF = Q_OFF + DIFF_QK_WIDTH
V_OFF = K_OFF + DIFF_QK_WIDTH
C_OFF = V_OFF + DIFF_WIDTH

VMEM_LIMIT_BYTES = 56 * 1024 * 1024

IN_PROJ_TM = 512
IN_PROJ_TN = 2816
ATTN_TQ = 512
CAST_SLAB_ROWS = 16
ATTN_ROW_CHUNK = 32
OUT_PROJ_TM = 512
FFN_TM = 1024
FFN_TF = 512
CONV_HALO_ROWS = 8


def _params(*semantics):
    return pltpu.CompilerParams(dimension_semantics=semantics, vmem_limit_bytes=VMEM_LIMIT_BYTES)


def _rms_rows(x, gain):
    return x * lax.rsqrt(jnp.mean(x * x, axis=-1, keepdims=True) + RMS_EPS) * gain


def _group_rms(x):
    return x * lax.rsqrt(jnp.mean(x * x, axis=-1, keepdims=True) + RMS_EPS)


def _gelu(x):
    return 0.5 * x * (1.0 + lax.erf(x * math.sqrt(0.5)))


def _in_proj_kernel(x_ref, g_ref, w_ref, ang_ref, sign_ref, z_ref, h_ref, cos_ref, sin_ref, *, tn, n_col_tiles):
    j = pl.program_id(1)

    def tile(col0):
        first = col0 == (n_col_tiles - 1) * tn
        if first:
            h_ref[...] = _rms_rows(x_ref[...], g_ref[...]).astype(BF16)
        acc = jnp.dot(h_ref[...], w_ref[...], preferred_element_type=F32)
        if first:
            cos_ref[...] = jnp.cos(ang_ref[...])
            sin_ref[...] = jnp.sin(ang_ref[...]) * sign_ref[...]
        cosf, sinf = cos_ref[...], sin_ref[...]
        for c in range(tn // HEAD_DIM):
            col = col0 + c * HEAD_DIM
            r = acc[:, c * HEAD_DIM:(c + 1) * HEAD_DIM]
            if col < Q_OFF:
                r = _gelu(r)
            elif col < V_OFF:
                r = r * cosf + pltpu.roll(r, HEAD_DIM // 2, 1) * sinf
                if col < K_OFF:
                    r = r * (LOG2_E / math.sqrt(HEAD_DIM))
            z_ref[:, c * HEAD_DIM:(c + 1) * HEAD_DIM] = r.astype(z_ref.dtype)

    for jt in range(n_col_tiles):
        pl.when(j == n_col_tiles - 1 - jt)(functools.partial(tile, jt * tn))


def _in_proj(x2, gain, w, layer, ang, sign):
    m, d = x2.shape
    n = w.shape[2]
    tm, tn = IN_PROJ_TM, IN_PROJ_TN
    assert m % tm == 0 and n % tn == 0 and tn % HEAD_DIM == 0
    return pl.pallas_call(
        functools.partial(_in_proj_kernel, tn=tn, n_col_tiles=n // tn),
        grid=(m // tm, n // tn),
        in_specs=[
            pl.BlockSpec((tm, d), lambda i, j: (i, 0)),
            pl.BlockSpec((1, d), lambda i, j: (0, 0)),
            pl.BlockSpec((None, d, tn), lambda i, j: (layer, 0, n // tn - 1 - j)),
            pl.BlockSpec((tm, HEAD_DIM), lambda i, j: (i, 0)),
            pl.BlockSpec((1, HEAD_DIM), lambda i, j: (0, 0)),
        ],
        out_specs=pl.BlockSpec((tm, tn), lambda i, j: (i, n // tn - 1 - j)),
        out_shape=jax.ShapeDtypeStruct((m, n), BF16),
        scratch_shapes=[pltpu.VMEM((tm, d), BF16),
                        pltpu.VMEM((tm, HEAD_DIM), F32),
                        pltpu.VMEM((tm, HEAD_DIM), F32)],
        compiler_params=_params("parallel", "arbitrary"),
        name="in_proj",
    )(x2, gain, w, ang, sign)


def _gmlp_mixer(za_ref, lng_ref, lnb_ref, ws_ref, bs_ref, ga_ref, vn_ref, mixed_ref, out_ref, rows):
    groups = [slice(g * HEAD_DIM, (g + 1) * HEAD_DIM) for g in range(N_GROUPS)]
    for g, cols in enumerate(groups):
        v = za_ref[:, GMLP_WIDTH + cols.start:GMLP_WIDTH + cols.stop].astype(F32)
        mu = jnp.mean(v, axis=-1, keepdims=True)
        vc = v - mu
        var = jnp.mean(vc * vc, axis=-1, keepdims=True)
        vn = vc * lax.rsqrt(var + LN_EPS) * lng_ref[g:g + 1, :] + lnb_ref[g:g + 1, :]
        vn_ref[:, cols] = vn.astype(BF16)
    t_idx = lax.broadcasted_iota(jnp.int32, (GMLP_CHUNK, GMLP_CHUNK), 0)
    s_idx = lax.broadcasted_iota(jnp.int32, (GMLP_CHUNK, GMLP_CHUNK), 1)
    for g, cols in enumerate(groups):
        w = jnp.where(s_idx <= t_idx, ws_ref[g], 0.0).astype(BF16)
        for c in range(rows // GMLP_CHUNK):
            chunk = slice(c * GMLP_CHUNK, (c + 1) * GMLP_CHUNK)
            mixed_ref[chunk, cols] = jnp.dot(w, vn_ref[chunk, cols], preferred_element_type=F32)
    for g, cols in enumerate(groups):
        u = za_ref[:, cols].astype(F32)
        ya = _group_rms(u * (mixed_ref[:, cols] + bs_ref[g])) * ga_ref[:, cols]
        out_ref[:, cols] = ya.astype(out_ref.dtype)


def _short_conv_mixer(bg_ref, cg_ref, hc_ref, cgh_ref, hch_ref, cw_ref, gc_ref, out_ref, at_seq_start):
    xh = cg_ref[...].astype(F32) * hc_ref[...].astype(F32)
    halo = cgh_ref[...].astype(F32) * hch_ref[...].astype(F32)
    halo = jnp.where(at_seq_start, 0.0, halo)
    row = lax.broadcasted_iota(jnp.int32, xh.shape, 0)
    prev1 = jnp.where(row == 0, halo[CONV_HALO_ROWS - 1:CONV_HALO_ROWS, :], pltpu.roll(xh, 1, 0))
    prev2 = jnp.where(row == 0, halo[CONV_HALO_ROWS - 2:CONV_HALO_ROWS - 1, :],
                      jnp.where(row == 1, halo[CONV_HALO_ROWS - 1:CONV_HALO_ROWS, :], pltpu.roll(xh, 2, 0)))
    y = cw_ref[0:1, :] * prev2 + cw_ref[1:2, :] * prev1 + cw_ref[2:3, :] * xh
    yc = bg_ref[...].astype(F32) * y
    for g in range(N_GROUPS):
        cols = slice(g * HEAD_DIM, (g + 1) * HEAD_DIM)
        out_ref[:, cols] = (_group_rms(yc[:, cols]) * gc_ref[:, cols]).astype(out_ref.dtype)


def _mix_out_kernel(x_ref, za_ref, bg_ref, cg_ref, hc_ref, cgh_ref, hch_ref, mb_ref, lng_ref, lnb_ref, ws_ref,
                    bs_ref, cw_ref, ga_ref, gc_ref, w0_ref, w1_ref, w2_ref, w3_ref, o_ref, vn_ref, mixed_ref,
                    ma_ref, mc_ref, *, rows, blocks_per_seq):
    acc = jnp.dot(mb_ref[:, :GMLP_WIDTH], w1_ref[...], preferred_element_type=F32)
    acc = acc + jnp.dot(mb_ref[:, GMLP_WIDTH:], w2_ref[...], preferred_element_type=F32)
    _gmlp_mixer(za_ref, lng_ref, lnb_ref, ws_ref, bs_ref, ga_ref, vn_ref, mixed_ref, ma_ref, rows)
    _short_conv_mixer(bg_ref, cg_ref, hc_ref, cgh_ref, hch_ref, cw_ref, gc_ref, mc_ref,
                      pl.program_id(0) % blocks_per_seq == 0)
    acc = acc + jnp.dot(ma_ref[...], w0_ref[...], preferred_element_type=F32)
    acc = acc + jnp.dot(mc_ref[...], w3_ref[...], preferred_element_type=F32)
    o_ref[...] = x_ref[...] + acc


def _mix_out_proj(x2, z, mix_b, ln_g, ln_b, ws, bs, conv_w, gain_a, gain_c, w, layer, seq_len):
    m, d = x2.shape
    rows = OUT_PROJ_TM
    assert m % rows == 0 and seq_len % rows == 0 and rows % GMLP_CHUNK == 0 and C_OFF % CONV_WIDTH == 0
    assert DIFF_WIDTH == 2 * GMLP_WIDTH and GMLP_WIDTH == CONV_WIDTH
    cb = C_OFF // CONV_WIDTH
    hb = rows // CONV_HALO_ROWS
    full = lambda shape: pl.BlockSpec(shape, lambda i: (0,) * len(shape))
    zcol = lambda width, col: pl.BlockSpec((rows, width), lambda i: (i, col))
    halo = lambda col: pl.BlockSpec((CONV_HALO_ROWS, CONV_WIDTH), lambda i: (jnp.maximum(i * hb - 1, 0), col))
    wrows = lambda blk: pl.BlockSpec((None, GMLP_WIDTH, d), lambda i: (layer, blk, 0))
    return pl.pallas_call(
        functools.partial(_mix_out_kernel, rows=rows, blocks_per_seq=seq_len // rows),
        grid=(m // rows,),
        in_specs=[
            pl.BlockSpec((rows, d), lambda i: (i, 0)),
            zcol(2 * GMLP_WIDTH, 0), zcol(CONV_WIDTH, cb), zcol(CONV_WIDTH, cb + 1), zcol(CONV_WIDTH, cb + 2),
            halo(cb + 1), halo(cb + 2),
            pl.BlockSpec((rows, DIFF_WIDTH), lambda i: (i, 0)),
            full((N_GROUPS, HEAD_DIM)), full((N_GROUPS, HEAD_DIM)),
            full((N_GROUPS, GMLP_CHUNK, GMLP_CHUNK)), full((N_GROUPS, rows, 1)),
            full((CONV_K, CONV_WIDTH)), full((1, GMLP_WIDTH)), full((1, CONV_WIDTH)),
            wrows(0), wrows(1), wrows(2), wrows(3),
        ],
        out_specs=pl.BlockSpec((rows, d), lambda i: (i, 0)),
        out_shape=jax.ShapeDtypeStruct((m, d), F32),
        scratch_shapes=[pltpu.VMEM((rows, GMLP_WIDTH), BF16),
                        pltpu.VMEM((rows, GMLP_WIDTH), F32),
                        pltpu.VMEM((rows, GMLP_WIDTH), BF16),
                        pltpu.VMEM((rows, CONV_WIDTH), BF16)],
        compiler_params=_params("parallel"),
        name="mix_out_proj",
    )(x2, z, z, z, z, z, z, mix_b, ln_g, ln_b, ws, bs, conv_w, gain_a, gain_c, w, w, w, w)


def _attn_kernel(lq1_ref, lk1_ref, lq2_ref, lk2_ref, q_ref, k_ref, v_ref, gain_ref, *rest, tq, n_q, n_buf, n_cast,
                 lam_init):
    cast_src, (o_ref, *cast_dst) = rest[:n_cast], rest[n_cast:2 * n_cast + 1]
    m_ref, l_ref, acc_ref, *bufs = rest[2 * n_cast + 1:]
    for src_ref, dst_ref in zip(cast_src, cast_dst):
        dst_ref[...] = src_ref[...].astype(dst_ref.dtype)

    rc = ATTN_ROW_CHUNK
    n_tiles = tq // HEAD_DIM
    p_bufs, a_bufs = bufs[:n_buf], bufs[n_buf:]

    def key_rows(j):
        return slice(j * tq, (j + 1) * tq)

    def scores(j):
        return [lax.dot_general(q_ref[0, :, mp * HEAD_DIM:(mp + 1) * HEAD_DIM],
                                k_ref[0, key_rows(j), mp * HEAD_DIM:(mp + 1) * HEAD_DIM],
                                (((1,), (1,)), ((), ())), preferred_element_type=F32) for mp in range(2)]

    def softmax(s_maps, buf, diagonal, first):
        p_ref, alpha_ref = p_bufs[buf], a_bufs[buf]
        for c in range(2 * tq // rc):
            rows = slice(c * rc, (c + 1) * rc)
            q0 = rows.start % tq
            s = s_maps[rows.start // tq]
            s_rows = slice(q0, q0 + rc)
            live = [t for t in range(n_tiles) if (not diagonal) or t * HEAD_DIM <= q0 + rc - 1]
            tiles = []
            for t in live:
                st = s[s_rows, t * HEAD_DIM:(t + 1) * HEAD_DIM]
                if diagonal and (t + 1) * HEAD_DIM - 1 > q0:
                    row = lax.broadcasted_iota(jnp.int32, (rc, HEAD_DIM), 0) + q0
                    col = lax.broadcasted_iota(jnp.int32, (rc, HEAD_DIM), 1) + t * HEAD_DIM
                    st = jnp.where(col <= row, st, -jnp.inf)
                tiles.append(st)
            m_blk = jnp.max(functools.reduce(jnp.maximum, tiles), axis=-1, keepdims=True)
            if first:
                m_new = jnp.broadcast_to(m_blk, (rc, HEAD_DIM))
                ps = [jnp.exp2(st - m_new) for st in tiles]
                l_ref[rows, :] = functools.reduce(jnp.add, ps)
            else:
                m_old = m_ref[rows, :]
                m_new = jnp.maximum(m_old, m_blk)
                alpha = jnp.exp2(m_old - m_new)
                ps = [jnp.exp2(st - m_new) for st in tiles]
                l_ref[rows, :] = alpha * l_ref[rows, :] + functools.reduce(jnp.add, ps)
                alpha_ref[rows, :] = alpha
            m_ref[rows, :] = m_new
            for t in range(n_tiles):
                p_t = ps[live.index(t)].astype(BF16) if t in live else jnp.zeros((rc, HEAD_DIM), BF16)
                p_ref[rows, t * HEAD_DIM:(t + 1) * HEAD_DIM] = p_t

    def accumulate(j, buf, first):
        pv = jnp.dot(p_bufs[buf][...], v_ref[0, key_rows(j), :], preferred_element_type=F32)
        if first:
            acc_ref[...] = pv
        else:
            alpha = a_bufs[buf][...]
            for t in range(DIFF_V_DIM // HEAD_DIM):
                cols = slice(t * HEAD_DIM, (t + 1) * HEAD_DIM)
                acc_ref[:, cols] = acc_ref[:, cols] * alpha + pv[:, cols]

    def query_block(qi):
        s_next = scores(0)
        for j in range(qi + 1):
            s = s_next
            if j < qi:
                s_next = scores(j + 1)
            if j >= 1:
                accumulate(j - 1, (j - 1) % n_buf, first=(j == 1))
            softmax(s, j % n_buf, diagonal=(j == qi), first=(j == 0))
        accumulate(qi, qi % n_buf, first=(qi == 0))

        lam = (jnp.exp(jnp.sum(lq1_ref[...] * lk1_ref[...], axis=-1, keepdims=True))
               - jnp.exp(jnp.sum(lq2_ref[...] * lk2_ref[...], axis=-1, keepdims=True)) + lam_init)
        o = acc_ref[...] * (1.0 / jnp.sum(l_ref[...], axis=-1, keepdims=True))
        o = o[:tq] - lam * o[tq:]
        o = _group_rms(o) * (1.0 - lam_init) * gain_ref[0]
        o_ref[0] = o.astype(o_ref.dtype)

    for qi in range(n_q):
        pl.when(pl.program_id(2) == qi)(functools.partial(query_block, qi))


def _diff_attention(z3, lq1, lk1, lq2, lk2, gain_b, lam_init, cast_weights):
    b, s, _ = z3.shape
    tq = ATTN_TQ
    assert s % tq == 0
    n_steps = b * N_GROUPS * (s // tq)
    cast_in_specs, cast_out_specs, cast_out_shapes = [], [], []
    for w, layer in cast_weights:
        _, r, c = w.shape
        slab = next(k for k in range(CAST_SLAB_ROWS, r + 1, CAST_SLAB_ROWS) if r % k == 0 and r // k <= n_steps)
        step = lambda bi, h, i, last=r // slab - 1: jnp.minimum((bi * N_GROUPS + h) * (s // tq) + i, last)
        cast_in_specs.append(pl.BlockSpec((None, slab, c), lambda bi, h, i, layer=layer, step=step: (layer, step(bi, h, i), 0)))
        cast_out_specs.append(pl.BlockSpec((slab, c), lambda bi, h, i, step=step: (step(bi, h, i), 0)))
        cast_out_shapes.append(jax.ShapeDtypeStruct((r, c), BF16))
    n_buf = s // tq
    qb, kb, vb = Q_OFF // DIFF_V_DIM, K_OFF // DIFF_V_DIM, V_OFF // DIFF_V_DIM
    vec = pl.BlockSpec((1, HEAD_DIM), lambda bi, h, i: (0, 0))
    outs = pl.pallas_call(
        functools.partial(_attn_kernel, tq=tq, n_q=s // tq, n_buf=n_buf, n_cast=len(cast_weights),
                          lam_init=lam_init),
        grid=(b, N_GROUPS, s // tq),
        in_specs=[
            vec, vec, vec, vec,
            pl.BlockSpec((1, tq, DIFF_V_DIM), lambda bi, h, i: (bi, i, qb + h)),
            pl.BlockSpec((1, s, DIFF_V_DIM), lambda bi, h, i: (bi, 0, kb + h)),
            pl.BlockSpec((1, s, DIFF_V_DIM), lambda bi, h, i: (bi, 0, vb + h)),
            pl.BlockSpec((1, 1, DIFF_V_DIM), lambda bi, h, i: (h, 0, 0)),
        ] + cast_in_specs,
        out_specs=[pl.BlockSpec((1, tq, DIFF_V_DIM), lambda bi, h, i: (bi, i, h))] + cast_out_specs,
        out_shape=[jax.ShapeDtypeStruct((b, s, DIFF_WIDTH), BF16)] + cast_out_shapes,
        scratch_shapes=[
            pltpu.VMEM((2 * tq, HEAD_DIM), F32),
            pltpu.VMEM((2 * tq, HEAD_DIM), F32),
            pltpu.VMEM((2 * tq, DIFF_V_DIM), F32),
        ] + [pltpu.VMEM((2 * tq, tq), BF16)] * n_buf
          + [pltpu.VMEM((2 * tq, HEAD_DIM), F32)] * n_buf,
        compiler_params=_params("arbitrary", "arbitrary", "arbitrary"),
        name="diff_attention",
    )(lq1, lk1, lq2, lk2, z3, z3, z3, gain_b, *[w for w, _ in cast_weights])
    return outs[0], outs[1:]


def _ffn_kernel(x_ref, g_ref, wg_ref, wu_ref, wd_ref, fg_ref, o_ref, h_ref, *, final_norm):
    f = pl.program_id(1)
    last_f = pl.num_programs(1) - 1

    def step(first, last):
        if first:
            h_ref[...] = _rms_rows(x_ref[...], g_ref[...]).astype(BF16)
        h = h_ref[...]
        gate = jnp.dot(h, wg_ref[...], preferred_element_type=F32)
        up = jnp.dot(h, wu_ref[...], preferred_element_type=F32)
        act = (gate * jax.nn.sigmoid(gate) * up).astype(BF16)
        out = (x_ref[...] if first else o_ref[...]) + jnp.dot(act, wd_ref[...], preferred_element_type=F32)
        if last and final_norm:
            out = _rms_rows(out, fg_ref[...])
        o_ref[...] = out

    pl.when(f == 0)(functools.partial(step, True, False))
    if final_norm:
        pl.when((f > 0) & (f < last_f))(functools.partial(step, False, False))
        pl.when(f == last_f)(functools.partial(step, False, True))
    else:
        pl.when(f > 0)(functools.partial(step, False, False))


def _ffn(x2, gain, wg, wu, wd, layer, final_gain, final_norm):
    m, d = x2.shape
    dff = wg.shape[2]
    tm, tf = FFN_TM, FFN_TF
    assert m % tm == 0 and dff % tf == 0
    return pl.pallas_call(
        functools.partial(_ffn_kernel, final_norm=final_norm),
        grid=(m // tm, dff // tf),
        in_specs=[
            pl.BlockSpec((tm, d), lambda i, f: (i, 0)),
            pl.BlockSpec((1, d), lambda i, f: (0, 0)),
            pl.BlockSpec((None, d, tf), lambda i, f: (layer, 0, f)),
            pl.BlockSpec((None, d, tf), lambda i, f: (layer, 0, f)),
            pl.BlockSpec((None, tf, d), lambda i, f: (layer, f, 0)),
            pl.BlockSpec((1, d), lambda i, f: (0, 0)),
        ],
        out_specs=pl.BlockSpec((tm, d), lambda i, f: (i, 0)),
        out_shape=jax.ShapeDtypeStruct((m, d), F32),
        scratch_shapes=[pltpu.VMEM((tm, d), BF16)],
        compiler_params=_params("parallel", "arbitrary"),
        name="ffn",
    )(x2, gain, wg, wu, wd, final_gain)


def kernel(x, positions, attn_norm, w_in, gmlp_ln_g, gmlp_ln_b, gmlp_ws, gmlp_bs, lambda_q1, lambda_k1,
           lambda_q2, lambda_k2, conv_w, mix_norm, w_out, ffn_norm, w_gate, w_up, w_down, final_norm):
    b, s, d = x.shape
    m = b * s
    depth = w_in.shape[0]

    inv_freq = 1.0 / (ROPE_THETA ** (jnp.arange(0, HEAD_DIM, 2, dtype=F32) / HEAD_DIM))
    ang = positions.astype(F32).reshape(m, 1) * jnp.concatenate([inv_freq, inv_freq]).reshape(1, HEAD_DIM)
    sign = jnp.concatenate([-jnp.ones((HEAD_DIM // 2,), F32), jnp.ones((HEAD_DIM // 2,), F32)]).reshape(1, HEAD_DIM)

    w_in_l, w_out_l = w_in[0].astype(BF16), w_out[0].astype(BF16)
    x2 = x.reshape(m, d)
    for l in range(depth):
        lam_init = 0.8 - 0.6 * math.exp(-0.3 * l)
        gain = mix_norm[l].astype(F32)
        z = _in_proj(x2, attn_norm[l].reshape(1, d), w_in_l[None], 0, ang, sign)
        z3 = z.reshape(b, s, z.shape[1])
        mix_b, casts = _diff_attention(
            z3, lambda_q1[l].reshape(1, HEAD_DIM), lambda_k1[l].reshape(1, HEAD_DIM),
            lambda_q2[l].reshape(1, HEAD_DIM), lambda_k2[l].reshape(1, HEAD_DIM),
            gain[GMLP_WIDTH:GMLP_WIDTH + DIFF_WIDTH].reshape(N_GROUPS, 1, DIFF_V_DIM), lam_init,
            [(w_gate, l), (w_up, l), (w_down, l)] + ([(w_in, l + 1), (w_out, l + 1)] if l + 1 < depth else []))
        w_gate_l, w_up_l, w_down_l = casts[:3]
        x2 = _mix_out_proj(
            x2, z, mix_b.reshape(m, DIFF_WIDTH), gmlp_ln_g[l], gmlp_ln_b[l], gmlp_ws[l],
            jnp.tile(gmlp_bs[l], (1, OUT_PROJ_TM // GMLP_CHUNK)).reshape(N_GROUPS, OUT_PROJ_TM, 1), conv_w[l],
            gain[:GMLP_WIDTH].reshape(1, GMLP_WIDTH), gain[GMLP_WIDTH + DIFF_WIDTH:].reshape(1, CONV_WIDTH),
            w_out_l[None], 0, s)
        x2 = _ffn(x2, ffn_norm[l].reshape(1, d), w_gate_l[None], w_up_l[None], w_down_l[None], 0,
                  final_norm.reshape(1, d), final_norm=(l == depth - 1))
        if l + 1 < depth:
            w_in_l, w_out_l = casts[3], casts[4]
    return x2.reshape(b, s, d)
```

```python
import functools
import math

import jax
import jax.numpy as jnp
from jax import lax
from jax.experimental import pallas as pl
from jax.experimental.pallas import tpu as pltpu

F32 = jnp.float32
BF16 = jnp.bfloat16

HEAD_DIM = 128
N_GROUPS = 4
GMLP_WIDTH = N_GROUPS * HEAD_DIM
GMLP_CHUNK = 128
DIFF_QK_WIDTH = N_GROUPS * 2 * HEAD_DIM
DIFF_V_DIM = 2 * HEAD_DIM
DIFF_WIDTH = N_GROUPS * DIFF_V_DIM
CONV_WIDTH = N_GROUPS * HEAD_DIM
CONV_K = 3
ROPE_THETA = 10000.0
RMS_EPS = 1e-6
LN_EPS = 1e-5
LOG2_E = math.log2(math.e)

Q_OFF = 2 * GMLP_WIDTH
K_OFF = Q_OFF + DIFF_QK_WIDTH
V_OFF = K_OFF + DIFF_QK_WIDTH
C_OFF = V_OFF + DIFF_WIDTH

VMEM_LIMIT_BYTES = 56 * 1024 * 1024

IN_PROJ_TM = 512
IN_PROJ_TN = 2816
ATTN_TQ = 512
CAST_SLAB_ROWS = 16
ATTN_ROW_CHUNK = 32
OUT_PROJ_TM = 512
FFN_TM = 1024
FFN_TF = 512
CONV_HALO_ROWS = 8


def _params(*semantics):
    return pltpu.CompilerParams(dimension_semantics=semantics, vmem_limit_bytes=VMEM_LIMIT_BYTES)


def _rms_rows(x, gain):
    return x * lax.rsqrt(jnp.mean(x * x, axis=-1, keepdims=True) + RMS_EPS) * gain


def _group_rms(x):
    return x * lax.rsqrt(jnp.mean(x * x, axis=-1, keepdims=True) + RMS_EPS)


def _gelu(x):
    return 0.5 * x * (1.0 + lax.erf(x * math.sqrt(0.5)))


def _in_proj_kernel(x_ref, g_ref, w_ref, ang_ref, sign_ref, z_ref, kt_ref, h_ref, cos_ref, sin_ref, *, tn,
                    n_col_tiles):
    j = pl.program_id(1)

    def tile(col0):
        first = col0 == (n_col_tiles - 1) * tn
        if first:
            h_ref[...] = _rms_rows(x_ref[...], g_ref[...]).astype(BF16)
        acc = jnp.dot(h_ref[...], w_ref[...], preferred_element_type=F32)
        if first:
            cos_ref[...] = jnp.cos(ang_ref[...])
            sin_ref[...] = jnp.sin(ang_ref[...]) * sign_ref[...]
        cosf, sinf = cos_ref[...], sin_ref[...]
        for c in range(tn // HEAD_DIM):
            col = col0 + c * HEAD_DIM
            r = acc[:, c * HEAD_DIM:(c + 1) * HEAD_DIM]
            if col < Q_OFF:
                r = _gelu(r)
            elif col < V_OFF:
                r = r * cosf + pltpu.roll(r, HEAD_DIM // 2, 1) * sinf
                if col < K_OFF:
                    r = r * (LOG2_E / math.sqrt(HEAD_DIM))
                else:
                    kt_ref[col - K_OFF:col - K_OFF + HEAD_DIM, :] = r.T.astype(kt_ref.dtype)
            z_ref[:, c * HEAD_DIM:(c + 1) * HEAD_DIM] = r.astype(z_ref.dtype)

    for jt in range(n_col_tiles):
        pl.when(j == n_col_tiles - 1 - jt)(functools.partial(tile, jt * tn))


def _in_proj(x2, gain, w, layer, ang, sign, seq_len):
    m, d = x2.shape
    n = w.shape[2]
    tm, tn = IN_PROJ_TM, IN_PROJ_TN
    assert m % tm == 0 and n % tn == 0 and tn % HEAD_DIM == 0 and seq_len % tm == 0
    nsb = seq_len // tm
    return pl.pallas_call(
        functools.partial(_in_proj_kernel, tn=tn, n_col_tiles=n // tn),
        grid=(m // tm, n // tn),
        in_specs=[
            pl.BlockSpec((tm, d), lambda i, j: (i, 0)),
            pl.BlockSpec((1, d), lambda i, j: (0, 0)),
            pl.BlockSpec((None, d, tn), lambda i, j: (layer, 0, n // tn - 1 - j)),
            pl.BlockSpec((tm, HEAD_DIM), lambda i, j: (i, 0)),
            pl.BlockSpec((1, HEAD_DIM), lambda i, j: (0, 0)),
        ],
        out_specs=[pl.BlockSpec((tm, tn), lambda i, j: (i, n // tn - 1 - j)),
                   pl.BlockSpec((None, DIFF_QK_WIDTH, tm), lambda i, j: (i // nsb, 0, i % nsb))],
        out_shape=[jax.ShapeDtypeStruct((m, n), BF16),
                   jax.ShapeDtypeStruct((m // seq_len, DIFF_QK_WIDTH, seq_len), BF16)],
        scratch_shapes=[pltpu.VMEM((tm, d), BF16),
                        pltpu.VMEM((tm, HEAD_DIM), F32),
                        pltpu.VMEM((tm, HEAD_DIM), F32)],
        compiler_params=_params("parallel", "arbitrary"),
        name="in_proj",
    )(x2, gain, w, ang, sign)


def _gmlp_mixer(za_ref, lng_ref, lnb_ref, ws_ref, bs_ref, ga_ref, vn_ref, mixed_ref, out_ref, rows):
    groups = [slice(g * HEAD_DIM, (g + 1) * HEAD_DIM) for g in range(N_GROUPS)]
    for g, cols in enumerate(groups):
        v = za_ref[:, GMLP_WIDTH + cols.start:GMLP_WIDTH + cols.stop].astype(F32)
        mu = jnp.mean(v, axis=-1, keepdims=True)
        vc = v - mu
        var = jnp.mean(vc * vc, axis=-1, keepdims=True)
        vn = vc * lax.rsqrt(var + LN_EPS) * lng_ref[g:g + 1, :] + lnb_ref[g:g + 1, :]
        vn_ref[:, cols] = vn.astype(BF16)
    t_idx = lax.broadcasted_iota(jnp.int32, (GMLP_CHUNK, GMLP_CHUNK), 0)
    s_idx = lax.broadcasted_iota(jnp.int32, (GMLP_CHUNK, GMLP_CHUNK), 1)
    for g, cols in enumerate(groups):
        w = jnp.where(s_idx <= t_idx, ws_ref[g], 0.0).astype(BF16)
        for c in range(rows // GMLP_CHUNK):
            chunk = slice(c * GMLP_CHUNK, (c + 1) * GMLP_CHUNK)
            mixed_ref[chunk, cols] = jnp.dot(w, vn_ref[chunk, cols], preferred_element_type=F32)
    for g, cols in enumerate(groups):
        u = za_ref[:, cols].astype(F32)
        ya = _group_rms(u * (mixed_ref[:, cols] + bs_ref[g])) * ga_ref[:, cols]
        out_ref[:, cols] = ya.astype(out_ref.dtype)


def _short_conv_mixer(bg_ref, cg_ref, hc_ref, cgh_ref, hch_ref, cw_ref, gc_ref, out_ref, at_seq_start):
    xh = cg_ref[...].astype(F32) * hc_ref[...].astype(F32)
    halo = cgh_ref[...].astype(F32) * hch_ref[...].astype(F32)
    halo = jnp.where(at_seq_start, 0.0, halo)
    row = lax.broadcasted_iota(jnp.int32, xh.shape, 0)
    prev1 = jnp.where(row == 0, halo[CONV_HALO_ROWS - 1:CONV_HALO_ROWS, :], pltpu.roll(xh, 1, 0))
    prev2 = jnp.where(row == 0, halo[CONV_HALO_ROWS - 2:CONV_HALO_ROWS - 1, :],
                      jnp.where(row == 1, halo[CONV_HALO_ROWS - 1:CONV_HALO_ROWS, :], pltpu.roll(xh, 2, 0)))
    y = cw_ref[0:1, :] * prev2 + cw_ref[1:2, :] * prev1 + cw_ref[2:3, :] * xh
    yc = bg_ref[...].astype(F32) * y
    for g in range(N_GROUPS):
        cols = slice(g * HEAD_DIM, (g + 1) * HEAD_DIM)
        out_ref[:, cols] = (_group_rms(yc[:, cols]) * gc_ref[:, cols]).astype(out_ref.dtype)


def _mix_out_kernel(x_ref, za_ref, bg_ref, cg_ref, hc_ref, cgh_ref, hch_ref, mb_ref, lng_ref, lnb_ref, ws_ref,
                    bs_ref, cw_ref, ga_ref, gc_ref, w0_ref, w1_ref, w2_ref, w3_ref, o_ref, vn_ref, mixed_ref,
                    ma_ref, mc_ref, *, rows, blocks_per_seq):
    acc = jnp.dot(mb_ref[:, :GMLP_WIDTH], w1_ref[...], preferred_element_type=F32)
    acc = acc + jnp.dot(mb_ref[:, GMLP_WIDTH:], w2_ref[...], preferred_element_type=F32)
    _gmlp_mixer(za_ref, lng_ref, lnb_ref, ws_ref, bs_ref, ga_ref, vn_ref, mixed_ref, ma_ref, rows)
    _short_conv_mixer(bg_ref, cg_ref, hc_ref, cgh_ref, hch_ref, cw_ref, gc_ref, mc_ref,
                      pl.program_id(0) % blocks_per_seq == 0)
    acc = acc + jnp.dot(ma_ref[...], w0_ref[...], preferred_element_type=F32)
    acc = acc + jnp.dot(mc_ref[...], w3_ref[...], preferred_element_type=F32)
    o_ref[...] = x_ref[...] + acc


def _mix_out_proj(x2, z, mix_b, ln_g, ln_b, ws, bs, conv_w, gain_a, gain_c, w, layer, seq_len):
    m, d = x2.shape
    rows = OUT_PROJ_TM
    assert m % rows == 0 and seq_len % rows == 0 and rows % GMLP_CHUNK == 0 and C_OFF % CONV_WIDTH == 0
    assert DIFF_WIDTH == 2 * GMLP_WIDTH and GMLP_WIDTH == CONV_WIDTH
    cb = C_OFF // CONV_WIDTH
    hb = rows // CONV_HALO_ROWS
    full = lambda shape: pl.BlockSpec(shape, lambda i: (0,) * len(shape))
    zcol = lambda width, col: pl.BlockSpec((rows, width), lambda i: (i, col))
    halo = lambda col: pl.BlockSpec((CONV_HALO_ROWS, CONV_WIDTH), lambda i: (jnp.maximum(i * hb - 1, 0), col))
    wrows = lambda blk: pl.BlockSpec((None, GMLP_WIDTH, d), lambda i: (layer, blk, 0))
    return pl.pallas_call(
        functools.partial(_mix_out_kernel, rows=rows, blocks_per_seq=seq_len // rows),
        grid=(m // rows,),
        in_specs=[
            pl.BlockSpec((rows, d), lambda i: (i, 0)),
            zcol(2 * GMLP_WIDTH, 0), zcol(CONV_WIDTH, cb), zcol(CONV_WIDTH, cb + 1), zcol(CONV_WIDTH, cb + 2),
            halo(cb + 1), halo(cb + 2),
            pl.BlockSpec((rows, DIFF_WIDTH), lambda i: (i, 0)),
            full((N_GROUPS, HEAD_DIM)), full((N_GROUPS, HEAD_DIM)),
            full((N_GROUPS, GMLP_CHUNK, GMLP_CHUNK)), full((N_GROUPS, rows, 1)),
            full((CONV_K, CONV_WIDTH)), full((1, GMLP_WIDTH)), full((1, CONV_WIDTH)),
            wrows(0), wrows(1), wrows(2), wrows(3),
        ],
        out_specs=pl.BlockSpec((rows, d), lambda i: (i, 0)),
        out_shape=jax.ShapeDtypeStruct((m, d), F32),
        scratch_shapes=[pltpu.VMEM((rows, GMLP_WIDTH), BF16),
                        pltpu.VMEM((rows, GMLP_WIDTH), F32),
                        pltpu.VMEM((rows, GMLP_WIDTH), BF16),
                        pltpu.VMEM((rows, CONV_WIDTH), BF16)],
        compiler_params=_params("parallel"),
        name="mix_out_proj",
    )(x2, z, z, z, z, z, z, mix_b, ln_g, ln_b, ws, bs, conv_w, gain_a, gain_c, w, w, w, w)


def _attn_kernel(lq1_ref, lk1_ref, lq2_ref, lk2_ref, q_ref, kt_ref, v_ref, gain_ref, *rest, tq, n_q, n_buf, n_cast,
                 lam_init):
    cast_src, (o_ref, *cast_dst) = rest[:n_cast], rest[n_cast:2 * n_cast + 1]
    m_ref, l_ref, acc_ref, *bufs = rest[2 * n_cast + 1:]
    for src_ref, dst_ref in zip(cast_src, cast_dst):
        dst_ref[...] = src_ref[...].astype(dst_ref.dtype)

    rc = ATTN_ROW_CHUNK
    n_tiles = tq // HEAD_DIM
    p_bufs, a_bufs = bufs[:n_buf], bufs[n_buf:]

    def key_rows(j):
        return slice(j * tq, (j + 1) * tq)

    def scores(j):
        return [lax.dot_general(q_ref[0, :, mp * HEAD_DIM:(mp + 1) * HEAD_DIM],
                                kt_ref[0, mp * HEAD_DIM:(mp + 1) * HEAD_DIM, key_rows(j)],
                                (((1,), (0,)), ((), ())), preferred_element_type=F32) for mp in range(2)]

    def softmax(s_maps, buf, diagonal, first):
        p_ref, alpha_ref = p_bufs[buf], a_bufs[buf]
        for c in range(2 * tq // rc):
            rows = slice(c * rc, (c + 1) * rc)
            q0 = rows.start % tq
            s = s_maps[rows.start // tq]
            s_rows = slice(q0, q0 + rc)
            live = [t for t in range(n_tiles) if (not diagonal) or t * HEAD_DIM <= q0 + rc - 1]
            tiles = []
            for t in live:
                st = s[s_rows, t * HEAD_DIM:(t + 1) * HEAD_DIM]
                if diagonal and (t + 1) * HEAD_DIM - 1 > q0:
                    row = lax.broadcasted_iota(jnp.int32, (rc, HEAD_DIM), 0) + q0
                    col = lax.broadcasted_iota(jnp.int32, (rc, HEAD_DIM), 1) + t * HEAD_DIM
                    st = jnp.where(col <= row, st, -jnp.inf)
                tiles.append(st)
            m_blk = jnp.max(functools.reduce(jnp.maximum, tiles), axis=-1, keepdims=True)
            if first:
                m_new = jnp.broadcast_to(m_blk, (rc, HEAD_DIM))
                ps = [jnp.exp2(st - m_new) for st in tiles]
                l_ref[rows, :] = functools.reduce(jnp.add, ps)
            else:
                m_old = m_ref[rows, :]
                m_new = jnp.maximum(m_old, m_blk)
                alpha = jnp.exp2(m_old - m_new)
                ps = [jnp.exp2(st - m_new) for st in tiles]
                l_ref[rows, :] = alpha * l_ref[rows, :] + functools.reduce(jnp.add, ps)
                alpha_ref[rows, :] = alpha
            m_ref[rows, :] = m_new
            for t in range(n_tiles):
                p_t = ps[live.index(t)].astype(BF16) if t in live else jnp.zeros((rc, HEAD_DIM), BF16)
                p_ref[rows, t * HEAD_DIM:(t + 1) * HEAD_DIM] = p_t

    def accumulate(j, buf, first):
        pv = jnp.dot(p_bufs[buf][...], v_ref[0, key_rows(j), :], preferred_element_type=F32)
        if first:
            acc_ref[...] = pv
        else:
            alpha = a_bufs[buf][...]
            for t in range(DIFF_V_DIM // HEAD_DIM):
                cols = slice(t * HEAD_DIM, (t + 1) * HEAD_DIM)
                acc_ref[:, cols] = acc_ref[:, cols] * alpha + pv[:, cols]

    def query_block(qi):
        s_next = scores(0)
        for j in range(qi + 1):
            s = s_next
            if j < qi:
                s_next = scores(j + 1)
            if j >= 1:
                accumulate(j - 1, (j - 1) % n_buf, first=(j == 1))
            softmax(s, j % n_buf, diagonal=(j == qi), first=(j == 0))
        accumulate(qi, qi % n_buf, first=(qi == 0))

        lam = (jnp.exp(jnp.sum(lq1_ref[...] * lk1_ref[...], axis=-1, keepdims=True))
               - jnp.exp(jnp.sum(lq2_ref[...] * lk2_ref[...], axis=-1, keepdims=True)) + lam_init)
        o = acc_ref[...] * (1.0 / jnp.sum(l_ref[...], axis=-1, keepdims=True))
        o = o[:tq] - lam * o[tq:]
        o = _group_rms(o) * (1.0 - lam_init) * gain_ref[0]
        o_ref[0] = o.astype(o_ref.dtype)

    for qi in range(n_q):
        pl.when(pl.program_id(2) == qi)(functools.partial(query_block, qi))


def _diff_attention(z3, kt, lq1, lk1, lq2, lk2, gain_b, lam_init, cast_weights):
    b, s, _ = z3.shape
    tq = ATTN_TQ
    assert s % tq == 0
    n_steps = b * N_GROUPS * (s // tq)
    cast_in_specs, cast_out_specs, cast_out_shapes = [], [], []
    for w, layer in cast_weights:
        _, r, c = w.shape
        slab = next(k for k in range(CAST_SLAB_ROWS, r + 1, CAST_SLAB_ROWS) if r % k == 0 and r // k <= n_steps)
        step = lambda bi, h, i, last=r // slab - 1: jnp.minimum((bi * N_GROUPS + h) * (s // tq) + i, last)
        cast_in_specs.append(pl.BlockSpec((None, slab, c), lambda bi, h, i, layer=layer, step=step: (layer, step(bi, h, i), 0)))
        cast_out_specs.append(pl.BlockSpec((slab, c), lambda bi, h, i, step=step: (step(bi, h, i), 0)))
        cast_out_shapes.append(jax.ShapeDtypeStruct((r, c), BF16))
    n_buf = s // tq
    qb, vb = Q_OFF // DIFF_V_DIM, V_OFF // DIFF_V_DIM
    vec = pl.BlockSpec((1, HEAD_DIM), lambda bi, h, i: (0, 0))
    outs = pl.pallas_call(
        functools.partial(_attn_kernel, tq=tq, n_q=s // tq, n_buf=n_buf, n_cast=len(cast_weights),
                          lam_init=lam_init),
        grid=(b, N_GROUPS, s // tq),
        in_specs=[
            vec, vec, vec, vec,
            pl.BlockSpec((1, tq, DIFF_V_DIM), lambda bi, h, i: (bi, i, qb + h)),
            pl.BlockSpec((1, DIFF_V_DIM, s), lambda bi, h, i: (bi, h, 0)),
            pl.BlockSpec((1, s, DIFF_V_DIM), lambda bi, h, i: (bi, 0, vb + h)),
            pl.BlockSpec((1, 1, DIFF_V_DIM), lambda bi, h, i: (h, 0, 0)),
        ] + cast_in_specs,
        out_specs=[pl.BlockSpec((1, tq, DIFF_V_DIM), lambda bi, h, i: (bi, i, h))] + cast_out_specs,
        out_shape=[jax.ShapeDtypeStruct((b, s, DIFF_WIDTH), BF16)] + cast_out_shapes,
        scratch_shapes=[
            pltpu.VMEM((2 * tq, HEAD_DIM), F32),
            pltpu.VMEM((2 * tq, HEAD_DIM), F32),
            pltpu.VMEM((2 * tq, DIFF_V_DIM), F32),
        ] + [pltpu.VMEM((2 * tq, tq), BF16)] * n_buf
          + [pltpu.VMEM((2 * tq, HEAD_DIM), F32)] * n_buf,
        compiler_params=_params("arbitrary", "arbitrary", "arbitrary"),
        name="diff_attention",
    )(lq1, lk1, lq2, lk2, z3, kt, z3, gain_b, *[w for w, _ in cast_weights])
    return outs[0], outs[1:]


def _ffn_kernel(x_ref, g_ref, wg_ref, wu_ref, wd_ref, fg_ref, o_ref, h_ref, *, final_norm):
    f = pl.program_id(1)
    last_f = pl.num_programs(1) - 1

    def step(first, last):
        if first:
            h_ref[...] = _rms_rows(x_ref[...], g_ref[...]).astype(BF16)
        h = h_ref[...]
        gate = jnp.dot(h, wg_ref[...], preferred_element_type=F32)
        up = jnp.dot(h, wu_ref[...], preferred_element_type=F32)
        act = (gate * jax.nn.sigmoid(gate) * up).astype(BF16)
        out = (x_ref[...] if first else o_ref[...]) + jnp.dot(act, wd_ref[...], preferred_element_type=F32)
        if last and final_norm:
            out = _rms_rows(out, fg_ref[...])
        o_ref[...] = out

    pl.when(f == 0)(functools.partial(step, True, False))
    if final_norm:
        pl.when((f > 0) & (f < last_f))(functools.partial(step, False, False))
        pl.when(f == last_f)(functools.partial(step, False, True))
    else:
        pl.when(f > 0)(functools.partial(step, False, False))


def _ffn(x2, gain, wg, wu, wd, layer, final_gain, final_norm):
    m, d = x2.shape
    dff = wg.shape[2]
    tm, tf = FFN_TM, FFN_TF
    assert m % tm == 0 and dff % tf == 0
    return pl.pallas_call(
        functools.partial(_ffn_kernel, final_norm=final_norm),
        grid=(m // tm, dff // tf),
        in_specs=[
            pl.BlockSpec((tm, d), lambda i, f: (i, 0)),
            pl.BlockSpec((1, d), lambda i, f: (0, 0)),
            pl.BlockSpec((None, d, tf), lambda i, f: (layer, 0, f)),
            pl.BlockSpec((None, d, tf), lambda i, f: (layer, 0, f)),
            pl.BlockSpec((None, tf, d), lambda i, f: (layer, f, 0)),
            pl.BlockSpec((1, d), lambda i, f: (0, 0)),
        ],
        out_specs=pl.BlockSpec((tm, d), lambda i, f: (i, 0)),
        out_shape=jax.ShapeDtypeStruct((m, d), F32),
        scratch_shapes=[pltpu.VMEM((tm, d), BF16)],
        compiler_params=_params("parallel", "arbitrary"),
        name="ffn",
    )(x2, gain, wg, wu, wd, final_gain)


def kernel(x, positions, attn_norm, w_in, gmlp_ln_g, gmlp_ln_b, gmlp_ws, gmlp_bs, lambda_q1, lambda_k1,
           lambda_q2, lambda_k2, conv_w, mix_norm, w_out, ffn_norm, w_gate, w_up, w_down, final_norm):
    b, s, d = x.shape
    m = b * s
    depth = w_in.shape[0]

    inv_freq = 1.0 / (ROPE_THETA ** (jnp.arange(0, HEAD_DIM, 2, dtype=F32) / HEAD_DIM))
    ang = positions.astype(F32).reshape(m, 1) * jnp.concatenate([inv_freq, inv_freq]).reshape(1, HEAD_DIM)
    sign = jnp.concatenate([-jnp.ones((HEAD_DIM // 2,), F32), jnp.ones((HEAD_DIM // 2,), F32)]).reshape(1, HEAD_DIM)

    w_in_l, w_out_l = w_in[0].astype(BF16), w_out[0].astype(BF16)
    x2 = x.reshape(m, d)
    for l in range(depth):
        lam_init = 0.8 - 0.6 * math.exp(-0.3 * l)
        gain = mix_norm[l].astype(F32)
        z, kt = _in_proj(x2, attn_norm[l].reshape(1, d), w_in_l[None], 0, ang, sign, s)
        z3 = z.reshape(b, s, z.shape[1])
        mix_b, casts = _diff_attention(
            z3, kt, lambda_q1[l].reshape(1, HEAD_DIM), lambda_k1[l].reshape(1, HEAD_DIM),
            lambda_q2[l].reshape(1, HEAD_DIM), lambda_k2[l].reshape(1, HEAD_DIM),
            gain[GMLP_WIDTH:GMLP_WIDTH + DIFF_WIDTH].reshape(N_GROUPS, 1, DIFF_V_DIM), lam_init,
            [(w_gate, l), (w_up, l), (w_down, l)] + ([(w_in, l + 1), (w_out, l + 1)] if l + 1 < depth else []))
        w_gate_l, w_up_l, w_down_l = casts[:3]
        x2 = _mix_out_proj(
            x2, z, mix_b.reshape(m, DIFF_WIDTH), gmlp_ln_g[l], gmlp_ln_b[l], gmlp_ws[l],
            jnp.tile(gmlp_bs[l], (1, OUT_PROJ_TM // GMLP_CHUNK)).reshape(N_GROUPS, OUT_PROJ_TM, 1), conv_w[l],
            gain[:GMLP_WIDTH].reshape(1, GMLP_WIDTH), gain[GMLP_WIDTH + DIFF_WIDTH:].reshape(1, CONV_WIDTH),
            w_out_l[None], 0, s)
        x2 = _ffn(x2, ffn_norm[l].reshape(1, d), w_gate_l[None], w_up_l[None], w_down_l[None], 0,
                  final_norm.reshape(1, d), final_norm=(l == depth - 1))
        if l + 1 < depth:
            w_in_l, w_out_l = casts[3], casts[4]
    return x2.reshape(b, s, d)
```

```python
import functools
import math

import jax
import jax.numpy as jnp
from jax import lax
from jax.experimental import pallas as pl
from jax.experimental.pallas import tpu as pltpu

F32 = jnp.float32
BF16 = jnp.bfloat16

HEAD_DIM = 128
N_GROUPS = 4
GMLP_WIDTH = N_GROUPS * HEAD_DIM
GMLP_CHUNK = 128
DIFF_QK_WIDTH = N_GROUPS * 2 * HEAD_DIM
DIFF_V_DIM = 2 * HEAD_DIM
DIFF_WIDTH = N_GROUPS * DIFF_V_DIM
CONV_WIDTH = N_GROUPS * HEAD_DIM
CONV_K = 3
ROPE_THETA = 10000.0
RMS_EPS = 1e-6
LN_EPS = 1e-5
LOG2_E = math.log2(math.e)

Q_OFF = 2 * GMLP_WIDTH
K_OFF = Q_OFF + DIFF_QK_WIDTH
V_OFF = K_OFF + DIFF_QK_WIDTH
C_OFF = V_OFF + DIFF_WIDTH

VMEM_LIMIT_BYTES = 56 * 1024 * 1024

IN_PROJ_TM = 512
IN_PROJ_TN = 2816
ATTN_TQ = 512
CAST_SLAB_ROWS = 16
ATTN_ROW_CHUNK = 32
OUT_PROJ_TM = 512
FFN_TM = 1024
FFN_TF = 512
CONV_HALO_ROWS = 8


def _params(*semantics):
    return pltpu.CompilerParams(dimension_semantics=semantics, vmem_limit_bytes=VMEM_LIMIT_BYTES)


def _rms_rows(x, gain):
    return x * lax.rsqrt(jnp.mean(x * x, axis=-1, keepdims=True) + RMS_EPS) * gain


def _group_rms(x):
    return x * lax.rsqrt(jnp.mean(x * x, axis=-1, keepdims=True) + RMS_EPS)


def _gelu(x):
    return 0.5 * x * (1.0 + lax.erf(x * math.sqrt(0.5)))


def _cast_specs(cast_weights, n_steps, step_of, n_grid_axes):
    in_specs, out_specs, out_shapes = [], [], []
    for w, layer in cast_weights:
        _, r, c = w.shape
        slab = next(k for k in range(CAST_SLAB_ROWS, r + 1, CAST_SLAB_ROWS) if r % k == 0 and r // k <= n_steps)
        blk = lambda *idx, last=r // slab - 1: jnp.minimum(step_of(*idx[:n_grid_axes]), last)
        in_specs.append(pl.BlockSpec((None, slab, c), lambda *idx, layer=layer, blk=blk: (layer, blk(*idx), 0)))
        out_specs.append(pl.BlockSpec((slab, c), lambda *idx, blk=blk: (blk(*idx), 0)))
        out_shapes.append(jax.ShapeDtypeStruct((r, c), BF16))
    return in_specs, out_specs, out_shapes


def _cast_slabs(src_refs, dst_refs):
    for src_ref, dst_ref in zip(src_refs, dst_refs):
        dst_ref[...] = src_ref[...].astype(dst_ref.dtype)


def _in_proj_kernel(x_ref, g_ref, w_ref, ang_ref, sign_ref, z_ref, h_ref, cos_ref, sin_ref, *, tn, n_col_tiles):
    j = pl.program_id(1)

    def tile(col0):
        first = col0 == (n_col_tiles - 1) * tn
        if first:
            h_ref[...] = _rms_rows(x_ref[...], g_ref[...]).astype(BF16)
        acc = jnp.dot(h_ref[...], w_ref[...], preferred_element_type=F32)
        if first:
            cos_ref[...] = jnp.cos(ang_ref[...])
            sin_ref[...] = jnp.sin(ang_ref[...]) * sign_ref[...]
        cosf, sinf = cos_ref[...], sin_ref[...]
        for c in range(tn // HEAD_DIM):
            col = col0 + c * HEAD_DIM
            r = acc[:, c * HEAD_DIM:(c + 1) * HEAD_DIM]
            if col < Q_OFF:
                r = _gelu(r)
            elif col < V_OFF:
                r = r * cosf + pltpu.roll(r, HEAD_DIM // 2, 1) * sinf
                if col < K_OFF:
                    r = r * (LOG2_E / math.sqrt(HEAD_DIM))
            z_ref[:, c * HEAD_DIM:(c + 1) * HEAD_DIM] = r.astype(z_ref.dtype)

    for jt in range(n_col_tiles):
        pl.when(j == n_col_tiles - 1 - jt)(functools.partial(tile, jt * tn))


def _in_proj(x2, gain, w, layer, ang, sign):
    m, d = x2.shape
    n = w.shape[2]
    tm, tn = IN_PROJ_TM, IN_PROJ_TN
    assert m % tm == 0 and n % tn == 0 and tn % HEAD_DIM == 0
    return pl.pallas_call(
        functools.partial(_in_proj_kernel, tn=tn, n_col_tiles=n // tn),
        grid=(m // tm, n // tn),
        in_specs=[
            pl.BlockSpec((tm, d), lambda i, j: (i, 0)),
            pl.BlockSpec((1, d), lambda i, j: (0, 0)),
            pl.BlockSpec((None, d, tn), lambda i, j: (layer, 0, n // tn - 1 - j)),
            pl.BlockSpec((tm, HEAD_DIM), lambda i, j: (i, 0)),
            pl.BlockSpec((1, HEAD_DIM), lambda i, j: (0, 0)),
        ],
        out_specs=pl.BlockSpec((tm, tn), lambda i, j: (i, n // tn - 1 - j)),
        out_shape=jax.ShapeDtypeStruct((m, n), BF16),
        scratch_shapes=[pltpu.VMEM((tm, d), BF16),
                        pltpu.VMEM((tm, HEAD_DIM), F32),
                        pltpu.VMEM((tm, HEAD_DIM), F32)],
        compiler_params=_params("parallel", "arbitrary"),
        name="in_proj",
    )(x2, gain, w, ang, sign)


def _gmlp_mixer(za_ref, lng_ref, lnb_ref, ws_ref, bs_ref, ga_ref, vn_ref, mixed_ref, out_ref, rows):
    groups = [slice(g * HEAD_DIM, (g + 1) * HEAD_DIM) for g in range(N_GROUPS)]
    for g, cols in enumerate(groups):
        v = za_ref[:, GMLP_WIDTH + cols.start:GMLP_WIDTH + cols.stop].astype(F32)
        mu = jnp.mean(v, axis=-1, keepdims=True)
        vc = v - mu
        var = jnp.mean(vc * vc, axis=-1, keepdims=True)
        vn = vc * lax.rsqrt(var + LN_EPS) * lng_ref[g:g + 1, :] + lnb_ref[g:g + 1, :]
        vn_ref[:, cols] = vn.astype(BF16)
    t_idx = lax.broadcasted_iota(jnp.int32, (GMLP_CHUNK, GMLP_CHUNK), 0)
    s_idx = lax.broadcasted_iota(jnp.int32, (GMLP_CHUNK, GMLP_CHUNK), 1)
    for g, cols in enumerate(groups):
        w = jnp.where(s_idx <= t_idx, ws_ref[g], 0.0).astype(BF16)
        for c in range(rows // GMLP_CHUNK):
            chunk = slice(c * GMLP_CHUNK, (c + 1) * GMLP_CHUNK)
            mixed_ref[chunk, cols] = jnp.dot(w, vn_ref[chunk, cols], preferred_element_type=F32)
    for g, cols in enumerate(groups):
        u = za_ref[:, cols].astype(F32)
        ya = _group_rms(u * (mixed_ref[:, cols] + bs_ref[g])) * ga_ref[:, cols]
        out_ref[:, cols] = ya.astype(out_ref.dtype)


def _short_conv_mixer(bg_ref, cg_ref, hc_ref, cgh_ref, hch_ref, cw_ref, gc_ref, out_ref, at_seq_start):
    xh = cg_ref[...].astype(F32) * hc_ref[...].astype(F32)
    halo = cgh_ref[...].astype(F32) * hch_ref[...].astype(F32)
    halo = jnp.where(at_seq_start, 0.0, halo)
    row = lax.broadcasted_iota(jnp.int32, xh.shape, 0)
    prev1 = jnp.where(row == 0, halo[CONV_HALO_ROWS - 1:CONV_HALO_ROWS, :], pltpu.roll(xh, 1, 0))
    prev2 = jnp.where(row == 0, halo[CONV_HALO_ROWS - 2:CONV_HALO_ROWS - 1, :],
                      jnp.where(row == 1, halo[CONV_HALO_ROWS - 1:CONV_HALO_ROWS, :], pltpu.roll(xh, 2, 0)))
    y = cw_ref[0:1, :] * prev2 + cw_ref[1:2, :] * prev1 + cw_ref[2:3, :] * xh
    yc = bg_ref[...].astype(F32) * y
    for g in range(N_GROUPS):
        cols = slice(g * HEAD_DIM, (g + 1) * HEAD_DIM)
        out_ref[:, cols] = (_group_rms(yc[:, cols]) * gc_ref[:, cols]).astype(out_ref.dtype)


def _mix_out_kernel(x_ref, za_ref, bg_ref, cg_ref, hc_ref, cgh_ref, hch_ref, mb_ref, lng_ref, lnb_ref, ws_ref,
                    bs_ref, cw_ref, ga_ref, gc_ref, w0_ref, w1_ref, w2_ref, w3_ref, *rest, rows, blocks_per_seq,
                    n_cast):
    cast_src, (o_ref, *cast_dst) = rest[:n_cast], rest[n_cast:2 * n_cast + 1]
    vn_ref, mixed_ref, ma_ref, mc_ref = rest[2 * n_cast + 1:]
    _cast_slabs(cast_src, cast_dst)
    acc = jnp.dot(mb_ref[:, :GMLP_WIDTH], w1_ref[...], preferred_element_type=F32)
    acc = acc + jnp.dot(mb_ref[:, GMLP_WIDTH:], w2_ref[...], preferred_element_type=F32)
    _gmlp_mixer(za_ref, lng_ref, lnb_ref, ws_ref, bs_ref, ga_ref, vn_ref, mixed_ref, ma_ref, rows)
    _short_conv_mixer(bg_ref, cg_ref, hc_ref, cgh_ref, hch_ref, cw_ref, gc_ref, mc_ref,
                      pl.program_id(0) % blocks_per_seq == 0)
    acc = acc + jnp.dot(ma_ref[...], w0_ref[...], preferred_element_type=F32)
    acc = acc + jnp.dot(mc_ref[...], w3_ref[...], preferred_element_type=F32)
    o_ref[...] = x_ref[...] + acc


def _mix_out_proj(x2, z, mix_b, ln_g, ln_b, ws, bs, conv_w, gain_a, gain_c, w, layer, seq_len, cast_weights):
    m, d = x2.shape
    rows = OUT_PROJ_TM
    assert m % rows == 0 and seq_len % rows == 0 and rows % GMLP_CHUNK == 0 and C_OFF % CONV_WIDTH == 0
    assert DIFF_WIDTH == 2 * GMLP_WIDTH and GMLP_WIDTH == CONV_WIDTH
    cb = C_OFF // CONV_WIDTH
    hb = rows // CONV_HALO_ROWS
    full = lambda shape: pl.BlockSpec(shape, lambda i: (0,) * len(shape))
    zcol = lambda width, col: pl.BlockSpec((rows, width), lambda i: (i, col))
    halo = lambda col: pl.BlockSpec((CONV_HALO_ROWS, CONV_WIDTH), lambda i: (jnp.maximum(i * hb - 1, 0), col))
    wrows = lambda blk: pl.BlockSpec((None, GMLP_WIDTH, d), lambda i: (layer, blk, 0))
    cast_in, cast_out, cast_shapes = _cast_specs(cast_weights, m // rows, lambda i: i, 1)
    outs = pl.pallas_call(
        functools.partial(_mix_out_kernel, rows=rows, blocks_per_seq=seq_len // rows, n_cast=len(cast_weights)),
        grid=(m // rows,),
        in_specs=[
            pl.BlockSpec((rows, d), lambda i: (i, 0)),
            zcol(2 * GMLP_WIDTH, 0), zcol(CONV_WIDTH, cb), zcol(CONV_WIDTH, cb + 1), zcol(CONV_WIDTH, cb + 2),
            halo(cb + 1), halo(cb + 2),
            pl.BlockSpec((rows, DIFF_WIDTH), lambda i: (i, 0)),
            full((N_GROUPS, HEAD_DIM)), full((N_GROUPS, HEAD_DIM)),
            full((N_GROUPS, GMLP_CHUNK, GMLP_CHUNK)), full((N_GROUPS, rows, 1)),
            full((CONV_K, CONV_WIDTH)), full((1, GMLP_WIDTH)), full((1, CONV_WIDTH)),
            wrows(0), wrows(1), wrows(2), wrows(3),
        ] + cast_in,
        out_specs=[pl.BlockSpec((rows, d), lambda i: (i, 0))] + cast_out,
        out_shape=[jax.ShapeDtypeStruct((m, d), F32)] + cast_shapes,
        scratch_shapes=[pltpu.VMEM((rows, GMLP_WIDTH), BF16),
                        pltpu.VMEM((rows, GMLP_WIDTH), F32),
                        pltpu.VMEM((rows, GMLP_WIDTH), BF16),
                        pltpu.VMEM((rows, CONV_WIDTH), BF16)],
        compiler_params=_params("arbitrary"),
        name="mix_out_proj",
    )(x2, z, z, z, z, z, z, mix_b, ln_g, ln_b, ws, bs, conv_w, gain_a, gain_c, w, w, w, w, *[cw for cw, _ in cast_weights])
    return outs[0], outs[1:]


def _attn_kernel(lq1_ref, lk1_ref, lq2_ref, lk2_ref, q_ref, k_ref, v_ref, gain_ref, *rest, tq, n_q, n_buf, n_cast,
                 lam_init):
    cast_src, (o_ref, *cast_dst) = rest[:n_cast], rest[n_cast:2 * n_cast + 1]
    m_ref, l_ref, acc_ref, *bufs = rest[2 * n_cast + 1:]
    _cast_slabs(cast_src, cast_dst)

    rc = ATTN_ROW_CHUNK
    n_tiles = tq // HEAD_DIM
    p_bufs, a_bufs = bufs[:n_buf], bufs[n_buf:]

    def key_rows(j):
        return slice(j * tq, (j + 1) * tq)

    def scores(j):
        return [lax.dot_general(q_ref[0, :, mp * HEAD_DIM:(mp + 1) * HEAD_DIM],
                                k_ref[0, key_rows(j), mp * HEAD_DIM:(mp + 1) * HEAD_DIM],
                                (((1,), (1,)), ((), ())), preferred_element_type=F32) for mp in range(2)]

    def softmax(s_maps, buf, diagonal, first):
        p_ref, alpha_ref = p_bufs[buf], a_bufs[buf]
        for c in range(2 * tq // rc):
            rows = slice(c * rc, (c + 1) * rc)
            q0 = rows.start % tq
            s = s_maps[rows.start // tq]
            s_rows = slice(q0, q0 + rc)
            live = [t for t in range(n_tiles) if (not diagonal) or t * HEAD_DIM <= q0 + rc - 1]
            tiles = []
            for t in live:
                st = s[s_rows, t * HEAD_DIM:(t + 1) * HEAD_DIM]
                if diagonal and (t + 1) * HEAD_DIM - 1 > q0:
                    row = lax.broadcasted_iota(jnp.int32, (rc, HEAD_DIM), 0) + q0
                    col = lax.broadcasted_iota(jnp.int32, (rc, HEAD_DIM), 1) + t * HEAD_DIM
                    st = jnp.where(col <= row, st, -jnp.inf)
                tiles.append(st)
            m_blk = jnp.max(functools.reduce(jnp.maximum, tiles), axis=-1, keepdims=True)
            if first:
                m_new = jnp.broadcast_to(m_blk, (rc, HEAD_DIM))
                ps = [jnp.exp2(st - m_new) for st in tiles]
                l_ref[rows, :] = functools.reduce(jnp.add, ps)
            else:
                m_old = m_ref[rows, :]
                m_new = jnp.maximum(m_old, m_blk)
                alpha = jnp.exp2(m_old - m_new)
                ps = [jnp.exp2(st - m_new) for st in tiles]
                l_ref[rows, :] = alpha * l_ref[rows, :] + functools.reduce(jnp.add, ps)
                alpha_ref[rows, :] = alpha
            m_ref[rows, :] = m_new
            for t in range(n_tiles):
                p_t = ps[live.index(t)].astype(BF16) if t in live else jnp.zeros((rc, HEAD_DIM), BF16)
                p_ref[rows, t * HEAD_DIM:(t + 1) * HEAD_DIM] = p_t

    def accumulate(j, buf, first):
        pv = jnp.dot(p_bufs[buf][...], v_ref[0, key_rows(j), :], preferred_element_type=F32)
        if first:
            acc_ref[...] = pv
        else:
            alpha = a_bufs[buf][...]
            for t in range(DIFF_V_DIM // HEAD_DIM):
                cols = slice(t * HEAD_DIM, (t + 1) * HEAD_DIM)
                acc_ref[:, cols] = acc_ref[:, cols] * alpha + pv[:, cols]

    def query_block(qi):
        s_next = scores(0)
        for j in range(qi + 1):
            s = s_next
            if j < qi:
                s_next = scores(j + 1)
            if j >= 1:
                accumulate(j - 1, (j - 1) % n_buf, first=(j == 1))
            softmax(s, j % n_buf, diagonal=(j == qi), first=(j == 0))
        accumulate(qi, qi % n_buf, first=(qi == 0))

        lam = (jnp.exp(jnp.sum(lq1_ref[...] * lk1_ref[...], axis=-1, keepdims=True))
               - jnp.exp(jnp.sum(lq2_ref[...] * lk2_ref[...], axis=-1, keepdims=True)) + lam_init)
        o = acc_ref[...] * (1.0 / jnp.sum(l_ref[...], axis=-1, keepdims=True))
        o = o[:tq] - lam * o[tq:]
        o = _group_rms(o) * (1.0 - lam_init) * gain_ref[0]
        o_ref[0] = o.astype(o_ref.dtype)

    for qi in range(n_q):
        pl.when(pl.program_id(2) == qi)(functools.partial(query_block, qi))


def _diff_attention(z3, lq1, lk1, lq2, lk2, gain_b, lam_init, cast_weights):
    b, s, _ = z3.shape
    tq = ATTN_TQ
    assert s % tq == 0
    cast_in_specs, cast_out_specs, cast_out_shapes = _cast_specs(
        cast_weights, b * N_GROUPS * (s // tq), lambda bi, h, i: (bi * N_GROUPS + h) * (s // tq) + i, 3)
    n_buf = s // tq
    qb, kb, vb = Q_OFF // DIFF_V_DIM, K_OFF // DIFF_V_DIM, V_OFF // DIFF_V_DIM
    vec = pl.BlockSpec((1, HEAD_DIM), lambda bi, h, i: (0, 0))
    outs = pl.pallas_call(
        functools.partial(_attn_kernel, tq=tq, n_q=s // tq, n_buf=n_buf, n_cast=len(cast_weights),
                          lam_init=lam_init),
        grid=(b, N_GROUPS, s // tq),
        in_specs=[
            vec, vec, vec, vec,
            pl.BlockSpec((1, tq, DIFF_V_DIM), lambda bi, h, i: (bi, i, qb + h)),
            pl.BlockSpec((1, s, DIFF_V_DIM), lambda bi, h, i: (bi, 0, kb + h)),
            pl.BlockSpec((1, s, DIFF_V_DIM), lambda bi, h, i: (bi, 0, vb + h)),
            pl.BlockSpec((1, 1, DIFF_V_DIM), lambda bi, h, i: (h, 0, 0)),
        ] + cast_in_specs,
        out_specs=[pl.BlockSpec((1, tq, DIFF_V_DIM), lambda bi, h, i: (bi, i, h))] + cast_out_specs,
        out_shape=[jax.ShapeDtypeStruct((b, s, DIFF_WIDTH), BF16)] + cast_out_shapes,
        scratch_shapes=[
            pltpu.VMEM((2 * tq, HEAD_DIM), F32),
            pltpu.VMEM((2 * tq, HEAD_DIM), F32),
            pltpu.VMEM((2 * tq, DIFF_V_DIM), F32),
        ] + [pltpu.VMEM((2 * tq, tq), BF16)] * n_buf
          + [pltpu.VMEM((2 * tq, HEAD_DIM), F32)] * n_buf,
        compiler_params=_params("arbitrary", "arbitrary", "arbitrary"),
        name="diff_attention",
    )(lq1, lk1, lq2, lk2, z3, z3, z3, gain_b, *[w for w, _ in cast_weights])
    return outs[0], outs[1:]


def _ffn_kernel(x_ref, g_ref, wg_ref, wu_ref, wd_ref, fg_ref, o_ref, h_ref, *, final_norm):
    f = pl.program_id(1)
    last_f = pl.num_programs(1) - 1

    def step(first, last):
        if first:
            h_ref[...] = _rms_rows(x_ref[...], g_ref[...]).astype(BF16)
        h = h_ref[...]
        gate = jnp.dot(h, wg_ref[...], preferred_element_type=F32)
        up = jnp.dot(h, wu_ref[...], preferred_element_type=F32)
        act = (gate * jax.nn.sigmoid(gate) * up).astype(BF16)
        out = (x_ref[...] if first else o_ref[...]) + jnp.dot(act, wd_ref[...], preferred_element_type=F32)
        if last and final_norm:
            out = _rms_rows(out, fg_ref[...])
        o_ref[...] = out

    pl.when(f == 0)(functools.partial(step, True, False))
    if final_norm:
        pl.when((f > 0) & (f < last_f))(functools.partial(step, False, False))
        pl.when(f == last_f)(functools.partial(step, False, True))
    else:
        pl.when(f > 0)(functools.partial(step, False, False))


def _ffn(x2, gain, wg, wu, wd, layer, final_gain, final_norm):
    m, d = x2.shape
    dff = wg.shape[2]
    tm, tf = FFN_TM, FFN_TF
    assert m % tm == 0 and dff % tf == 0
    return pl.pallas_call(
        functools.partial(_ffn_kernel, final_norm=final_norm),
        grid=(m // tm, dff // tf),
        in_specs=[
            pl.BlockSpec((tm, d), lambda i, f: (i, 0)),
            pl.BlockSpec((1, d), lambda i, f: (0, 0)),
            pl.BlockSpec((None, d, tf), lambda i, f: (layer, 0, f)),
            pl.BlockSpec((None, d, tf), lambda i, f: (layer, 0, f)),
            pl.BlockSpec((None, tf, d), lambda i, f: (layer, f, 0)),
            pl.BlockSpec((1, d), lambda i, f: (0, 0)),
        ],
        out_specs=pl.BlockSpec((tm, d), lambda i, f: (i, 0)),
        out_shape=jax.ShapeDtypeStruct((m, d), F32),
        scratch_shapes=[pltpu.VMEM((tm, d), BF16)],
        compiler_params=_params("parallel", "arbitrary"),
        name="ffn",
    )(x2, gain, wg, wu, wd, final_gain)


def kernel(x, positions, attn_norm, w_in, gmlp_ln_g, gmlp_ln_b, gmlp_ws, gmlp_bs, lambda_q1, lambda_k1,
           lambda_q2, lambda_k2, conv_w, mix_norm, w_out, ffn_norm, w_gate, w_up, w_down, final_norm):
    b, s, d = x.shape
    m = b * s
    depth = w_in.shape[0]

    inv_freq = 1.0 / (ROPE_THETA ** (jnp.arange(0, HEAD_DIM, 2, dtype=F32) / HEAD_DIM))
    ang = positions.astype(F32).reshape(m, 1) * jnp.concatenate([inv_freq, inv_freq]).reshape(1, HEAD_DIM)
    sign = jnp.concatenate([-jnp.ones((HEAD_DIM // 2,), F32), jnp.ones((HEAD_DIM // 2,), F32)]).reshape(1, HEAD_DIM)

    w_in_l, w_out_l = w_in[0].astype(BF16), w_out[0].astype(BF16)
    x2 = x.reshape(m, d)
    for l in range(depth):
        lam_init = 0.8 - 0.6 * math.exp(-0.3 * l)
        gain = mix_norm[l].astype(F32)
        z = _in_proj(x2, attn_norm[l].reshape(1, d), w_in_l[None], 0, ang, sign)
        z3 = z.reshape(b, s, z.shape[1])
        mix_b, casts = _diff_attention(
            z3, lambda_q1[l].reshape(1, HEAD_DIM), lambda_k1[l].reshape(1, HEAD_DIM),
            lambda_q2[l].reshape(1, HEAD_DIM), lambda_k2[l].reshape(1, HEAD_DIM),
            gain[GMLP_WIDTH:GMLP_WIDTH + DIFF_WIDTH].reshape(N_GROUPS, 1, DIFF_V_DIM), lam_init,
            [(w_in, l + 1), (w_out, l + 1)] if l + 1 < depth else [])
        x2, (w_gate_l, w_up_l, w_down_l) = _mix_out_proj(
            x2, z, mix_b.reshape(m, DIFF_WIDTH), gmlp_ln_g[l], gmlp_ln_b[l], gmlp_ws[l],
            jnp.tile(gmlp_bs[l], (1, OUT_PROJ_TM // GMLP_CHUNK)).reshape(N_GROUPS, OUT_PROJ_TM, 1), conv_w[l],
            gain[:GMLP_WIDTH].reshape(1, GMLP_WIDTH), gain[GMLP_WIDTH + DIFF_WIDTH:].reshape(1, CONV_WIDTH),
            w_out_l[None], 0, s, [(w_gate, l), (w_up, l), (w_down, l)])
        x2 = _ffn(x2, ffn_norm[l].reshape(1, d), w_gate_l[None], w_up_l[None], w_down_l[None], 0,
                  final_norm.reshape(1, d), final_norm=(l == depth - 1))
        if l + 1 < depth:
            w_in_l, w_out_l = casts
    return x2.reshape(b, s, d)
```

```python
import functools
import math

import jax
import jax.numpy as jnp
from jax import lax
from jax.experimental import pallas as pl
from jax.experimental.pallas import tpu as pltpu

F32 = jnp.float32
BF16 = jnp.bfloat16

HEAD_DIM = 128
N_GROUPS = 4
GMLP_WIDTH = N_GROUPS * HEAD_DIM
GMLP_CHUNK = 128
DIFF_QK_WIDTH = N_GROUPS * 2 * HEAD_DIM
DIFF_V_DIM = 2 * HEAD_DIM
DIFF_WIDTH = N_GROUPS * DIFF_V_DIM
CONV_WIDTH = N_GROUPS * HEAD_DIM
CONV_K = 3
ROPE_THETA = 10000.0
RMS_EPS = 1e-6
LN_EPS = 1e-5
LOG2_E = math.log2(math.e)

Q_OFF = 2 * GMLP_WIDTH
K_OFF = Q_OFF + DIFF_QK_WIDTH
V_OFF = K_OFF + DIFF_QK_WIDTH
C_OFF = V_OFF + DIFF_WIDTH

VMEM_LIMIT_BYTES = 56 * 1024 * 1024

IN_PROJ_TM = 512
IN_PROJ_TN = 2816
ATTN_TQ = 512
CAST_SLAB_ROWS = 16
ATTN_ROW_CHUNK = 32
OUT_PROJ_TM = 512
FFN_TM = 1024
FFN_TF = 512
CONV_HALO_ROWS = 8


def _params(*semantics):
    return pltpu.CompilerParams(dimension_semantics=semantics, vmem_limit_bytes=VMEM_LIMIT_BYTES)


def _rms_rows(x, gain):
    return x * lax.rsqrt(jnp.mean(x * x, axis=-1, keepdims=True) + RMS_EPS) * gain


def _group_rms(x):
    return x * lax.rsqrt(jnp.mean(x * x, axis=-1, keepdims=True) + RMS_EPS)


def _gelu(x):
    return 0.5 * x * (1.0 + lax.erf(x * math.sqrt(0.5)))


def _cast_specs(cast_weights, n_steps, step_of, n_grid_axes):
    in_specs, out_specs, out_shapes = [], [], []
    for w, layer in cast_weights:
        _, r, c = w.shape
        slab = next(k for k in range(CAST_SLAB_ROWS, r + 1, CAST_SLAB_ROWS) if r % k == 0 and r // k <= n_steps)
        blk = lambda *idx, last=r // slab - 1: jnp.minimum(step_of(*idx[:n_grid_axes]), last)
        in_specs.append(pl.BlockSpec((None, slab, c), lambda *idx, layer=layer, blk=blk: (layer, blk(*idx), 0)))
        out_specs.append(pl.BlockSpec((slab, c), lambda *idx, blk=blk: (blk(*idx), 0)))
        out_shapes.append(jax.ShapeDtypeStruct((r, c), BF16))
    return in_specs, out_specs, out_shapes


def _cast_slabs(src_refs, dst_refs):
    for src_ref, dst_ref in zip(src_refs, dst_refs):
        dst_ref[...] = src_ref[...].astype(dst_ref.dtype)


def _in_proj_kernel(x_ref, g_ref, w_ref, ang_ref, sign_ref, z_ref, h_ref, cos_ref, sin_ref, *, tn, n_col_tiles):
    j = pl.program_id(1)

    def tile(col0):
        first = col0 == (n_col_tiles - 1) * tn
        if first:
            h_ref[...] = _rms_rows(x_ref[...], g_ref[...]).astype(BF16)
        acc = jnp.dot(h_ref[...], w_ref[...], preferred_element_type=F32)
        if first:
            cos_ref[...] = jnp.cos(ang_ref[...])
            sin_ref[...] = jnp.sin(ang_ref[...]) * sign_ref[...]
        cosf, sinf = cos_ref[...], sin_ref[...]
        for c in range(tn // HEAD_DIM):
            col = col0 + c * HEAD_DIM
            r = acc[:, c * HEAD_DIM:(c + 1) * HEAD_DIM]
            if col < Q_OFF:
                r = _gelu(r)
            elif col < V_OFF:
                r = r * cosf + pltpu.roll(r, HEAD_DIM // 2, 1) * sinf
                if col < K_OFF:
                    r = r * (LOG2_E / math.sqrt(HEAD_DIM))
            z_ref[:, c * HEAD_DIM:(c + 1) * HEAD_DIM] = r.astype(z_ref.dtype)

    for jt in range(n_col_tiles):
        pl.when(j == n_col_tiles - 1 - jt)(functools.partial(tile, jt * tn))


def _in_proj(x2, gain, w, layer, ang, sign):
    m, d = x2.shape
    n = w.shape[2]
    tm, tn = IN_PROJ_TM, IN_PROJ_TN
    assert m % tm == 0 and n % tn == 0 and tn % HEAD_DIM == 0
    return pl.pallas_call(
        functools.partial(_in_proj_kernel, tn=tn, n_col_tiles=n // tn),
        grid=(m // tm, n // tn),
        in_specs=[
            pl.BlockSpec((tm, d), lambda i, j: (i, 0)),
            pl.BlockSpec((1, d), lambda i, j: (0, 0)),
            pl.BlockSpec((None, d, tn), lambda i, j: (layer, 0, n // tn - 1 - j)),
            pl.BlockSpec((tm, HEAD_DIM), lambda i, j: (i, 0)),
            pl.BlockSpec((1, HEAD_DIM), lambda i, j: (0, 0)),
        ],
        out_specs=pl.BlockSpec((tm, tn), lambda i, j: (i, n // tn - 1 - j)),
        out_shape=jax.ShapeDtypeStruct((m, n), BF16),
        scratch_shapes=[pltpu.VMEM((tm, d), BF16),
                        pltpu.VMEM((tm, HEAD_DIM), F32),
                        pltpu.VMEM((tm, HEAD_DIM), F32)],
        compiler_params=_params("parallel", "arbitrary"),
        name="in_proj",
    )(x2, gain, w, ang, sign)


def _gmlp_mixer(za_ref, lng_ref, lnb_ref, ws_ref, bs_ref, ga_ref, vn_ref, mixed_ref, out_ref, rows):
    groups = [slice(g * HEAD_DIM, (g + 1) * HEAD_DIM) for g in range(N_GROUPS)]
    for g, cols in enumerate(groups):
        v = za_ref[:, GMLP_WIDTH + cols.start:GMLP_WIDTH + cols.stop].astype(F32)
        mu = jnp.mean(v, axis=-1, keepdims=True)
        vc = v - mu
        var = jnp.mean(vc * vc, axis=-1, keepdims=True)
        vn = vc * lax.rsqrt(var + LN_EPS) * lng_ref[g:g + 1, :] + lnb_ref[g:g + 1, :]
        vn_ref[:, cols] = vn.astype(BF16)
    t_idx = lax.broadcasted_iota(jnp.int32, (GMLP_CHUNK, GMLP_CHUNK), 0)
    s_idx = lax.broadcasted_iota(jnp.int32, (GMLP_CHUNK, GMLP_CHUNK), 1)
    for g, cols in enumerate(groups):
        w = jnp.where(s_idx <= t_idx, ws_ref[g], 0.0).astype(BF16)
        for c in range(rows // GMLP_CHUNK):
            chunk = slice(c * GMLP_CHUNK, (c + 1) * GMLP_CHUNK)
            mixed_ref[chunk, cols] = jnp.dot(w, vn_ref[chunk, cols], preferred_element_type=F32)
    for g, cols in enumerate(groups):
        u = za_ref[:, cols].astype(F32)
        ya = _group_rms(u * (mixed_ref[:, cols] + bs_ref[g])) * ga_ref[:, cols]
        out_ref[:, cols] = ya.astype(out_ref.dtype)


def _short_conv_mixer(bg_ref, cg_ref, hc_ref, cgh_ref, hch_ref, cw_ref, gc_ref, out_ref, at_seq_start):
    xh = cg_ref[...].astype(F32) * hc_ref[...].astype(F32)
    halo = cgh_ref[...].astype(F32) * hch_ref[...].astype(F32)
    halo = jnp.where(at_seq_start, 0.0, halo)
    row = lax.broadcasted_iota(jnp.int32, xh.shape, 0)
    prev1 = jnp.where(row == 0, halo[CONV_HALO_ROWS - 1:CONV_HALO_ROWS, :], pltpu.roll(xh, 1, 0))
    prev2 = jnp.where(row == 0, halo[CONV_HALO_ROWS - 2:CONV_HALO_ROWS - 1, :],
                      jnp.where(row == 1, halo[CONV_HALO_ROWS - 1:CONV_HALO_ROWS, :], pltpu.roll(xh, 2, 0)))
    y = cw_ref[0:1, :] * prev2 + cw_ref[1:2, :] * prev1 + cw_ref[2:3, :] * xh
    yc = bg_ref[...].astype(F32) * y
    for g in range(N_GROUPS):
        cols = slice(g * HEAD_DIM, (g + 1) * HEAD_DIM)
        out_ref[:, cols] = (_group_rms(yc[:, cols]) * gc_ref[:, cols]).astype(out_ref.dtype)


def _mix_out_kernel(x_ref, za_ref, bg_ref, cg_ref, hc_ref, cgh_ref, hch_ref, mb_ref, lng_ref, lnb_ref, ws_ref,
                    bs_ref, cw_ref, ga_ref, gc_ref, w0_ref, w1_ref, w2_ref, w3_ref, *rest, rows, blocks_per_seq,
                    n_cast):
    cast_src, (o_ref, *cast_dst) = rest[:n_cast], rest[n_cast:2 * n_cast + 1]
    vn_ref, mixed_ref, ma_ref, mc_ref = rest[2 * n_cast + 1:]
    _cast_slabs(cast_src, cast_dst)
    o_ref[...] = (x_ref[...] + jnp.dot(mb_ref[:, :GMLP_WIDTH], w1_ref[...], preferred_element_type=F32)
                  + jnp.dot(mb_ref[:, GMLP_WIDTH:], w2_ref[...], preferred_element_type=F32))
    _gmlp_mixer(za_ref, lng_ref, lnb_ref, ws_ref, bs_ref, ga_ref, vn_ref, mixed_ref, ma_ref, rows)
    _short_conv_mixer(bg_ref, cg_ref, hc_ref, cgh_ref, hch_ref, cw_ref, gc_ref, mc_ref,
                      pl.program_id(0) % blocks_per_seq == 0)
    o_ref[...] += (jnp.dot(ma_ref[...], w0_ref[...], preferred_element_type=F32)
                   + jnp.dot(mc_ref[...], w3_ref[...], preferred_element_type=F32))


def _mix_out_proj(x2, z, mix_b, ln_g, ln_b, ws, bs, conv_w, gain_a, gain_c, w, layer, seq_len, cast_weights):
    m, d = x2.shape
    rows = OUT_PROJ_TM
    assert m % rows == 0 and seq_len % rows == 0 and rows % GMLP_CHUNK == 0 and C_OFF % CONV_WIDTH == 0
    assert DIFF_WIDTH == 2 * GMLP_WIDTH and GMLP_WIDTH == CONV_WIDTH
    cb = C_OFF // CONV_WIDTH
    hb = rows // CONV_HALO_ROWS
    full = lambda shape: pl.BlockSpec(shape, lambda i: (0,) * len(shape))
    zcol = lambda width, col: pl.BlockSpec((rows, width), lambda i: (i, col))
    halo = lambda col: pl.BlockSpec((CONV_HALO_ROWS, CONV_WIDTH), lambda i: (jnp.maximum(i * hb - 1, 0), col))
    wrows = lambda blk: pl.BlockSpec((None, GMLP_WIDTH, d), lambda i: (layer, blk, 0))
    cast_in, cast_out, cast_shapes = _cast_specs(cast_weights, m // rows, lambda i: i, 1)
    outs = pl.pallas_call(
        functools.partial(_mix_out_kernel, rows=rows, blocks_per_seq=seq_len // rows, n_cast=len(cast_weights)),
        grid=(m // rows,),
        in_specs=[
            pl.BlockSpec((rows, d), lambda i: (i, 0)),
            zcol(2 * GMLP_WIDTH, 0), zcol(CONV_WIDTH, cb), zcol(CONV_WIDTH, cb + 1), zcol(CONV_WIDTH, cb + 2),
            halo(cb + 1), halo(cb + 2),
            pl.BlockSpec((rows, DIFF_WIDTH), lambda i: (i, 0)),
            full((N_GROUPS, HEAD_DIM)), full((N_GROUPS, HEAD_DIM)),
            full((N_GROUPS, GMLP_CHUNK, GMLP_CHUNK)), full((N_GROUPS, rows, 1)),
            full((CONV_K, CONV_WIDTH)), full((1, GMLP_WIDTH)), full((1, CONV_WIDTH)),
            wrows(0), wrows(1), wrows(2), wrows(3),
        ] + cast_in,
        out_specs=[pl.BlockSpec((rows, d), lambda i: (i, 0))] + cast_out,
        out_shape=[jax.ShapeDtypeStruct((m, d), F32)] + cast_shapes,
        scratch_shapes=[pltpu.VMEM((rows, GMLP_WIDTH), BF16),
                        pltpu.VMEM((rows, GMLP_WIDTH), F32),
                        pltpu.VMEM((rows, GMLP_WIDTH), BF16),
                        pltpu.VMEM((rows, CONV_WIDTH), BF16)],
        compiler_params=_params("arbitrary"),
        name="mix_out_proj",
    )(x2, z, z, z, z, z, z, mix_b, ln_g, ln_b, ws, bs, conv_w, gain_a, gain_c, w, w, w, w, *[cw for cw, _ in cast_weights])
    return outs[0], outs[1:]


def _attn_kernel(lq1_ref, lk1_ref, lq2_ref, lk2_ref, q_ref, k_ref, v_ref, gain_ref, *rest, tq, n_q, n_buf, n_cast,
                 lam_init):
    cast_src, (o_ref, *cast_dst) = rest[:n_cast], rest[n_cast:2 * n_cast + 1]
    m_ref, l_ref, acc_ref, *bufs = rest[2 * n_cast + 1:]
    _cast_slabs(cast_src, cast_dst)

    rc = ATTN_ROW_CHUNK
    n_tiles = tq // HEAD_DIM
    p_bufs, a_bufs = bufs[:n_buf], bufs[n_buf:]

    def key_rows(j):
        return slice(j * tq, (j + 1) * tq)

    def scores(j):
        return [lax.dot_general(q_ref[0, :, mp * HEAD_DIM:(mp + 1) * HEAD_DIM],
                                k_ref[0, key_rows(j), mp * HEAD_DIM:(mp + 1) * HEAD_DIM],
                                (((1,), (1,)), ((), ())), preferred_element_type=F32) for mp in range(2)]

    def softmax(s_maps, buf, diagonal, first):
        p_ref, alpha_ref = p_bufs[buf], a_bufs[buf]
        for c in range(2 * tq // rc):
            rows = slice(c * rc, (c + 1) * rc)
            q0 = rows.start % tq
            s = s_maps[rows.start // tq]
            s_rows = slice(q0, q0 + rc)
            live = [t for t in range(n_tiles) if (not diagonal) or t * HEAD_DIM <= q0 + rc - 1]
            tiles = []
            for t in live:
                st = s[s_rows, t * HEAD_DIM:(t + 1) * HEAD_DIM]
                if diagonal and (t + 1) * HEAD_DIM - 1 > q0:
                    row = lax.broadcasted_iota(jnp.int32, (rc, HEAD_DIM), 0) + q0
                    col = lax.broadcasted_iota(jnp.int32, (rc, HEAD_DIM), 1) + t * HEAD_DIM
                    st = jnp.where(col <= row, st, -jnp.inf)
                tiles.append(st)
            m_blk = jnp.max(functools.reduce(jnp.maximum, tiles), axis=-1, keepdims=True)
            if first:
                m_new = jnp.broadcast_to(m_blk, (rc, HEAD_DIM))
                ps = [jnp.exp2(st - m_new) for st in tiles]
                l_ref[rows, :] = functools.reduce(jnp.add, ps)
            else:
                m_old = m_ref[rows, :]
                m_new = jnp.maximum(m_old, m_blk)
                alpha = jnp.exp2(m_old - m_new)
                ps = [jnp.exp2(st - m_new) for st in tiles]
                l_ref[rows, :] = alpha * l_ref[rows, :] + functools.reduce(jnp.add, ps)
                alpha_ref[rows, :] = alpha
            m_ref[rows, :] = m_new
            for t in range(n_tiles):
                p_t = ps[live.index(t)].astype(BF16) if t in live else jnp.zeros((rc, HEAD_DIM), BF16)
                p_ref[rows, t * HEAD_DIM:(t + 1) * HEAD_DIM] = p_t

    def accumulate(j, buf, first):
        pv = jnp.dot(p_bufs[buf][...], v_ref[0, key_rows(j), :], preferred_element_type=F32)
        if first:
            acc_ref[...] = pv
        else:
            alpha = a_bufs[buf][...]
            for t in range(DIFF_V_DIM // HEAD_DIM):
                cols = slice(t * HEAD_DIM, (t + 1) * HEAD_DIM)
                acc_ref[:, cols] = acc_ref[:, cols] * alpha + pv[:, cols]

    def query_block(qi):
        s_next = scores(0)
        for j in range(qi + 1):
            s = s_next
            if j < qi:
                s_next = scores(j + 1)
            if j >= 1:
                accumulate(j - 1, (j - 1) % n_buf, first=(j == 1))
            softmax(s, j % n_buf, diagonal=(j == qi), first=(j == 0))
        accumulate(qi, qi % n_buf, first=(qi == 0))

        lam = (jnp.exp(jnp.sum(lq1_ref[...] * lk1_ref[...], axis=-1, keepdims=True))
               - jnp.exp(jnp.sum(lq2_ref[...] * lk2_ref[...], axis=-1, keepdims=True)) + lam_init)
        o = acc_ref[...] * (1.0 / jnp.sum(l_ref[...], axis=-1, keepdims=True))
        o = o[:tq] - lam * o[tq:]
        o = _group_rms(o) * (1.0 - lam_init) * gain_ref[0]
        o_ref[0] = o.astype(o_ref.dtype)

    for qi in range(n_q):
        pl.when(pl.program_id(2) == qi)(functools.partial(query_block, qi))


def _diff_attention(z3, lq1, lk1, lq2, lk2, gain_b, lam_init, cast_weights):
    b, s, _ = z3.shape
    tq = ATTN_TQ
    assert s % tq == 0
    cast_in_specs, cast_out_specs, cast_out_shapes = _cast_specs(
        cast_weights, b * N_GROUPS * (s // tq), lambda bi, h, i: (bi * N_GROUPS + h) * (s // tq) + i, 3)
    n_buf = s // tq
    qb, kb, vb = Q_OFF // DIFF_V_DIM, K_OFF // DIFF_V_DIM, V_OFF // DIFF_V_DIM
    vec = pl.BlockSpec((1, HEAD_DIM), lambda bi, h, i: (0, 0))
    outs = pl.pallas_call(
        functools.partial(_attn_kernel, tq=tq, n_q=s // tq, n_buf=n_buf, n_cast=len(cast_weights),
                          lam_init=lam_init),
        grid=(b, N_GROUPS, s // tq),
        in_specs=[
            vec, vec, vec, vec,
            pl.BlockSpec((1, tq, DIFF_V_DIM), lambda bi, h, i: (bi, i, qb + h)),
            pl.BlockSpec((1, s, DIFF_V_DIM), lambda bi, h, i: (bi, 0, kb + h)),
            pl.BlockSpec((1, s, DIFF_V_DIM), lambda bi, h, i: (bi, 0, vb + h)),
            pl.BlockSpec((1, 1, DIFF_V_DIM), lambda bi, h, i: (h, 0, 0)),
        ] + cast_in_specs,
        out_specs=[pl.BlockSpec((1, tq, DIFF_V_DIM), lambda bi, h, i: (bi, i, h))] + cast_out_specs,
        out_shape=[jax.ShapeDtypeStruct((b, s, DIFF_WIDTH), BF16)] + cast_out_shapes,
        scratch_shapes=[
            pltpu.VMEM((2 * tq, HEAD_DIM), F32),
            pltpu.VMEM((2 * tq, HEAD_DIM), F32),
            pltpu.VMEM((2 * tq, DIFF_V_DIM), F32),
        ] + [pltpu.VMEM((2 * tq, tq), BF16)] * n_buf
          + [pltpu.VMEM((2 * tq, HEAD_DIM), F32)] * n_buf,
        compiler_params=_params("arbitrary", "arbitrary", "arbitrary"),
        name="diff_attention",
    )(lq1, lk1, lq2, lk2, z3, z3, z3, gain_b, *[w for w, _ in cast_weights])
    return outs[0], outs[1:]


def _ffn_kernel(x_ref, g_ref, wg_ref, wu_ref, wd_ref, fg_ref, o_ref, h_ref, *, final_norm):
    f = pl.program_id(1)
    last_f = pl.num_programs(1) - 1

    def step(first, last):
        if first:
            h_ref[...] = _rms_rows(x_ref[...], g_ref[...]).astype(BF16)
        h = h_ref[...]
        gate = jnp.dot(h, wg_ref[...], preferred_element_type=F32)
        up = jnp.dot(h, wu_ref[...], preferred_element_type=F32)
        act = (gate * jax.nn.sigmoid(gate) * up).astype(BF16)
        out = (x_ref[...] if first else o_ref[...]) + jnp.dot(act, wd_ref[...], preferred_element_type=F32)
        if last and final_norm:
            out = _rms_rows(out, fg_ref[...])
        o_ref[...] = out

    pl.when(f == 0)(functools.partial(step, True, False))
    if final_norm:
        pl.when((f > 0) & (f < last_f))(functools.partial(step, False, False))
        pl.when(f == last_f)(functools.partial(step, False, True))
    else:
        pl.when(f > 0)(functools.partial(step, False, False))


def _ffn(x2, gain, wg, wu, wd, layer, final_gain, final_norm):
    m, d = x2.shape
    dff = wg.shape[2]
    tm, tf = FFN_TM, FFN_TF
    assert m % tm == 0 and dff % tf == 0
    return pl.pallas_call(
        functools.partial(_ffn_kernel, final_norm=final_norm),
        grid=(m // tm, dff // tf),
        in_specs=[
            pl.BlockSpec((tm, d), lambda i, f: (i, 0)),
            pl.BlockSpec((1, d), lambda i, f: (0, 0)),
            pl.BlockSpec((None, d, tf), lambda i, f: (layer, 0, f)),
            pl.BlockSpec((None, d, tf), lambda i, f: (layer, 0, f)),
            pl.BlockSpec((None, tf, d), lambda i, f: (layer, f, 0)),
            pl.BlockSpec((1, d), lambda i, f: (0, 0)),
        ],
        out_specs=pl.BlockSpec((tm, d), lambda i, f: (i, 0)),
        out_shape=jax.ShapeDtypeStruct((m, d), F32),
        scratch_shapes=[pltpu.VMEM((tm, d), BF16)],
        compiler_params=_params("parallel", "arbitrary"),
        name="ffn",
    )(x2, gain, wg, wu, wd, final_gain)


def kernel(x, positions, attn_norm, w_in, gmlp_ln_g, gmlp_ln_b, gmlp_ws, gmlp_bs, lambda_q1, lambda_k1,
           lambda_q2, lambda_k2, conv_w, mix_norm, w_out, ffn_norm, w_gate, w_up, w_down, final_norm):
    b, s, d = x.shape
    m = b * s
    depth = w_in.shape[0]

    inv_freq = 1.0 / (ROPE_THETA ** (jnp.arange(0, HEAD_DIM, 2, dtype=F32) / HEAD_DIM))
    ang = positions.astype(F32).reshape(m, 1) * jnp.concatenate([inv_freq, inv_freq]).reshape(1, HEAD_DIM)
    sign = jnp.concatenate([-jnp.ones((HEAD_DIM // 2,), F32), jnp.ones((HEAD_DIM // 2,), F32)]).reshape(1, HEAD_DIM)

    w_in_l, w_out_l = w_in[0].astype(BF16), w_out[0].astype(BF16)
    x2 = x.reshape(m, d)
    for l in range(depth):
        lam_init = 0.8 - 0.6 * math.exp(-0.3 * l)
        gain = mix_norm[l].astype(F32)
        z = _in_proj(x2, attn_norm[l].reshape(1, d), w_in_l[None], 0, ang, sign)
        z3 = z.reshape(b, s, z.shape[1])
        mix_b, casts = _diff_attention(
            z3, lambda_q1[l].reshape(1, HEAD_DIM), lambda_k1[l].reshape(1, HEAD_DIM),
            lambda_q2[l].reshape(1, HEAD_DIM), lambda_k2[l].reshape(1, HEAD_DIM),
            gain[GMLP_WIDTH:GMLP_WIDTH + DIFF_WIDTH].reshape(N_GROUPS, 1, DIFF_V_DIM), lam_init,
            [(w_in, l + 1), (w_out, l + 1)] if l + 1 < depth else [])
        x2, (w_gate_l, w_up_l, w_down_l) = _mix_out_proj(
            x2, z, mix_b.reshape(m, DIFF_WIDTH), gmlp_ln_g[l], gmlp_ln_b[l], gmlp_ws[l],
            jnp.tile(gmlp_bs[l], (1, OUT_PROJ_TM // GMLP_CHUNK)).reshape(N_GROUPS, OUT_PROJ_TM, 1), conv_w[l],
            gain[:GMLP_WIDTH].reshape(1, GMLP_WIDTH), gain[GMLP_WIDTH + DIFF_WIDTH:].reshape(1, CONV_WIDTH),
            w_out_l[None], 0, s, [(w_gate, l), (w_up, l), (w_down, l)])
        x2 = _ffn(x2, ffn_norm[l].reshape(1, d), w_gate_l[None], w_up_l[None], w_down_l[None], 0,
                  final_norm.reshape(1, d), final_norm=(l == depth - 1))
        if l + 1 < depth:
            w_in_l, w_out_l = casts
    return x2.reshape(b, s, d)
```

```python
import functools
import math

import jax
import jax.numpy as jnp
from jax import lax
from jax.experimental import pallas as pl
from jax.experimental.pallas import tpu as pltpu

F32 = jnp.float32
BF16 = jnp.bfloat16

HEAD_DIM = 128
N_GROUPS = 4
GMLP_WIDTH = N_GROUPS * HEAD_DIM
GMLP_CHUNK = 128
DIFF_QK_WIDTH = N_GROUPS * 2 * HEAD_DIM
DIFF_V_DIM = 2 * HEAD_DIM
DIFF_WIDTH = N_GROUPS * DIFF_V_DIM
CONV_WIDTH = N_GROUPS * HEAD_DIM
CONV_K = 3
ROPE_THETA = 10000.0
RMS_EPS = 1e-6
LN_EPS = 1e-5
LOG2_E = math.log2(math.e)

Q_OFF = 2 * GMLP_WIDTH
K_OFF = Q_OFF + DIFF_QK_WIDTH
V_OFF = K_OFF + DIFF_QK_WIDTH
C_OFF = V_OFF + DIFF_WIDTH

VMEM_LIMIT_BYTES = 56 * 1024 * 1024

IN_PROJ_TM = 512
IN_PROJ_TN = 2816
ATTN_TQ = 512
CAST_SLAB_ROWS = 16
ATTN_ROW_CHUNK = 32
OUT_PROJ_TM = 512
FFN_TM = 1024
FFN_TF = 512
FFN_GU_CHUNK = 256
CONV_HALO_ROWS = 8


def _params(*semantics):
    return pltpu.CompilerParams(dimension_semantics=semantics, vmem_limit_bytes=VMEM_LIMIT_BYTES)


def _rms_rows(x, gain):
    return x * lax.rsqrt(jnp.mean(x * x, axis=-1, keepdims=True) + RMS_EPS) * gain


def _group_rms(x):
    return x * lax.rsqrt(jnp.mean(x * x, axis=-1, keepdims=True) + RMS_EPS)


def _gelu(x):
    return 0.5 * x * (1.0 + lax.erf(x * math.sqrt(0.5)))


def _cast_specs(cast_weights, n_steps, step_of, n_grid_axes):
    in_specs, out_specs, out_shapes = [], [], []
    for w, layer in cast_weights:
        _, r, c = w.shape
        slab = next(k for k in range(CAST_SLAB_ROWS, r + 1, CAST_SLAB_ROWS) if r % k == 0 and r // k <= n_steps)
        blk = lambda *idx, last=r // slab - 1: jnp.minimum(step_of(*idx[:n_grid_axes]), last)
        in_specs.append(pl.BlockSpec((None, slab, c), lambda *idx, layer=layer, blk=blk: (layer, blk(*idx), 0)))
        out_specs.append(pl.BlockSpec((slab, c), lambda *idx, blk=blk: (blk(*idx), 0)))
        out_shapes.append(jax.ShapeDtypeStruct((r, c), BF16))
    return in_specs, out_specs, out_shapes


def _interleaved_pair_specs(w_a, w_b, layer, n_steps, step_of, n_grid_axes):
    (in_a,), (out_a,), _ = _cast_specs([(w_a, layer)], n_steps, step_of, n_grid_axes)
    (in_b,), _, _ = _cast_specs([(w_b, layer)], n_steps, step_of, n_grid_axes)
    _, r, c = w_a.shape
    slab = out_a.block_shape[0]
    out = pl.BlockSpec((slab, 2 * c), out_a.index_map)
    return [in_a, in_b], out, jax.ShapeDtypeStruct((r, 2 * c), BF16)


def _cast_interleaved(a_ref, b_ref, dst_ref, chunk):
    for k in range(a_ref.shape[1] // chunk):
        dst_ref[:, 2 * k * chunk:(2 * k + 1) * chunk] = a_ref[:, k * chunk:(k + 1) * chunk].astype(dst_ref.dtype)
        dst_ref[:, (2 * k + 1) * chunk:(2 * k + 2) * chunk] = b_ref[:, k * chunk:(k + 1) * chunk].astype(dst_ref.dtype)


def _cast_slabs(src_refs, dst_refs):
    for src_ref, dst_ref in zip(src_refs, dst_refs):
        dst_ref[...] = src_ref[...].astype(dst_ref.dtype)


def _in_proj_kernel(x_ref, g_ref, w_ref, ang_ref, sign_ref, z_ref, h_ref, cos_ref, sin_ref, *, tn, n_col_tiles):
    j = pl.program_id(1)

    def tile(col0):
        first = col0 == (n_col_tiles - 1) * tn
        if first:
            h_ref[...] = _rms_rows(x_ref[...], g_ref[...]).astype(BF16)
        acc = jnp.dot(h_ref[...], w_ref[...], preferred_element_type=F32)
        if first:
            cos_ref[...] = jnp.cos(ang_ref[...])
            sin_ref[...] = jnp.sin(ang_ref[...]) * sign_ref[...]
        cosf, sinf = cos_ref[...], sin_ref[...]
        for c in range(tn // HEAD_DIM):
            col = col0 + c * HEAD_DIM
            r = acc[:, c * HEAD_DIM:(c + 1) * HEAD_DIM]
            if col < Q_OFF:
                r = _gelu(r)
            elif col < V_OFF:
                r = r * cosf + pltpu.roll(r, HEAD_DIM // 2, 1) * sinf
                if col < K_OFF:
                    r = r * (LOG2_E / math.sqrt(HEAD_DIM))
            z_ref[:, c * HEAD_DIM:(c + 1) * HEAD_DIM] = r.astype(z_ref.dtype)

    for jt in range(n_col_tiles):
        pl.when(j == n_col_tiles - 1 - jt)(functools.partial(tile, jt * tn))


def _in_proj(x2, gain, w, layer, ang, sign):
    m, d = x2.shape
    n = w.shape[2]
    tm, tn = IN_PROJ_TM, IN_PROJ_TN
    assert m % tm == 0 and n % tn == 0 and tn % HEAD_DIM == 0
    return pl.pallas_call(
        functools.partial(_in_proj_kernel, tn=tn, n_col_tiles=n // tn),
        grid=(m // tm, n // tn),
        in_specs=[
            pl.BlockSpec((tm, d), lambda i, j: (i, 0)),
            pl.BlockSpec((1, d), lambda i, j: (0, 0)),
            pl.BlockSpec((None, d, tn), lambda i, j: (layer, 0, n // tn - 1 - j)),
            pl.BlockSpec((tm, HEAD_DIM), lambda i, j: (i, 0)),
            pl.BlockSpec((1, HEAD_DIM), lambda i, j: (0, 0)),
        ],
        out_specs=pl.BlockSpec((tm, tn), lambda i, j: (i, n // tn - 1 - j)),
        out_shape=jax.ShapeDtypeStruct((m, n), BF16),
        scratch_shapes=[pltpu.VMEM((tm, d), BF16),
                        pltpu.VMEM((tm, HEAD_DIM), F32),
                        pltpu.VMEM((tm, HEAD_DIM), F32)],
        compiler_params=_params("parallel", "arbitrary"),
        name="in_proj",
    )(x2, gain, w, ang, sign)


def _gmlp_mixer(za_ref, lng_ref, lnb_ref, ws_ref, bs_ref, ga_ref, vn_ref, mixed_ref, out_ref, rows):
    groups = [slice(g * HEAD_DIM, (g + 1) * HEAD_DIM) for g in range(N_GROUPS)]
    for g, cols in enumerate(groups):
        v = za_ref[:, GMLP_WIDTH + cols.start:GMLP_WIDTH + cols.stop].astype(F32)
        mu = jnp.mean(v, axis=-1, keepdims=True)
        vc = v - mu
        var = jnp.mean(vc * vc, axis=-1, keepdims=True)
        vn = vc * lax.rsqrt(var + LN_EPS) * lng_ref[g:g + 1, :] + lnb_ref[g:g + 1, :]
        vn_ref[:, cols] = vn.astype(BF16)
    t_idx = lax.broadcasted_iota(jnp.int32, (GMLP_CHUNK, GMLP_CHUNK), 0)
    s_idx = lax.broadcasted_iota(jnp.int32, (GMLP_CHUNK, GMLP_CHUNK), 1)
    for g, cols in enumerate(groups):
        w = jnp.where(s_idx <= t_idx, ws_ref[g], 0.0).astype(BF16)
        for c in range(rows // GMLP_CHUNK):
            chunk = slice(c * GMLP_CHUNK, (c + 1) * GMLP_CHUNK)
            mixed_ref[chunk, cols] = jnp.dot(w, vn_ref[chunk, cols], preferred_element_type=F32)
    for g, cols in enumerate(groups):
        u = za_ref[:, cols].astype(F32)
        ya = _group_rms(u * (mixed_ref[:, cols] + bs_ref[g])) * ga_ref[:, cols]
        out_ref[:, cols] = ya.astype(out_ref.dtype)


def _short_conv_mixer(bg_ref, cg_ref, hc_ref, cgh_ref, hch_ref, cw_ref, gc_ref, out_ref, at_seq_start):
    xh = cg_ref[...].astype(F32) * hc_ref[...].astype(F32)
    halo = cgh_ref[...].astype(F32) * hch_ref[...].astype(F32)
    halo = jnp.where(at_seq_start, 0.0, halo)
    row = lax.broadcasted_iota(jnp.int32, xh.shape, 0)
    prev1 = jnp.where(row == 0, halo[CONV_HALO_ROWS - 1:CONV_HALO_ROWS, :], pltpu.roll(xh, 1, 0))
    prev2 = jnp.where(row == 0, halo[CONV_HALO_ROWS - 2:CONV_HALO_ROWS - 1, :],
                      jnp.where(row == 1, halo[CONV_HALO_ROWS - 1:CONV_HALO_ROWS, :], pltpu.roll(xh, 2, 0)))
    y = cw_ref[0:1, :] * prev2 + cw_ref[1:2, :] * prev1 + cw_ref[2:3, :] * xh
    yc = bg_ref[...].astype(F32) * y
    for g in range(N_GROUPS):
        cols = slice(g * HEAD_DIM, (g + 1) * HEAD_DIM)
        out_ref[:, cols] = (_group_rms(yc[:, cols]) * gc_ref[:, cols]).astype(out_ref.dtype)


def _mix_out_kernel(x_ref, za_ref, bg_ref, cg_ref, hc_ref, cgh_ref, hch_ref, mb_ref, lng_ref, lnb_ref, ws_ref,
                    bs_ref, cw_ref, ga_ref, gc_ref, w0_ref, w1_ref, w2_ref, w3_ref, *rest, rows, blocks_per_seq,
                    n_cast):
    cast_src, (gate_ref, up_ref, o_ref), cast_dst = rest[:n_cast], rest[n_cast:n_cast + 3], rest[n_cast + 3:2 * n_cast + 3]
    gu_ref, vn_ref, mixed_ref, ma_ref, mc_ref = rest[2 * n_cast + 3:]
    _cast_slabs(cast_src, cast_dst)
    _cast_interleaved(gate_ref, up_ref, gu_ref, FFN_GU_CHUNK)
    o_ref[...] = (x_ref[...] + jnp.dot(mb_ref[:, :GMLP_WIDTH], w1_ref[...], preferred_element_type=F32)
                  + jnp.dot(mb_ref[:, GMLP_WIDTH:], w2_ref[...], preferred_element_type=F32))
    _gmlp_mixer(za_ref, lng_ref, lnb_ref, ws_ref, bs_ref, ga_ref, vn_ref, mixed_ref, ma_ref, rows)
    _short_conv_mixer(bg_ref, cg_ref, hc_ref, cgh_ref, hch_ref, cw_ref, gc_ref, mc_ref,
                      pl.program_id(0) % blocks_per_seq == 0)
    o_ref[...] += (jnp.dot(ma_ref[...], w0_ref[...], preferred_element_type=F32)
                   + jnp.dot(mc_ref[...], w3_ref[...], preferred_element_type=F32))


def _mix_out_proj(x2, z, mix_b, ln_g, ln_b, ws, bs, conv_w, gain_a, gain_c, w, layer, seq_len, cast_weights,
                  gate_up):
    m, d = x2.shape
    rows = OUT_PROJ_TM
    assert m % rows == 0 and seq_len % rows == 0 and rows % GMLP_CHUNK == 0 and C_OFF % CONV_WIDTH == 0
    assert DIFF_WIDTH == 2 * GMLP_WIDTH and GMLP_WIDTH == CONV_WIDTH
    cb = C_OFF // CONV_WIDTH
    hb = rows // CONV_HALO_ROWS
    full = lambda shape: pl.BlockSpec(shape, lambda i: (0,) * len(shape))
    zcol = lambda width, col: pl.BlockSpec((rows, width), lambda i: (i, col))
    halo = lambda col: pl.BlockSpec((CONV_HALO_ROWS, CONV_WIDTH), lambda i: (jnp.maximum(i * hb - 1, 0), col))
    wrows = lambda blk: pl.BlockSpec((None, GMLP_WIDTH, d), lambda i: (layer, blk, 0))
    cast_in, cast_out, cast_shapes = _cast_specs(cast_weights, m // rows, lambda i: i, 1)
    gu_in, gu_out, gu_shape = _interleaved_pair_specs(*gate_up, m // rows, lambda i: i, 1)
    outs = pl.pallas_call(
        functools.partial(_mix_out_kernel, rows=rows, blocks_per_seq=seq_len // rows, n_cast=len(cast_weights)),
        grid=(m // rows,),
        in_specs=[
            pl.BlockSpec((rows, d), lambda i: (i, 0)),
            zcol(2 * GMLP_WIDTH, 0), zcol(CONV_WIDTH, cb), zcol(CONV_WIDTH, cb + 1), zcol(CONV_WIDTH, cb + 2),
            halo(cb + 1), halo(cb + 2),
            pl.BlockSpec((rows, DIFF_WIDTH), lambda i: (i, 0)),
            full((N_GROUPS, HEAD_DIM)), full((N_GROUPS, HEAD_DIM)),
            full((N_GROUPS, GMLP_CHUNK, GMLP_CHUNK)), full((N_GROUPS, rows, 1)),
            full((CONV_K, CONV_WIDTH)), full((1, GMLP_WIDTH)), full((1, CONV_WIDTH)),
            wrows(0), wrows(1), wrows(2), wrows(3),
        ] + cast_in + gu_in,
        out_specs=[pl.BlockSpec((rows, d), lambda i: (i, 0))] + cast_out + [gu_out],
        out_shape=[jax.ShapeDtypeStruct((m, d), F32)] + cast_shapes + [gu_shape],
        scratch_shapes=[pltpu.VMEM((rows, GMLP_WIDTH), BF16),
                        pltpu.VMEM((rows, GMLP_WIDTH), F32),
                        pltpu.VMEM((rows, GMLP_WIDTH), BF16),
                        pltpu.VMEM((rows, CONV_WIDTH), BF16)],
        compiler_params=_params("arbitrary"),
        name="mix_out_proj",
    )(x2, z, z, z, z, z, z, mix_b, ln_g, ln_b, ws, bs, conv_w, gain_a, gain_c, w, w, w, w, *[cw for cw, _ in cast_weights],
      gate_up[0], gate_up[1])
    return outs[0], outs[1:-1], outs[-1]


def _attn_kernel(lq1_ref, lk1_ref, lq2_ref, lk2_ref, q_ref, k_ref, v_ref, gain_ref, *rest, tq, n_q, n_buf, n_cast,
                 lam_init):
    cast_src, (o_ref, *cast_dst) = rest[:n_cast], rest[n_cast:2 * n_cast + 1]
    m_ref, l_ref, acc_ref, *bufs = rest[2 * n_cast + 1:]
    _cast_slabs(cast_src, cast_dst)

    rc = ATTN_ROW_CHUNK
    n_tiles = tq // HEAD_DIM
    p_bufs, a_bufs = bufs[:n_buf], bufs[n_buf:]

    def key_rows(j):
        return slice(j * tq, (j + 1) * tq)

    def scores(j):
        return [lax.dot_general(q_ref[0, :, mp * HEAD_DIM:(mp + 1) * HEAD_DIM],
                                k_ref[0, key_rows(j), mp * HEAD_DIM:(mp + 1) * HEAD_DIM],
                                (((1,), (1,)), ((), ())), preferred_element_type=F32) for mp in range(2)]

    def softmax(s_maps, buf, diagonal, first):
        p_ref, alpha_ref = p_bufs[buf], a_bufs[buf]
        for c in range(2 * tq // rc):
            rows = slice(c * rc, (c + 1) * rc)
            q0 = rows.start % tq
            s = s_maps[rows.start // tq]
            s_rows = slice(q0, q0 + rc)
            live = [t for t in range(n_tiles) if (not diagonal) or t * HEAD_DIM <= q0 + rc - 1]
            tiles = []
            for t in live:
                st = s[s_rows, t * HEAD_DIM:(t + 1) * HEAD_DIM]
                if diagonal and (t + 1) * HEAD_DIM - 1 > q0:
                    row = lax.broadcasted_iota(jnp.int32, (rc, HEAD_DIM), 0) + q0
                    col = lax.broadcasted_iota(jnp.int32, (rc, HEAD_DIM), 1) + t * HEAD_DIM
                    st = jnp.where(col <= row, st, -jnp.inf)
                tiles.append(st)
            m_blk = jnp.max(functools.reduce(jnp.maximum, tiles), axis=-1, keepdims=True)
            if first:
                m_new = jnp.broadcast_to(m_blk, (rc, HEAD_DIM))
                ps = [jnp.exp2(st - m_new) for st in tiles]
                l_ref[rows, :] = functools.reduce(jnp.add, ps)
            else:
                m_old = m_ref[rows, :]
                m_new = jnp.maximum(m_old, m_blk)
                alpha = jnp.exp2(m_old - m_new)
                ps = [jnp.exp2(st - m_new) for st in tiles]
                l_ref[rows, :] = alpha * l_ref[rows, :] + functools.reduce(jnp.add, ps)
                alpha_ref[rows, :] = alpha
            m_ref[rows, :] = m_new
            for t in range(n_tiles):
                p_t = ps[live.index(t)].astype(BF16) if t in live else jnp.zeros((rc, HEAD_DIM), BF16)
                p_ref[rows, t * HEAD_DIM:(t + 1) * HEAD_DIM] = p_t

    def accumulate(j, buf, first):
        pv = jnp.dot(p_bufs[buf][...], v_ref[0, key_rows(j), :], preferred_element_type=F32)
        if first:
            acc_ref[...] = pv
        else:
            alpha = a_bufs[buf][...]
            for t in range(DIFF_V_DIM // HEAD_DIM):
                cols = slice(t * HEAD_DIM, (t + 1) * HEAD_DIM)
                acc_ref[:, cols] = acc_ref[:, cols] * alpha + pv[:, cols]

    def query_block(qi):
        s_next = scores(0)
        for j in range(qi + 1):
            s = s_next
            if j < qi:
                s_next = scores(j + 1)
            if j >= 1:
                accumulate(j - 1, (j - 1) % n_buf, first=(j == 1))
            softmax(s, j % n_buf, diagonal=(j == qi), first=(j == 0))
        accumulate(qi, qi % n_buf, first=(qi == 0))

        lam = (jnp.exp(jnp.sum(lq1_ref[...] * lk1_ref[...], axis=-1, keepdims=True))
               - jnp.exp(jnp.sum(lq2_ref[...] * lk2_ref[...], axis=-1, keepdims=True)) + lam_init)
        o = acc_ref[...] * (1.0 / jnp.sum(l_ref[...], axis=-1, keepdims=True))
        o = o[:tq] - lam * o[tq:]
        o = _group_rms(o) * (1.0 - lam_init) * gain_ref[0]
        o_ref[0] = o.astype(o_ref.dtype)

    for qi in range(n_q):
        pl.when(pl.program_id(2) == qi)(functools.partial(query_block, qi))


def _diff_attention(z3, lq1, lk1, lq2, lk2, gain_b, lam_init, cast_weights):
    b, s, _ = z3.shape
    tq = ATTN_TQ
    assert s % tq == 0
    cast_in_specs, cast_out_specs, cast_out_shapes = _cast_specs(
        cast_weights, b * N_GROUPS * (s // tq), lambda bi, h, i: (bi * N_GROUPS + h) * (s // tq) + i, 3)
    n_buf = s // tq
    qb, kb, vb = Q_OFF // DIFF_V_DIM, K_OFF // DIFF_V_DIM, V_OFF // DIFF_V_DIM
    vec = pl.BlockSpec((1, HEAD_DIM), lambda bi, h, i: (0, 0))
    outs = pl.pallas_call(
        functools.partial(_attn_kernel, tq=tq, n_q=s // tq, n_buf=n_buf, n_cast=len(cast_weights),
                          lam_init=lam_init),
        grid=(b, N_GROUPS, s // tq),
        in_specs=[
            vec, vec, vec, vec,
            pl.BlockSpec((1, tq, DIFF_V_DIM), lambda bi, h, i: (bi, i, qb + h)),
            pl.BlockSpec((1, s, DIFF_V_DIM), lambda bi, h, i: (bi, 0, kb + h)),
            pl.BlockSpec((1, s, DIFF_V_DIM), lambda bi, h, i: (bi, 0, vb + h)),
            pl.BlockSpec((1, 1, DIFF_V_DIM), lambda bi, h, i: (h, 0, 0)),
        ] + cast_in_specs,
        out_specs=[pl.BlockSpec((1, tq, DIFF_V_DIM), lambda bi, h, i: (bi, i, h))] + cast_out_specs,
        out_shape=[jax.ShapeDtypeStruct((b, s, DIFF_WIDTH), BF16)] + cast_out_shapes,
        scratch_shapes=[
            pltpu.VMEM((2 * tq, HEAD_DIM), F32),
            pltpu.VMEM((2 * tq, HEAD_DIM), F32),
            pltpu.VMEM((2 * tq, DIFF_V_DIM), F32),
        ] + [pltpu.VMEM((2 * tq, tq), BF16)] * n_buf
          + [pltpu.VMEM((2 * tq, HEAD_DIM), F32)] * n_buf,
        compiler_params=_params("arbitrary", "arbitrary", "arbitrary"),
        name="diff_attention",
    )(lq1, lk1, lq2, lk2, z3, z3, z3, gain_b, *[w for w, _ in cast_weights])
    return outs[0], outs[1:]


def _ffn_kernel(x_ref, g_ref, wgu_ref, wd_ref, fg_ref, o_ref, h_ref, *, final_norm):
    f = pl.program_id(1)
    last_f = pl.num_programs(1) - 1

    def step(first, last):
        if first:
            h_ref[...] = _rms_rows(x_ref[...], g_ref[...]).astype(BF16)
        gu = jnp.dot(h_ref[...], wgu_ref[...], preferred_element_type=F32)
        ck = FFN_GU_CHUNK
        acts = []
        for k in range(gu.shape[1] // (2 * ck)):
            gate, up = gu[:, 2 * k * ck:(2 * k + 1) * ck], gu[:, (2 * k + 1) * ck:(2 * k + 2) * ck]
            acts.append((gate * jax.nn.sigmoid(gate) * up).astype(BF16))
        act = jnp.concatenate(acts, axis=1)
        out = (x_ref[...] if first else o_ref[...]) + jnp.dot(act, wd_ref[...], preferred_element_type=F32)
        if last and final_norm:
            out = _rms_rows(out, fg_ref[...])
        o_ref[...] = out

    pl.when(f == 0)(functools.partial(step, True, False))
    if final_norm:
        pl.when((f > 0) & (f < last_f))(functools.partial(step, False, False))
        pl.when(f == last_f)(functools.partial(step, False, True))
    else:
        pl.when(f > 0)(functools.partial(step, False, False))


def _ffn(x2, gain, wgu, wd, layer, final_gain, final_norm):
    m, d = x2.shape
    dff = wd.shape[1]
    tm, tf = FFN_TM, FFN_TF
    assert m % tm == 0 and dff % tf == 0 and tf % FFN_GU_CHUNK == 0
    return pl.pallas_call(
        functools.partial(_ffn_kernel, final_norm=final_norm),
        grid=(m // tm, dff // tf),
        in_specs=[
            pl.BlockSpec((tm, d), lambda i, f: (i, 0)),
            pl.BlockSpec((1, d), lambda i, f: (0, 0)),
            pl.BlockSpec((None, d, 2 * tf), lambda i, f: (layer, 0, f)),
            pl.BlockSpec((None, tf, d), lambda i, f: (layer, f, 0)),
            pl.BlockSpec((1, d), lambda i, f: (0, 0)),
        ],
        out_specs=pl.BlockSpec((tm, d), lambda i, f: (i, 0)),
        out_shape=jax.ShapeDtypeStruct((m, d), F32),
        scratch_shapes=[pltpu.VMEM((tm, d), BF16)],
        compiler_params=_params("parallel", "arbitrary"),
        name="ffn",
    )(x2, gain, wgu, wd, final_gain)


def kernel(x, positions, attn_norm, w_in, gmlp_ln_g, gmlp_ln_b, gmlp_ws, gmlp_bs, lambda_q1, lambda_k1,
           lambda_q2, lambda_k2, conv_w, mix_norm, w_out, ffn_norm, w_gate, w_up, w_down, final_norm):
    b, s, d = x.shape
    m = b * s
    depth = w_in.shape[0]

    inv_freq = 1.0 / (ROPE_THETA ** (jnp.arange(0, HEAD_DIM, 2, dtype=F32) / HEAD_DIM))
    ang = positions.astype(F32).reshape(m, 1) * jnp.concatenate([inv_freq, inv_freq]).reshape(1, HEAD_DIM)
    sign = jnp.concatenate([-jnp.ones((HEAD_DIM // 2,), F32), jnp.ones((HEAD_DIM // 2,), F32)]).reshape(1, HEAD_DIM)

    w_in_l, w_out_l = w_in[0].astype(BF16), w_out[0].astype(BF16)
    x2 = x.reshape(m, d)
    for l in range(depth):
        lam_init = 0.8 - 0.6 * math.exp(-0.3 * l)
        gain = mix_norm[l].astype(F32)
        z = _in_proj(x2, attn_norm[l].reshape(1, d), w_in_l[None], 0, ang, sign)
        z3 = z.reshape(b, s, z.shape[1])
        mix_b, casts = _diff_attention(
            z3, lambda_q1[l].reshape(1, HEAD_DIM), lambda_k1[l].reshape(1, HEAD_DIM),
            lambda_q2[l].reshape(1, HEAD_DIM), lambda_k2[l].reshape(1, HEAD_DIM),
            gain[GMLP_WIDTH:GMLP_WIDTH + DIFF_WIDTH].reshape(N_GROUPS, 1, DIFF_V_DIM), lam_init,
            [(w_in, l + 1), (w_out, l + 1)] if l + 1 < depth else [])
        x2, (w_down_l,), w_gu_l = _mix_out_proj(
            x2, z, mix_b.reshape(m, DIFF_WIDTH), gmlp_ln_g[l], gmlp_ln_b[l], gmlp_ws[l],
            jnp.tile(gmlp_bs[l], (1, OUT_PROJ_TM // GMLP_CHUNK)).reshape(N_GROUPS, OUT_PROJ_TM, 1), conv_w[l],
            gain[:GMLP_WIDTH].reshape(1, GMLP_WIDTH), gain[GMLP_WIDTH + DIFF_WIDTH:].reshape(1, CONV_WIDTH),
            w_out_l[None], 0, s, [(w_down, l)], (w_gate, w_up, l))
        x2 = _ffn(x2, ffn_norm[l].reshape(1, d), w_gu_l[None], w_down_l[None], 0,
                  final_norm.reshape(1, d), final_norm=(l == depth - 1))
        if l + 1 < depth:
            w_in_l, w_out_l = casts
    return x2.reshape(b, s, d)
```

```python
import functools
import math

import jax
import jax.numpy as jnp
from jax import lax
from jax.experimental import pallas as pl
from jax.experimental.pallas import tpu as pltpu

F32 = jnp.float32
BF16 = jnp.bfloat16

HEAD_DIM = 128
N_GROUPS = 4
GMLP_WIDTH = N_GROUPS * HEAD_DIM
GMLP_CHUNK = 128
DIFF_QK_WIDTH = N_GROUPS * 2 * HEAD_DIM
DIFF_V_DIM = 2 * HEAD_DIM
DIFF_WIDTH = N_GROUPS * DIFF_V_DIM
CONV_WIDTH = N_GROUPS * HEAD_DIM
CONV_K = 3
ROPE_THETA = 10000.0
RMS_EPS = 1e-6
LN_EPS = 1e-5
LOG2_E = math.log2(math.e)

Q_OFF = 2 * GMLP_WIDTH
K_OFF = Q_OFF + DIFF_QK_WIDTH
V_OFF = K_OFF + DIFF_QK_WIDTH
C_OFF = V_OFF + DIFF_WIDTH

VMEM_LIMIT_BYTES = 56 * 1024 * 1024

IN_PROJ_TM = 512
IN_PROJ_TN = 2816
ATTN_TQ = 512
CAST_SLAB_ROWS = 16
ATTN_ROW_CHUNK = 32
OUT_PROJ_TM = 512
FFN_TM = 1024
FFN_TF = 512
FFN_GU_CHUNK = 256
CONV_HALO_ROWS = 16


def _params(*semantics):
    return pltpu.CompilerParams(dimension_semantics=semantics, vmem_limit_bytes=VMEM_LIMIT_BYTES)


def _rms_rows(x, gain):
    return x * lax.rsqrt(jnp.mean(x * x, axis=-1, keepdims=True) + RMS_EPS) * gain


def _group_rms(x):
    return x * lax.rsqrt(jnp.mean(x * x, axis=-1, keepdims=True) + RMS_EPS)


def _gelu(x):
    return 0.5 * x * (1.0 + lax.erf(x * math.sqrt(0.5)))


def _cast_specs(cast_weights, n_steps, step_of, n_grid_axes):
    in_specs, out_specs, out_shapes = [], [], []
    for w, layer in cast_weights:
        _, r, c = w.shape
        slab = next(k for k in range(CAST_SLAB_ROWS, r + 1, CAST_SLAB_ROWS) if r % k == 0 and r // k <= n_steps)
        blk = lambda *idx, last=r // slab - 1: jnp.minimum(step_of(*idx[:n_grid_axes]), last)
        in_specs.append(pl.BlockSpec((None, slab, c), lambda *idx, layer=layer, blk=blk: (layer, blk(*idx), 0)))
        out_specs.append(pl.BlockSpec((slab, c), lambda *idx, blk=blk: (blk(*idx), 0)))
        out_shapes.append(jax.ShapeDtypeStruct((r, c), BF16))
    return in_specs, out_specs, out_shapes


def _interleaved_pair_specs(w_a, w_b, layer, n_steps, step_of, n_grid_axes):
    (in_a,), (out_a,), _ = _cast_specs([(w_a, layer)], n_steps, step_of, n_grid_axes)
    (in_b,), _, _ = _cast_specs([(w_b, layer)], n_steps, step_of, n_grid_axes)
    _, r, c = w_a.shape
    slab = out_a.block_shape[0]
    out = pl.BlockSpec((slab, 2 * c), out_a.index_map)
    return [in_a, in_b], out, jax.ShapeDtypeStruct((r, 2 * c), BF16)


def _cast_interleaved(a_ref, b_ref, dst_ref, chunk):
    for k in range(a_ref.shape[1] // chunk):
        dst_ref[:, 2 * k * chunk:(2 * k + 1) * chunk] = a_ref[:, k * chunk:(k + 1) * chunk].astype(dst_ref.dtype)
        dst_ref[:, (2 * k + 1) * chunk:(2 * k + 2) * chunk] = b_ref[:, k * chunk:(k + 1) * chunk].astype(dst_ref.dtype)


def _cast_slabs(src_refs, dst_refs):
    for src_ref, dst_ref in zip(src_refs, dst_refs):
        dst_ref[...] = src_ref[...].astype(dst_ref.dtype)


def _in_proj_kernel(x_ref, g_ref, w_ref, ang_ref, sign_ref, z_ref, h_ref, cos_ref, sin_ref, *, tn, n_col_tiles):
    j = pl.program_id(1)

    def tile(col0):
        first = col0 == (n_col_tiles - 1) * tn
        if first:
            h_ref[...] = _rms_rows(x_ref[...], g_ref[...]).astype(BF16)
        acc = jnp.dot(h_ref[...], w_ref[...], preferred_element_type=F32)
        if first:
            cos_ref[...] = jnp.cos(ang_ref[...])
            sin_ref[...] = jnp.sin(ang_ref[...]) * sign_ref[...]
        cosf, sinf = cos_ref[...], sin_ref[...]
        for c in range(tn // HEAD_DIM):
            col = col0 + c * HEAD_DIM
            r = acc[:, c * HEAD_DIM:(c + 1) * HEAD_DIM]
            if col < Q_OFF:
                r = _gelu(r)
            elif col < V_OFF:
                r = r * cosf + pltpu.roll(r, HEAD_DIM // 2, 1) * sinf
                if col < K_OFF:
                    r = r * (LOG2_E / math.sqrt(HEAD_DIM))
            z_ref[:, c * HEAD_DIM:(c + 1) * HEAD_DIM] = r.astype(z_ref.dtype)

    for jt in range(n_col_tiles):
        pl.when(j == n_col_tiles - 1 - jt)(functools.partial(tile, jt * tn))


def _in_proj(x2, gain, w, layer, ang, sign):
    m, d = x2.shape
    n = w.shape[2]
    tm, tn = IN_PROJ_TM, IN_PROJ_TN
    assert m % tm == 0 and n % tn == 0 and tn % HEAD_DIM == 0
    return pl.pallas_call(
        functools.partial(_in_proj_kernel, tn=tn, n_col_tiles=n // tn),
        grid=(m // tm, n // tn),
        in_specs=[
            pl.BlockSpec((tm, d), lambda i, j: (i, 0)),
            pl.BlockSpec((1, d), lambda i, j: (0, 0)),
            pl.BlockSpec((None, d, tn), lambda i, j: (layer, 0, n // tn - 1 - j)),
            pl.BlockSpec((tm, HEAD_DIM), lambda i, j: (i, 0)),
            pl.BlockSpec((1, HEAD_DIM), lambda i, j: (0, 0)),
        ],
        out_specs=pl.BlockSpec((tm, tn), lambda i, j: (i, n // tn - 1 - j)),
        out_shape=jax.ShapeDtypeStruct((m, n), BF16),
        scratch_shapes=[pltpu.VMEM((tm, d), BF16),
                        pltpu.VMEM((tm, HEAD_DIM), F32),
                        pltpu.VMEM((tm, HEAD_DIM), F32)],
        compiler_params=_params("parallel", "arbitrary"),
        name="in_proj",
    )(x2, gain, w, ang, sign)


def _gmlp_mixer(za_ref, lng_ref, lnb_ref, ws_ref, bs_ref, ga_ref, vn_ref, mixed_ref, out_ref, rows):
    groups = [slice(g * HEAD_DIM, (g + 1) * HEAD_DIM) for g in range(N_GROUPS)]
    for g, cols in enumerate(groups):
        v = za_ref[:, GMLP_WIDTH + cols.start:GMLP_WIDTH + cols.stop].astype(F32)
        mu = jnp.mean(v, axis=-1, keepdims=True)
        vc = v - mu
        var = jnp.mean(vc * vc, axis=-1, keepdims=True)
        vn = vc * lax.rsqrt(var + LN_EPS) * lng_ref[g:g + 1, :] + lnb_ref[g:g + 1, :]
        vn_ref[:, cols] = vn.astype(BF16)
    t_idx = lax.broadcasted_iota(jnp.int32, (GMLP_CHUNK, GMLP_CHUNK), 0)
    s_idx = lax.broadcasted_iota(jnp.int32, (GMLP_CHUNK, GMLP_CHUNK), 1)
    for g, cols in enumerate(groups):
        w = jnp.where(s_idx <= t_idx, ws_ref[g], 0.0).astype(BF16)
        for c in range(rows // GMLP_CHUNK):
            chunk = slice(c * GMLP_CHUNK, (c + 1) * GMLP_CHUNK)
            mixed_ref[chunk, cols] = jnp.dot(w, vn_ref[chunk, cols], preferred_element_type=F32)
    for g, cols in enumerate(groups):
        u = za_ref[:, cols].astype(F32)
        ya = _group_rms(u * (mixed_ref[:, cols] + bs_ref[g])) * ga_ref[:, cols]
        out_ref[:, cols] = ya.astype(out_ref.dtype)


def _short_conv_mixer(bg_ref, cg_ref, hc_ref, cgh_ref, hch_ref, cw_ref, gc_ref, out_ref, at_seq_start):
    xh = cg_ref[...].astype(F32) * hc_ref[...].astype(F32)
    halo = cgh_ref[...].astype(F32) * hch_ref[...].astype(F32)
    halo = jnp.where(at_seq_start, 0.0, halo)
    row = lax.broadcasted_iota(jnp.int32, xh.shape, 0)
    prev1 = jnp.where(row == 0, halo[CONV_HALO_ROWS - 1:CONV_HALO_ROWS, :], pltpu.roll(xh, 1, 0))
    prev2 = jnp.where(row == 0, halo[CONV_HALO_ROWS - 2:CONV_HALO_ROWS - 1, :],
                      jnp.where(row == 1, halo[CONV_HALO_ROWS - 1:CONV_HALO_ROWS, :], pltpu.roll(xh, 2, 0)))
    y = cw_ref[0:1, :] * prev2 + cw_ref[1:2, :] * prev1 + cw_ref[2:3, :] * xh
    yc = bg_ref[...].astype(F32) * y
    for g in range(N_GROUPS):
        cols = slice(g * HEAD_DIM, (g + 1) * HEAD_DIM)
        out_ref[:, cols] = (_group_rms(yc[:, cols]) * gc_ref[:, cols]).astype(out_ref.dtype)


def _mix_out_kernel(x_ref, za_ref, bg_ref, cg_ref, hc_ref, cgh_ref, hch_ref, mb_ref, lng_ref, lnb_ref, ws_ref,
                    bs_ref, cw_ref, ga_ref, gc_ref, w0_ref, w1_ref, w2_ref, w3_ref, *rest, rows, blocks_per_seq,
                    n_cast):
    cast_src, (gate_ref, up_ref, o_ref), cast_dst = rest[:n_cast], rest[n_cast:n_cast + 3], rest[n_cast + 3:2 * n_cast + 3]
    gu_ref, vn_ref, mixed_ref, ma_ref, mc_ref = rest[2 * n_cast + 3:]
    _cast_slabs(cast_src, cast_dst)
    _cast_interleaved(gate_ref, up_ref, gu_ref, FFN_GU_CHUNK)
    o_ref[...] = (x_ref[...] + jnp.dot(mb_ref[:, :GMLP_WIDTH], w1_ref[...], preferred_element_type=F32)
                  + jnp.dot(mb_ref[:, GMLP_WIDTH:], w2_ref[...], preferred_element_type=F32))
    _gmlp_mixer(za_ref, lng_ref, lnb_ref, ws_ref, bs_ref, ga_ref, vn_ref, mixed_ref, ma_ref, rows)
    _short_conv_mixer(bg_ref, cg_ref, hc_ref, cgh_ref, hch_ref, cw_ref, gc_ref, mc_ref,
                      pl.program_id(0) % blocks_per_seq == 0)
    o_ref[...] += (jnp.dot(ma_ref[...], w0_ref[...], preferred_element_type=F32)
                   + jnp.dot(mc_ref[...], w3_ref[...], preferred_element_type=F32))


def _mix_out_proj(x2, z, mix_b, ln_g, ln_b, ws, bs, conv_w, gain_a, gain_c, w, layer, seq_len, cast_weights,
                  gate_up):
    m, d = x2.shape
    rows = OUT_PROJ_TM
    assert m % rows == 0 and seq_len % rows == 0 and rows % GMLP_CHUNK == 0 and C_OFF % CONV_WIDTH == 0
    assert DIFF_WIDTH == 2 * GMLP_WIDTH and GMLP_WIDTH == CONV_WIDTH
    cb = C_OFF // CONV_WIDTH
    hb = rows // CONV_HALO_ROWS
    full = lambda shape: pl.BlockSpec(shape, lambda i: (0,) * len(shape))
    zcol = lambda width, col: pl.BlockSpec((rows, width), lambda i: (i, col))
    halo = lambda col: pl.BlockSpec((CONV_HALO_ROWS, CONV_WIDTH), lambda i: (jnp.maximum(i * hb - 1, 0), col))
    wrows = lambda blk: pl.BlockSpec((None, GMLP_WIDTH, d), lambda i: (layer, blk, 0))
    cast_in, cast_out, cast_shapes = _cast_specs(cast_weights, m // rows, lambda i: i, 1)
    gu_in, gu_out, gu_shape = _interleaved_pair_specs(*gate_up, m // rows, lambda i: i, 1)
    outs = pl.pallas_call(
        functools.partial(_mix_out_kernel, rows=rows, blocks_per_seq=seq_len // rows, n_cast=len(cast_weights)),
        grid=(m // rows,),
        in_specs=[
            pl.BlockSpec((rows, d), lambda i: (i, 0)),
            zcol(2 * GMLP_WIDTH, 0), zcol(CONV_WIDTH, cb), zcol(CONV_WIDTH, cb + 1), zcol(CONV_WIDTH, cb + 2),
            halo(cb + 1), halo(cb + 2),
            pl.BlockSpec((rows, DIFF_WIDTH), lambda i: (i, 0)),
            full((N_GROUPS, HEAD_DIM)), full((N_GROUPS, HEAD_DIM)),
            full((N_GROUPS, GMLP_CHUNK, GMLP_CHUNK)), full((N_GROUPS, rows, 1)),
            full((CONV_K, CONV_WIDTH)), full((1, GMLP_WIDTH)), full((1, CONV_WIDTH)),
            wrows(0), wrows(1), wrows(2), wrows(3),
        ] + cast_in + gu_in,
        out_specs=[pl.BlockSpec((rows, d), lambda i: (i, 0))] + cast_out + [gu_out],
        out_shape=[jax.ShapeDtypeStruct((m, d), F32)] + cast_shapes + [gu_shape],
        scratch_shapes=[pltpu.VMEM((rows, GMLP_WIDTH), BF16),
                        pltpu.VMEM((rows, GMLP_WIDTH), F32),
                        pltpu.VMEM((rows, GMLP_WIDTH), BF16),
                        pltpu.VMEM((rows, CONV_WIDTH), BF16)],
        compiler_params=_params("arbitrary"),
        name="mix_out_proj",
    )(x2, z, z, z, z, z, z, mix_b, ln_g, ln_b, ws, bs, conv_w, gain_a, gain_c, w, w, w, w, *[cw for cw, _ in cast_weights],
      gate_up[0], gate_up[1])
    return outs[0], outs[1:-1], outs[-1]


def _attn_kernel(lq1_ref, lk1_ref, lq2_ref, lk2_ref, q_ref, k_ref, v_ref, gain_ref, o_ref, m_ref, l_ref, acc_ref,
                 *bufs, tq, n_q, n_buf, lam_init):
    rc = ATTN_ROW_CHUNK
    n_tiles = tq // HEAD_DIM
    p_bufs, a_bufs = bufs[:n_buf], bufs[n_buf:]

    def key_rows(j):
        return slice(j * tq, (j + 1) * tq)

    def scores(j):
        return [lax.dot_general(q_ref[0, :, mp * HEAD_DIM:(mp + 1) * HEAD_DIM],
                                k_ref[0, key_rows(j), mp * HEAD_DIM:(mp + 1) * HEAD_DIM],
                                (((1,), (1,)), ((), ())), preferred_element_type=F32) for mp in range(2)]

    def softmax(s_maps, buf, diagonal, first):
        p_ref, alpha_ref = p_bufs[buf], a_bufs[buf]
        for c in range(2 * tq // rc):
            rows = slice(c * rc, (c + 1) * rc)
            q0 = rows.start % tq
            s = s_maps[rows.start // tq]
            s_rows = slice(q0, q0 + rc)
            live = [t for t in range(n_tiles) if (not diagonal) or t * HEAD_DIM <= q0 + rc - 1]
            tiles = []
            for t in live:
                st = s[s_rows, t * HEAD_DIM:(t + 1) * HEAD_DIM]
                if diagonal and (t + 1) * HEAD_DIM - 1 > q0:
                    row = lax.broadcasted_iota(jnp.int32, (rc, HEAD_DIM), 0) + q0
                    col = lax.broadcasted_iota(jnp.int32, (rc, HEAD_DIM), 1) + t * HEAD_DIM
                    st = jnp.where(col <= row, st, -jnp.inf)
                tiles.append(st)
            m_blk = jnp.max(functools.reduce(jnp.maximum, tiles), axis=-1, keepdims=True)
            if first:
                m_new = jnp.broadcast_to(m_blk, (rc, HEAD_DIM))
                ps = [jnp.exp2(st - m_new) for st in tiles]
                l_ref[rows, :] = functools.reduce(jnp.add, ps)
            else:
                m_old = m_ref[rows, :]
                m_new = jnp.maximum(m_old, m_blk)
                alpha = jnp.exp2(m_old - m_new)
                ps = [jnp.exp2(st - m_new) for st in tiles]
                l_ref[rows, :] = alpha * l_ref[rows, :] + functools.reduce(jnp.add, ps)
                alpha_ref[rows, :] = alpha
            m_ref[rows, :] = m_new
            for t in range(n_tiles):
                p_t = ps[live.index(t)].astype(BF16) if t in live else jnp.zeros((rc, HEAD_DIM), BF16)
                p_ref[rows, t * HEAD_DIM:(t + 1) * HEAD_DIM] = p_t

    def accumulate(j, buf, first):
        pv = jnp.dot(p_bufs[buf][...], v_ref[0, key_rows(j), :], preferred_element_type=F32)
        if first:
            acc_ref[...] = pv
        else:
            alpha = a_bufs[buf][...]
            for t in range(DIFF_V_DIM // HEAD_DIM):
                cols = slice(t * HEAD_DIM, (t + 1) * HEAD_DIM)
                acc_ref[:, cols] = acc_ref[:, cols] * alpha + pv[:, cols]

    def query_block(qi):
        s_next = scores(0)
        for j in range(qi + 1):
            s = s_next
            if j < qi:
                s_next = scores(j + 1)
            if j >= 1:
                accumulate(j - 1, (j - 1) % n_buf, first=(j == 1))
            softmax(s, j % n_buf, diagonal=(j == qi), first=(j == 0))
        accumulate(qi, qi % n_buf, first=(qi == 0))

        lam = (jnp.exp(jnp.sum(lq1_ref[...] * lk1_ref[...], axis=-1, keepdims=True))
               - jnp.exp(jnp.sum(lq2_ref[...] * lk2_ref[...], axis=-1, keepdims=True)) + lam_init)
        o = acc_ref[...] * (1.0 / jnp.sum(l_ref[...], axis=-1, keepdims=True))
        o = o[:tq] - lam * o[tq:]
        o = _group_rms(o) * (1.0 - lam_init) * gain_ref[0]
        o_ref[0] = o.astype(o_ref.dtype)

    for qi in range(n_q):
        pl.when(pl.program_id(2) == qi)(functools.partial(query_block, qi))


def _diff_attention(z3, lq1, lk1, lq2, lk2, gain_b, lam_init):
    b, s, _ = z3.shape
    tq = ATTN_TQ
    assert s % tq == 0
    n_buf = s // tq
    qb, kb, vb = Q_OFF // DIFF_V_DIM, K_OFF // DIFF_V_DIM, V_OFF // DIFF_V_DIM
    vec = pl.BlockSpec((1, HEAD_DIM), lambda bi, h, i: (0, 0))
    return pl.pallas_call(
        functools.partial(_attn_kernel, tq=tq, n_q=s // tq, n_buf=n_buf, lam_init=lam_init),
        grid=(b, N_GROUPS, s // tq),
        in_specs=[
            vec, vec, vec, vec,
            pl.BlockSpec((1, tq, DIFF_V_DIM), lambda bi, h, i: (bi, i, qb + h)),
            pl.BlockSpec((1, s, DIFF_V_DIM), lambda bi, h, i: (bi, 0, kb + h)),
            pl.BlockSpec((1, s, DIFF_V_DIM), lambda bi, h, i: (bi, 0, vb + h)),
            pl.BlockSpec((1, 1, DIFF_V_DIM), lambda bi, h, i: (h, 0, 0)),
        ],
        out_specs=pl.BlockSpec((1, tq, DIFF_V_DIM), lambda bi, h, i: (bi, i, h)),
        out_shape=jax.ShapeDtypeStruct((b, s, DIFF_WIDTH), BF16),
        scratch_shapes=[
            pltpu.VMEM((2 * tq, HEAD_DIM), F32),
            pltpu.VMEM((2 * tq, HEAD_DIM), F32),
            pltpu.VMEM((2 * tq, DIFF_V_DIM), F32),
        ] + [pltpu.VMEM((2 * tq, tq), BF16)] * n_buf
          + [pltpu.VMEM((2 * tq, HEAD_DIM), F32)] * n_buf,
        compiler_params=_params("parallel", "parallel", "arbitrary"),
        name="diff_attention",
    )(lq1, lk1, lq2, lk2, z3, z3, z3, gain_b)


def _ffn_kernel(x_ref, g_ref, wgu_ref, wd_ref, fg_ref, o_ref, h_ref, *, final_norm):
    f = pl.program_id(1)
    last_f = pl.num_programs(1) - 1

    def step(first, last):
        if first:
            h_ref[...] = _rms_rows(x_ref[...], g_ref[...]).astype(BF16)
        gu = jnp.dot(h_ref[...], wgu_ref[...], preferred_element_type=F32)
        ck = FFN_GU_CHUNK
        acts = []
        for k in range(gu.shape[1] // (2 * ck)):
            gate, up = gu[:, 2 * k * ck:(2 * k + 1) * ck], gu[:, (2 * k + 1) * ck:(2 * k + 2) * ck]
            acts.append((gate * jax.nn.sigmoid(gate) * up).astype(BF16))
        act = jnp.concatenate(acts, axis=1)
        out = (x_ref[...] if first else o_ref[...]) + jnp.dot(act, wd_ref[...], preferred_element_type=F32)
        if last and final_norm:
            out = _rms_rows(out, fg_ref[...])
        o_ref[...] = out

    pl.when(f == 0)(functools.partial(step, True, False))
    if final_norm:
        pl.when((f > 0) & (f < last_f))(functools.partial(step, False, False))
        pl.when(f == last_f)(functools.partial(step, False, True))
    else:
        pl.when(f > 0)(functools.partial(step, False, False))


def _ffn(x2, gain, wgu, wd, layer, final_gain, final_norm):
    m, d = x2.shape
    dff = wd.shape[1]
    tm, tf = FFN_TM, FFN_TF
    assert m % tm == 0 and dff % tf == 0 and tf % FFN_GU_CHUNK == 0
    return pl.pallas_call(
        functools.partial(_ffn_kernel, final_norm=final_norm),
        grid=(m // tm, dff // tf),
        in_specs=[
            pl.BlockSpec((tm, d), lambda i, f: (i, 0)),
            pl.BlockSpec((1, d), lambda i, f: (0, 0)),
            pl.BlockSpec((None, d, 2 * tf), lambda i, f: (layer, 0, f)),
            pl.BlockSpec((None, tf, d), lambda i, f: (layer, f, 0)),
            pl.BlockSpec((1, d), lambda i, f: (0, 0)),
        ],
        out_specs=pl.BlockSpec((tm, d), lambda i, f: (i, 0)),
        out_shape=jax.ShapeDtypeStruct((m, d), F32),
        scratch_shapes=[pltpu.VMEM((tm, d), BF16)],
        compiler_params=_params("parallel", "arbitrary"),
        name="ffn",
    )(x2, gain, wgu, wd, final_gain)


def kernel(x, positions, attn_norm, w_in, gmlp_ln_g, gmlp_ln_b, gmlp_ws, gmlp_bs, lambda_q1, lambda_k1,
           lambda_q2, lambda_k2, conv_w, mix_norm, w_out, ffn_norm, w_gate, w_up, w_down, final_norm):
    b, s, d = x.shape
    m = b * s
    depth = w_in.shape[0]

    inv_freq = 1.0 / (ROPE_THETA ** (jnp.arange(0, HEAD_DIM, 2, dtype=F32) / HEAD_DIM))
    ang = positions.astype(F32).reshape(m, 1) * jnp.concatenate([inv_freq, inv_freq]).reshape(1, HEAD_DIM)
    sign = jnp.concatenate([-jnp.ones((HEAD_DIM // 2,), F32), jnp.ones((HEAD_DIM // 2,), F32)]).reshape(1, HEAD_DIM)

    w_in_l, w_out_l = w_in[0].astype(BF16), w_out[0].astype(BF16)
    x2 = x.reshape(m, d)
    for l in range(depth):
        lam_init = 0.8 - 0.6 * math.exp(-0.3 * l)
        gain = mix_norm[l]
        z = _in_proj(x2, attn_norm[l].reshape(1, d), w_in_l[None], 0, ang, sign)
        z3 = z.reshape(b, s, z.shape[1])
        mix_b = _diff_attention(
            z3, lambda_q1[l].reshape(1, HEAD_DIM), lambda_k1[l].reshape(1, HEAD_DIM),
            lambda_q2[l].reshape(1, HEAD_DIM), lambda_k2[l].reshape(1, HEAD_DIM),
            gain[GMLP_WIDTH:GMLP_WIDTH + DIFF_WIDTH].reshape(N_GROUPS, 1, DIFF_V_DIM), lam_init)
        next_proj = [(w_in, l + 1), (w_out, l + 1)] if l + 1 < depth else []
        x2, (w_down_l, *casts), w_gu_l = _mix_out_proj(
            x2, z, mix_b.reshape(m, DIFF_WIDTH), gmlp_ln_g[l], gmlp_ln_b[l], gmlp_ws[l],
            jnp.tile(gmlp_bs[l], (1, OUT_PROJ_TM // GMLP_CHUNK)).reshape(N_GROUPS, OUT_PROJ_TM, 1), conv_w[l],
            gain[:GMLP_WIDTH].reshape(1, GMLP_WIDTH), gain[GMLP_WIDTH + DIFF_WIDTH:].reshape(1, CONV_WIDTH),
            w_out_l[None], 0, s, [(w_down, l)] + next_proj, (w_gate, w_up, l))
        x2 = _ffn(x2, ffn_norm[l].reshape(1, d), w_gu_l[None], w_down_l[None], 0,
                  final_norm.reshape(1, d), final_norm=(l == depth - 1))
        if l + 1 < depth:
            w_in_l, w_out_l = casts
    return x2.reshape(b, s, d)
```

```python
import functools
import math

import jax
import jax.numpy as jnp
from jax import lax
from jax.experimental import pallas as pl
from jax.experimental.pallas import tpu as pltpu

F32 = jnp.float32
BF16 = jnp.bfloat16

HEAD_DIM = 128
N_GROUPS = 4
GMLP_WIDTH = N_GROUPS * HEAD_DIM
GMLP_CHUNK = 128
DIFF_QK_WIDTH = N_GROUPS * 2 * HEAD_DIM
DIFF_V_DIM = 2 * HEAD_DIM
DIFF_WIDTH = N_GROUPS * DIFF_V_DIM
CONV_WIDTH = N_GROUPS * HEAD_DIM
CONV_K = 3
ROPE_THETA = 10000.0
RMS_EPS = 1e-6
LN_EPS = 1e-5
LOG2_E = math.log2(math.e)

Q_OFF = 2 * GMLP_WIDTH
K_OFF = Q_OFF + DIFF_QK_WIDTH
V_OFF = K_OFF + DIFF_QK_WIDTH
C_OFF = V_OFF + DIFF_WIDTH

VMEM_LIMIT_BYTES = 56 * 1024 * 1024

IN_PROJ_TM = 512
IN_PROJ_TN = 2816
ATTN_TQ = 1024
ATTN_TK = 512
CAST_SLAB_ROWS = 16
ATTN_ROW_CHUNK = 32
OUT_PROJ_TM = 512
FFN_TM = 1024
FFN_TF = 512
FFN_GU_CHUNK = 256
CONV_HALO_ROWS = 16


def _params(*semantics):
    return pltpu.CompilerParams(dimension_semantics=semantics, vmem_limit_bytes=VMEM_LIMIT_BYTES)


def _rms_rows(x, gain):
    return x * lax.rsqrt(jnp.mean(x * x, axis=-1, keepdims=True) + RMS_EPS) * gain


def _group_rms(x):
    return x * lax.rsqrt(jnp.mean(x * x, axis=-1, keepdims=True) + RMS_EPS)


def _gelu(x):
    return 0.5 * x * (1.0 + lax.erf(x * math.sqrt(0.5)))


def _cast_specs(cast_weights, n_steps, step_of, n_grid_axes):
    in_specs, out_specs, out_shapes = [], [], []
    for w, layer in cast_weights:
        _, r, c = w.shape
        slab = next(k for k in range(CAST_SLAB_ROWS, r + 1, CAST_SLAB_ROWS) if r % k == 0 and r // k <= n_steps)
        blk = lambda *idx, last=r // slab - 1: jnp.minimum(step_of(*idx[:n_grid_axes]), last)
        in_specs.append(pl.BlockSpec((None, slab, c), lambda *idx, layer=layer, blk=blk: (layer, blk(*idx), 0)))
        out_specs.append(pl.BlockSpec((slab, c), lambda *idx, blk=blk: (blk(*idx), 0)))
        out_shapes.append(jax.ShapeDtypeStruct((r, c), BF16))
    return in_specs, out_specs, out_shapes


def _interleaved_pair_specs(w_a, w_b, layer, n_steps, step_of, n_grid_axes):
    (in_a,), (out_a,), _ = _cast_specs([(w_a, layer)], n_steps, step_of, n_grid_axes)
    (in_b,), _, _ = _cast_specs([(w_b, layer)], n_steps, step_of, n_grid_axes)
    _, r, c = w_a.shape
    slab = out_a.block_shape[0]
    out = pl.BlockSpec((slab, 2 * c), out_a.index_map)
    return [in_a, in_b], out, jax.ShapeDtypeStruct((r, 2 * c), BF16)


def _cast_interleaved(a_ref, b_ref, dst_ref, chunk):
    for k in range(a_ref.shape[1] // chunk):
        dst_ref[:, 2 * k * chunk:(2 * k + 1) * chunk] = a_ref[:, k * chunk:(k + 1) * chunk].astype(dst_ref.dtype)
        dst_ref[:, (2 * k + 1) * chunk:(2 * k + 2) * chunk] = b_ref[:, k * chunk:(k + 1) * chunk].astype(dst_ref.dtype)


def _cast_slabs(src_refs, dst_refs):
    for src_ref, dst_ref in zip(src_refs, dst_refs):
        dst_ref[...] = src_ref[...].astype(dst_ref.dtype)


def _in_proj_kernel(x_ref, g_ref, w_ref, ang_ref, sign_ref, z_ref, h_ref, cos_ref, sin_ref, *, tn, n_col_tiles):
    j = pl.program_id(1)

    def tile(col0):
        first = col0 == (n_col_tiles - 1) * tn
        if first:
            h_ref[...] = _rms_rows(x_ref[...], g_ref[...]).astype(BF16)
        acc = jnp.dot(h_ref[...], w_ref[...], preferred_element_type=F32)
        if first:
            cos_ref[...] = jnp.cos(ang_ref[...])
            sin_ref[...] = jnp.sin(ang_ref[...]) * sign_ref[...]
        cosf, sinf = cos_ref[...], sin_ref[...]
        for c in range(tn // HEAD_DIM):
            col = col0 + c * HEAD_DIM
            r = acc[:, c * HEAD_DIM:(c + 1) * HEAD_DIM]
            if col < Q_OFF:
                r = _gelu(r)
            elif col < V_OFF:
                r = r * cosf + pltpu.roll(r, HEAD_DIM // 2, 1) * sinf
                if col < K_OFF:
                    r = r * (LOG2_E / math.sqrt(HEAD_DIM))
            z_ref[:, c * HEAD_DIM:(c + 1) * HEAD_DIM] = r.astype(z_ref.dtype)

    for jt in range(n_col_tiles):
        pl.when(j == n_col_tiles - 1 - jt)(functools.partial(tile, jt * tn))


def _in_proj(x2, gain, w, layer, ang, sign):
    m, d = x2.shape
    n = w.shape[2]
    tm, tn = IN_PROJ_TM, IN_PROJ_TN
    assert m % tm == 0 and n % tn == 0 and tn % HEAD_DIM == 0
    return pl.pallas_call(
        functools.partial(_in_proj_kernel, tn=tn, n_col_tiles=n // tn),
        grid=(m // tm, n // tn),
        in_specs=[
            pl.BlockSpec((tm, d), lambda i, j: (i, 0)),
            pl.BlockSpec((1, d), lambda i, j: (0, 0)),
            pl.BlockSpec((None, d, tn), lambda i, j: (layer, 0, n // tn - 1 - j)),
            pl.BlockSpec((tm, HEAD_DIM), lambda i, j: (i, 0)),
            pl.BlockSpec((1, HEAD_DIM), lambda i, j: (0, 0)),
        ],
        out_specs=pl.BlockSpec((tm, tn), lambda i, j: (i, n // tn - 1 - j)),
        out_shape=jax.ShapeDtypeStruct((m, n), BF16),
        scratch_shapes=[pltpu.VMEM((tm, d), BF16),
                        pltpu.VMEM((tm, HEAD_DIM), F32),
                        pltpu.VMEM((tm, HEAD_DIM), F32)],
        compiler_params=_params("parallel", "arbitrary"),
        name="in_proj",
    )(x2, gain, w, ang, sign)


def _gmlp_mixer(za_ref, lng_ref, lnb_ref, ws_ref, bs_ref, ga_ref, vn_ref, mixed_ref, out_ref, rows):
    groups = [slice(g * HEAD_DIM, (g + 1) * HEAD_DIM) for g in range(N_GROUPS)]
    for g, cols in enumerate(groups):
        v = za_ref[:, GMLP_WIDTH + cols.start:GMLP_WIDTH + cols.stop].astype(F32)
        mu = jnp.mean(v, axis=-1, keepdims=True)
        vc = v - mu
        var = jnp.mean(vc * vc, axis=-1, keepdims=True)
        vn = vc * lax.rsqrt(var + LN_EPS) * lng_ref[g:g + 1, :] + lnb_ref[g:g + 1, :]
        vn_ref[:, cols] = vn.astype(BF16)
    t_idx = lax.broadcasted_iota(jnp.int32, (GMLP_CHUNK, GMLP_CHUNK), 0)
    s_idx = lax.broadcasted_iota(jnp.int32, (GMLP_CHUNK, GMLP_CHUNK), 1)
    for g, cols in enumerate(groups):
        w = jnp.where(s_idx <= t_idx, ws_ref[g], 0.0).astype(BF16)
        for c in range(rows // GMLP_CHUNK):
            chunk = slice(c * GMLP_CHUNK, (c + 1) * GMLP_CHUNK)
            mixed_ref[chunk, cols] = jnp.dot(w, vn_ref[chunk, cols], preferred_element_type=F32)
    for g, cols in enumerate(groups):
        u = za_ref[:, cols].astype(F32)
        ya = _group_rms(u * (mixed_ref[:, cols] + bs_ref[g])) * ga_ref[:, cols]
        out_ref[:, cols] = ya.astype(out_ref.dtype)


def _short_conv_mixer(bg_ref, cg_ref, hc_ref, cgh_ref, hch_ref, cw_ref, gc_ref, out_ref, at_seq_start):
    xh = cg_ref[...].astype(F32) * hc_ref[...].astype(F32)
    halo = cgh_ref[...].astype(F32) * hch_ref[...].astype(F32)
    halo = jnp.where(at_seq_start, 0.0, halo)
    row = lax.broadcasted_iota(jnp.int32, xh.shape, 0)
    prev1 = jnp.where(row == 0, halo[CONV_HALO_ROWS - 1:CONV_HALO_ROWS, :], pltpu.roll(xh, 1, 0))
    prev2 = jnp.where(row == 0, halo[CONV_HALO_ROWS - 2:CONV_HALO_ROWS - 1, :],
                      jnp.where(row == 1, halo[CONV_HALO_ROWS - 1:CONV_HALO_ROWS, :], pltpu.roll(xh, 2, 0)))
    y = cw_ref[0:1, :] * prev2 + cw_ref[1:2, :] * prev1 + cw_ref[2:3, :] * xh
    yc = bg_ref[...].astype(F32) * y
    for g in range(N_GROUPS):
        cols = slice(g * HEAD_DIM, (g + 1) * HEAD_DIM)
        out_ref[:, cols] = (_group_rms(yc[:, cols]) * gc_ref[:, cols]).astype(out_ref.dtype)


def _mix_out_kernel(x_ref, za_ref, bg_ref, cg_ref, hc_ref, cgh_ref, hch_ref, mb_ref, lng_ref, lnb_ref, ws_ref,
                    bs_ref, cw_ref, ga_ref, gc_ref, w0_ref, w1_ref, w2_ref, w3_ref, *rest, rows, blocks_per_seq,
                    n_cast):
    cast_src, (gate_ref, up_ref, o_ref), cast_dst = rest[:n_cast], rest[n_cast:n_cast + 3], rest[n_cast + 3:2 * n_cast + 3]
    gu_ref, vn_ref, mixed_ref, ma_ref, mc_ref = rest[2 * n_cast + 3:]
    _cast_slabs(cast_src, cast_dst)
    _cast_interleaved(gate_ref, up_ref, gu_ref, FFN_GU_CHUNK)
    o_ref[...] = (x_ref[...] + jnp.dot(mb_ref[:, :GMLP_WIDTH], w1_ref[...], preferred_element_type=F32)
                  + jnp.dot(mb_ref[:, GMLP_WIDTH:], w2_ref[...], preferred_element_type=F32))
    _gmlp_mixer(za_ref, lng_ref, lnb_ref, ws_ref, bs_ref, ga_ref, vn_ref, mixed_ref, ma_ref, rows)
    _short_conv_mixer(bg_ref, cg_ref, hc_ref, cgh_ref, hch_ref, cw_ref, gc_ref, mc_ref,
                      pl.program_id(0) % blocks_per_seq == 0)
    o_ref[...] += (jnp.dot(ma_ref[...], w0_ref[...], preferred_element_type=F32)
                   + jnp.dot(mc_ref[...], w3_ref[...], preferred_element_type=F32))


def _mix_out_proj(x2, z, mix_b, ln_g, ln_b, ws, bs, conv_w, gain_a, gain_c, w, layer, seq_len, cast_weights,
                  gate_up):
    m, d = x2.shape
    rows = OUT_PROJ_TM
    assert m % rows == 0 and seq_len % rows == 0 and rows % GMLP_CHUNK == 0 and C_OFF % CONV_WIDTH == 0
    assert DIFF_WIDTH == 2 * GMLP_WIDTH and GMLP_WIDTH == CONV_WIDTH
    cb = C_OFF // CONV_WIDTH
    hb = rows // CONV_HALO_ROWS
    full = lambda shape: pl.BlockSpec(shape, lambda i: (0,) * len(shape))
    zcol = lambda width, col: pl.BlockSpec((rows, width), lambda i: (i, col))
    halo = lambda col: pl.BlockSpec((CONV_HALO_ROWS, CONV_WIDTH), lambda i: (jnp.maximum(i * hb - 1, 0), col))
    wrows = lambda blk: pl.BlockSpec((None, GMLP_WIDTH, d), lambda i: (layer, blk, 0))
    cast_in, cast_out, cast_shapes = _cast_specs(cast_weights, m // rows, lambda i: i, 1)
    gu_in, gu_out, gu_shape = _interleaved_pair_specs(*gate_up, m // rows, lambda i: i, 1)
    outs = pl.pallas_call(
        functools.partial(_mix_out_kernel, rows=rows, blocks_per_seq=seq_len // rows, n_cast=len(cast_weights)),
        grid=(m // rows,),
        in_specs=[
            pl.BlockSpec((rows, d), lambda i: (i, 0)),
            zcol(2 * GMLP_WIDTH, 0), zcol(CONV_WIDTH, cb), zcol(CONV_WIDTH, cb + 1), zcol(CONV_WIDTH, cb + 2),
            halo(cb + 1), halo(cb + 2),
            pl.BlockSpec((rows, DIFF_WIDTH), lambda i: (i, 0)),
            full((N_GROUPS, HEAD_DIM)), full((N_GROUPS, HEAD_DIM)),
            full((N_GROUPS, GMLP_CHUNK, GMLP_CHUNK)), full((N_GROUPS, rows, 1)),
            full((CONV_K, CONV_WIDTH)), full((1, GMLP_WIDTH)), full((1, CONV_WIDTH)),
            wrows(0), wrows(1), wrows(2), wrows(3),
        ] + cast_in + gu_in,
        out_specs=[pl.BlockSpec((rows, d), lambda i: (i, 0))] + cast_out + [gu_out],
        out_shape=[jax.ShapeDtypeStruct((m, d), F32)] + cast_shapes + [gu_shape],
        scratch_shapes=[pltpu.VMEM((rows, GMLP_WIDTH), BF16),
                        pltpu.VMEM((rows, GMLP_WIDTH), F32),
                        pltpu.VMEM((rows, GMLP_WIDTH), BF16),
                        pltpu.VMEM((rows, CONV_WIDTH), BF16)],
        compiler_params=_params("arbitrary"),
        name="mix_out_proj",
    )(x2, z, z, z, z, z, z, mix_b, ln_g, ln_b, ws, bs, conv_w, gain_a, gain_c, w, w, w, w, *[cw for cw, _ in cast_weights],
      gate_up[0], gate_up[1])
    return outs[0], outs[1:-1], outs[-1]


def _attn_kernel(lq1_ref, lk1_ref, lq2_ref, lk2_ref, q_ref, k_ref, v_ref, gain_ref, o_ref, m_ref, l_ref, acc_ref,
                 *bufs, tq, tk, n_q, n_buf, lam_init):
    rc = ATTN_ROW_CHUNK
    n_tiles = tk // HEAD_DIM
    sub = tq // tk
    p_bufs, a_bufs = bufs[:n_buf], bufs[n_buf:]

    def key_rows(j):
        return slice(j * tk, (j + 1) * tk)

    def scores(j, row_lo):
        return [lax.dot_general(q_ref[0, row_lo:, mp * HEAD_DIM:(mp + 1) * HEAD_DIM],
                                k_ref[0, key_rows(j), mp * HEAD_DIM:(mp + 1) * HEAD_DIM],
                                (((1,), (1,)), ((), ())), preferred_element_type=F32) for mp in range(2)]

    def softmax(s_maps, buf, k0, row_lo, first):
        p_ref, alpha_ref = p_bufs[buf], a_bufs[buf]
        live_rows = tq - row_lo
        for mp in range(2):
            for c in range(live_rows // rc):
                q0 = row_lo + c * rc
                rows = slice(mp * tq + q0, mp * tq + q0 + rc)
                p_rows = slice(mp * live_rows + c * rc, mp * live_rows + (c + 1) * rc)
                live = [t for t in range(n_tiles) if k0 is None or k0 + t * HEAD_DIM <= q0 + rc - 1]
                tiles = []
                for t in live:
                    st = s_maps[mp][c * rc:(c + 1) * rc, t * HEAD_DIM:(t + 1) * HEAD_DIM]
                    if k0 is not None and k0 + (t + 1) * HEAD_DIM - 1 > q0:
                        row = lax.broadcasted_iota(jnp.int32, (rc, HEAD_DIM), 0) + q0
                        col = lax.broadcasted_iota(jnp.int32, (rc, HEAD_DIM), 1) + (k0 + t * HEAD_DIM)
                        st = jnp.where(col <= row, st, -jnp.inf)
                    tiles.append(st)
                m_blk = jnp.max(functools.reduce(jnp.maximum, tiles), axis=-1, keepdims=True)
                if first:
                    m_new = jnp.broadcast_to(m_blk, (rc, HEAD_DIM))
                    ps = [jnp.exp2(st - m_new) for st in tiles]
                    l_ref[rows, :] = functools.reduce(jnp.add, ps)
                else:
                    m_old = m_ref[rows, :]
                    m_new = jnp.maximum(m_old, m_blk)
                    alpha = jnp.exp2(m_old - m_new)
                    ps = [jnp.exp2(st - m_new) for st in tiles]
                    l_ref[rows, :] = alpha * l_ref[rows, :] + functools.reduce(jnp.add, ps)
                    alpha_ref[p_rows, :] = alpha
                m_ref[rows, :] = m_new
                for t in range(n_tiles):
                    p_t = ps[live.index(t)].astype(BF16) if t in live else jnp.zeros((rc, HEAD_DIM), BF16)
                    p_ref[p_rows, t * HEAD_DIM:(t + 1) * HEAD_DIM] = p_t

    def accumulate(j, buf, row_lo, first):
        live_rows = tq - row_lo
        pv = jnp.dot(p_bufs[buf][:2 * live_rows, :], v_ref[0, key_rows(j), :],
                     preferred_element_type=F32)
        for mp in range(2):
            rows = slice(mp * tq + row_lo, (mp + 1) * tq)
            c_rows = slice(mp * live_rows, (mp + 1) * live_rows)
            if first:
                acc_ref[rows, :] = pv[c_rows, :]
            else:
                alpha = a_bufs[buf][c_rows, :]
                for t in range(DIFF_V_DIM // HEAD_DIM):
                    cols = slice(t * HEAD_DIM, (t + 1) * HEAD_DIM)
                    acc_ref[rows, cols] = acc_ref[rows, cols] * alpha + pv[c_rows, cols]

    def query_block(qi):
        n_keys = (qi + 1) * sub
        k0 = [None if j < qi * sub else (j - qi * sub) * tk for j in range(n_keys)]
        row_lo = [0 if k is None else k for k in k0]
        s_next = scores(0, row_lo[0])
        for j in range(n_keys):
            s = s_next
            if j + 1 < n_keys:
                s_next = scores(j + 1, row_lo[j + 1])
            if j >= 1:
                accumulate(j - 1, (j - 1) % n_buf, row_lo[j - 1], first=(j == 1))
            softmax(s, j % n_buf, k0[j], row_lo[j], first=(j == 0))
        accumulate(n_keys - 1, (n_keys - 1) % n_buf, row_lo[n_keys - 1], first=(n_keys == 1))

        lam = (jnp.exp(jnp.sum(lq1_ref[...] * lk1_ref[...], axis=-1, keepdims=True))
               - jnp.exp(jnp.sum(lq2_ref[...] * lk2_ref[...], axis=-1, keepdims=True)) + lam_init)
        o = acc_ref[...] * (1.0 / jnp.sum(l_ref[...], axis=-1, keepdims=True))
        o = o[:tq] - lam * o[tq:]
        o = _group_rms(o) * (1.0 - lam_init) * gain_ref[0]
        o_ref[0] = o.astype(o_ref.dtype)

    for qi in range(n_q):
        pl.when(pl.program_id(2) == qi)(functools.partial(query_block, qi))


def _diff_attention(z3, lq1, lk1, lq2, lk2, gain_b, lam_init):
    b, s, _ = z3.shape
    tq, tk = ATTN_TQ, ATTN_TK
    assert s % tq == 0 and tq % tk == 0
    n_buf = s // tk
    qb, kb, vb = Q_OFF // DIFF_V_DIM, K_OFF // DIFF_V_DIM, V_OFF // DIFF_V_DIM
    vec = pl.BlockSpec((1, HEAD_DIM), lambda bi, h, i: (0, 0))
    return pl.pallas_call(
        functools.partial(_attn_kernel, tq=tq, tk=tk, n_q=s // tq, n_buf=n_buf, lam_init=lam_init),
        grid=(b, N_GROUPS, s // tq),
        in_specs=[
            vec, vec, vec, vec,
            pl.BlockSpec((1, tq, DIFF_V_DIM), lambda bi, h, i: (bi, i, qb + h)),
            pl.BlockSpec((1, s, DIFF_V_DIM), lambda bi, h, i: (bi, 0, kb + h)),
            pl.BlockSpec((1, s, DIFF_V_DIM), lambda bi, h, i: (bi, 0, vb + h)),
            pl.BlockSpec((1, 1, DIFF_V_DIM), lambda bi, h, i: (h, 0, 0)),
        ],
        out_specs=pl.BlockSpec((1, tq, DIFF_V_DIM), lambda bi, h, i: (bi, i, h)),
        out_shape=jax.ShapeDtypeStruct((b, s, DIFF_WIDTH), BF16),
        scratch_shapes=[
            pltpu.VMEM((2 * tq, HEAD_DIM), F32),
            pltpu.VMEM((2 * tq, HEAD_DIM), F32),
            pltpu.VMEM((2 * tq, DIFF_V_DIM), F32),
        ] + [pltpu.VMEM((2 * tq, tk), BF16)] * n_buf
          + [pltpu.VMEM((2 * tq, HEAD_DIM), F32)] * n_buf,
        compiler_params=_params("parallel", "parallel", "arbitrary"),
        name="diff_attention",
    )(lq1, lk1, lq2, lk2, z3, z3, z3, gain_b)


def _ffn_kernel(x_ref, g_ref, wgu_ref, wd_ref, fg_ref, o_ref, h_ref, *, final_norm):
    f = pl.program_id(1)
    last_f = pl.num_programs(1) - 1

    def step(first, last):
        if first:
            h_ref[...] = _rms_rows(x_ref[...], g_ref[...]).astype(BF16)
        gu = jnp.dot(h_ref[...], wgu_ref[...], preferred_element_type=F32)
        ck = FFN_GU_CHUNK
        acts = []
        for k in range(gu.shape[1] // (2 * ck)):
            gate, up = gu[:, 2 * k * ck:(2 * k + 1) * ck], gu[:, (2 * k + 1) * ck:(2 * k + 2) * ck]
            acts.append((gate * jax.nn.sigmoid(gate) * up).astype(BF16))
        act = jnp.concatenate(acts, axis=1)
        out = (x_ref[...] if first else o_ref[...]) + jnp.dot(act, wd_ref[...], preferred_element_type=F32)
        if last and final_norm:
            out = _rms_rows(out, fg_ref[...])
        o_ref[...] = out

    pl.when(f == 0)(functools.partial(step, True, False))
    if final_norm:
        pl.when((f > 0) & (f < last_f))(functools.partial(step, False, False))
        pl.when(f == last_f)(functools.partial(step, False, True))
    else:
        pl.when(f > 0)(functools.partial(step, False, False))


def _ffn(x2, gain, wgu, wd, layer, final_gain, final_norm):
    m, d = x2.shape
    dff = wd.shape[1]
    tm, tf = FFN_TM, FFN_TF
    assert m % tm == 0 and dff % tf == 0 and tf % FFN_GU_CHUNK == 0
    return pl.pallas_call(
        functools.partial(_ffn_kernel, final_norm=final_norm),
        grid=(m // tm, dff // tf),
        in_specs=[
            pl.BlockSpec((tm, d), lambda i, f: (i, 0)),
            pl.BlockSpec((1, d), lambda i, f: (0, 0)),
            pl.BlockSpec((None, d, 2 * tf), lambda i, f: (layer, 0, f)),
            pl.BlockSpec((None, tf, d), lambda i, f: (layer, f, 0)),
            pl.BlockSpec((1, d), lambda i, f: (0, 0)),
        ],
        out_specs=pl.BlockSpec((tm, d), lambda i, f: (i, 0)),
        out_shape=jax.ShapeDtypeStruct((m, d), F32),
        scratch_shapes=[pltpu.VMEM((tm, d), BF16)],
        compiler_params=_params("parallel", "arbitrary"),
        name="ffn",
    )(x2, gain, wgu, wd, final_gain)


def kernel(x, positions, attn_norm, w_in, gmlp_ln_g, gmlp_ln_b, gmlp_ws, gmlp_bs, lambda_q1, lambda_k1,
           lambda_q2, lambda_k2, conv_w, mix_norm, w_out, ffn_norm, w_gate, w_up, w_down, final_norm):
    b, s, d = x.shape
    m = b * s
    depth = w_in.shape[0]

    inv_freq = 1.0 / (ROPE_THETA ** (jnp.arange(0, HEAD_DIM, 2, dtype=F32) / HEAD_DIM))
    ang = positions.astype(F32).reshape(m, 1) * jnp.concatenate([inv_freq, inv_freq]).reshape(1, HEAD_DIM)
    sign = jnp.concatenate([-jnp.ones((HEAD_DIM // 2,), F32), jnp.ones((HEAD_DIM // 2,), F32)]).reshape(1, HEAD_DIM)

    w_in_l, w_out_l = w_in[0].astype(BF16), w_out[0].astype(BF16)
    x2 = x.reshape(m, d)
    for l in range(depth):
        lam_init = 0.8 - 0.6 * math.exp(-0.3 * l)
        gain = mix_norm[l]
        z = _in_proj(x2, attn_norm[l].reshape(1, d), w_in_l[None], 0, ang, sign)
        z3 = z.reshape(b, s, z.shape[1])
        mix_b = _diff_attention(
            z3, lambda_q1[l].reshape(1, HEAD_DIM), lambda_k1[l].reshape(1, HEAD_DIM),
            lambda_q2[l].reshape(1, HEAD_DIM), lambda_k2[l].reshape(1, HEAD_DIM),
            gain[GMLP_WIDTH:GMLP_WIDTH + DIFF_WIDTH].reshape(N_GROUPS, 1, DIFF_V_DIM), lam_init)
        next_proj = [(w_in, l + 1), (w_out, l + 1)] if l + 1 < depth else []
        x2, (w_down_l, *casts), w_gu_l = _mix_out_proj(
            x2, z, mix_b.reshape(m, DIFF_WIDTH), gmlp_ln_g[l], gmlp_ln_b[l], gmlp_ws[l],
            jnp.tile(gmlp_bs[l], (1, OUT_PROJ_TM // GMLP_CHUNK)).reshape(N_GROUPS, OUT_PROJ_TM, 1), conv_w[l],
            gain[:GMLP_WIDTH].reshape(1, GMLP_WIDTH), gain[GMLP_WIDTH + DIFF_WIDTH:].reshape(1, CONV_WIDTH),
            w_out_l[None], 0, s, [(w_down, l)] + next_proj, (w_gate, w_up, l))
        x2 = _ffn(x2, ffn_norm[l].reshape(1, d), w_gu_l[None], w_down_l[None], 0,
                  final_norm.reshape(1, d), final_norm=(l == depth - 1))
        if l + 1 < depth:
            w_in_l, w_out_l = casts
    return x2.reshape(b, s, d)
```

```python
import functools
import math

import jax
import jax.numpy as jnp
from jax import lax
from jax.experimental import pallas as pl
from jax.experimental.pallas import tpu as pltpu

F32 = jnp.float32
BF16 = jnp.bfloat16

HEAD_DIM = 128
N_GROUPS = 4
GMLP_WIDTH = N_GROUPS * HEAD_DIM
GMLP_CHUNK = 128
DIFF_QK_WIDTH = N_GROUPS * 2 * HEAD_DIM
DIFF_V_DIM = 2 * HEAD_DIM
DIFF_WIDTH = N_GROUPS * DIFF_V_DIM
CONV_WIDTH = N_GROUPS * HEAD_DIM
CONV_K = 3
ROPE_THETA = 10000.0
RMS_EPS = 1e-6
LN_EPS = 1e-5
LOG2_E = math.log2(math.e)

Q_OFF = 2 * GMLP_WIDTH
K_OFF = Q_OFF + DIFF_QK_WIDTH
V_OFF = K_OFF + DIFF_QK_WIDTH
C_OFF = V_OFF + DIFF_WIDTH

VMEM_LIMIT_BYTES = 56 * 1024 * 1024

IN_PROJ_TM = 512
IN_PROJ_TN = 5632
ATTN_TQ = 1024
ATTN_TK = 512
CAST_SLAB_ROWS = 16
ATTN_ROW_CHUNK = 32
OUT_PROJ_TM = 512
FFN_TM = 1024
FFN_TF = 512
FFN_GU_CHUNK = 256
CONV_HALO_ROWS = 16


def _params(*semantics):
    return pltpu.CompilerParams(dimension_semantics=semantics, vmem_limit_bytes=VMEM_LIMIT_BYTES)


def _rms_rows(x, gain):
    return x * lax.rsqrt(jnp.mean(x * x, axis=-1, keepdims=True) + RMS_EPS) * gain


def _group_rms(x):
    return x * lax.rsqrt(jnp.mean(x * x, axis=-1, keepdims=True) + RMS_EPS)


def _gelu(x):
    return 0.5 * x * (1.0 + lax.erf(x * math.sqrt(0.5)))


def _cast_specs(cast_weights, n_steps, step_of, n_grid_axes):
    in_specs, out_specs, out_shapes = [], [], []
    for w, layer in cast_weights:
        _, r, c = w.shape
        slab = next(k for k in range(CAST_SLAB_ROWS, r + 1, CAST_SLAB_ROWS) if r % k == 0 and r // k <= n_steps)
        blk = lambda *idx, last=r // slab - 1: jnp.minimum(step_of(*idx[:n_grid_axes]), last)
        in_specs.append(pl.BlockSpec((None, slab, c), lambda *idx, layer=layer, blk=blk: (layer, blk(*idx), 0)))
        out_specs.append(pl.BlockSpec((slab, c), lambda *idx, blk=blk: (blk(*idx), 0)))
        out_shapes.append(jax.ShapeDtypeStruct((r, c), BF16))
    return in_specs, out_specs, out_shapes


def _interleaved_pair_specs(w_a, w_b, layer, n_steps, step_of, n_grid_axes):
    (in_a,), (out_a,), _ = _cast_specs([(w_a, layer)], n_steps, step_of, n_grid_axes)
    (in_b,), _, _ = _cast_specs([(w_b, layer)], n_steps, step_of, n_grid_axes)
    _, r, c = w_a.shape
    slab = out_a.block_shape[0]
    out = pl.BlockSpec((slab, 2 * c), out_a.index_map)
    return [in_a, in_b], out, jax.ShapeDtypeStruct((r, 2 * c), BF16)


def _cast_interleaved(a_ref, b_ref, dst_ref, chunk):
    for k in range(a_ref.shape[1] // chunk):
        dst_ref[:, 2 * k * chunk:(2 * k + 1) * chunk] = a_ref[:, k * chunk:(k + 1) * chunk].astype(dst_ref.dtype)
        dst_ref[:, (2 * k + 1) * chunk:(2 * k + 2) * chunk] = b_ref[:, k * chunk:(k + 1) * chunk].astype(dst_ref.dtype)


def _cast_slabs(src_refs, dst_refs):
    for src_ref, dst_ref in zip(src_refs, dst_refs):
        dst_ref[...] = src_ref[...].astype(dst_ref.dtype)


def _in_proj_kernel(x_ref, g_ref, w_ref, ang_ref, sign_ref, z_ref, h_ref, cos_ref, sin_ref, *, tn, n_col_tiles):
    j = pl.program_id(1)

    def tile(col0):
        first = col0 == (n_col_tiles - 1) * tn
        if first:
            h_ref[...] = _rms_rows(x_ref[...], g_ref[...]).astype(BF16)
        acc = jnp.dot(h_ref[...], w_ref[...], preferred_element_type=F32)
        if first:
            cos_ref[...] = jnp.cos(ang_ref[...])
            sin_ref[...] = jnp.sin(ang_ref[...]) * sign_ref[...]
        cosf, sinf = cos_ref[...], sin_ref[...]
        for c in range(tn // HEAD_DIM):
            col = col0 + c * HEAD_DIM
            r = acc[:, c * HEAD_DIM:(c + 1) * HEAD_DIM]
            if col < Q_OFF:
                r = _gelu(r)
            elif col < V_OFF:
                r = r * cosf + pltpu.roll(r, HEAD_DIM // 2, 1) * sinf
                if col < K_OFF:
                    r = r * (LOG2_E / math.sqrt(HEAD_DIM))
            z_ref[:, c * HEAD_DIM:(c + 1) * HEAD_DIM] = r.astype(z_ref.dtype)

    for jt in range(n_col_tiles):
        pl.when(j == n_col_tiles - 1 - jt)(functools.partial(tile, jt * tn))


def _in_proj(x2, gain, w, layer, ang, sign):
    m, d = x2.shape
    n = w.shape[2]
    tm, tn = IN_PROJ_TM, IN_PROJ_TN
    assert m % tm == 0 and n % tn == 0 and tn % HEAD_DIM == 0
    return pl.pallas_call(
        functools.partial(_in_proj_kernel, tn=tn, n_col_tiles=n // tn),
        grid=(m // tm, n // tn),
        in_specs=[
            pl.BlockSpec((tm, d), lambda i, j: (i, 0)),
            pl.BlockSpec((1, d), lambda i, j: (0, 0)),
            pl.BlockSpec((None, d, tn), lambda i, j: (layer, 0, n // tn - 1 - j),
                         pipeline_mode=pl.Buffered(1) if n == tn else None),
            pl.BlockSpec((tm, HEAD_DIM), lambda i, j: (i, 0)),
            pl.BlockSpec((1, HEAD_DIM), lambda i, j: (0, 0)),
        ],
        out_specs=pl.BlockSpec((tm, tn), lambda i, j: (i, n // tn - 1 - j)),
        out_shape=jax.ShapeDtypeStruct((m, n), BF16),
        scratch_shapes=[pltpu.VMEM((tm, d), BF16),
                        pltpu.VMEM((tm, HEAD_DIM), F32),
                        pltpu.VMEM((tm, HEAD_DIM), F32)],
        compiler_params=_params("parallel", "arbitrary"),
        name="in_proj",
    )(x2, gain, w, ang, sign)


def _gmlp_mixer(za_ref, lng_ref, lnb_ref, ws_ref, bs_ref, ga_ref, vn_ref, mixed_ref, out_ref, rows):
    groups = [slice(g * HEAD_DIM, (g + 1) * HEAD_DIM) for g in range(N_GROUPS)]
    for g, cols in enumerate(groups):
        v = za_ref[:, GMLP_WIDTH + cols.start:GMLP_WIDTH + cols.stop].astype(F32)
        mu = jnp.mean(v, axis=-1, keepdims=True)
        vc = v - mu
        var = jnp.mean(vc * vc, axis=-1, keepdims=True)
        vn = vc * lax.rsqrt(var + LN_EPS) * lng_ref[g:g + 1, :] + lnb_ref[g:g + 1, :]
        vn_ref[:, cols] = vn.astype(BF16)
    t_idx = lax.broadcasted_iota(jnp.int32, (GMLP_CHUNK, GMLP_CHUNK), 0)
    s_idx = lax.broadcasted_iota(jnp.int32, (GMLP_CHUNK, GMLP_CHUNK), 1)
    for g, cols in enumerate(groups):
        w = jnp.where(s_idx <= t_idx, ws_ref[g], 0.0).astype(BF16)
        for c in range(rows // GMLP_CHUNK):
            chunk = slice(c * GMLP_CHUNK, (c + 1) * GMLP_CHUNK)
            mixed_ref[chunk, cols] = jnp.dot(w, vn_ref[chunk, cols], preferred_element_type=F32)
    for g, cols in enumerate(groups):
        u = za_ref[:, cols].astype(F32)
        ya = _group_rms(u * (mixed_ref[:, cols] + bs_ref[g])) * ga_ref[:, cols]
        out_ref[:, cols] = ya.astype(out_ref.dtype)


def _short_conv_mixer(bg_ref, cg_ref, hc_ref, cgh_ref, hch_ref, cw_ref, gc_ref, out_ref, at_seq_start):
    xh = cg_ref[...].astype(F32) * hc_ref[...].astype(F32)
    halo = cgh_ref[...].astype(F32) * hch_ref[...].astype(F32)
    halo = jnp.where(at_seq_start, 0.0, halo)
    row = lax.broadcasted_iota(jnp.int32, xh.shape, 0)
    prev1 = jnp.where(row == 0, halo[CONV_HALO_ROWS - 1:CONV_HALO_ROWS, :], pltpu.roll(xh, 1, 0))
    prev2 = jnp.where(row == 0, halo[CONV_HALO_ROWS - 2:CONV_HALO_ROWS - 1, :],
                      jnp.where(row == 1, halo[CONV_HALO_ROWS - 1:CONV_HALO_ROWS, :], pltpu.roll(xh, 2, 0)))
    y = cw_ref[0:1, :] * prev2 + cw_ref[1:2, :] * prev1 + cw_ref[2:3, :] * xh
    yc = bg_ref[...].astype(F32) * y
    for g in range(N_GROUPS):
        cols = slice(g * HEAD_DIM, (g + 1) * HEAD_DIM)
        out_ref[:, cols] = (_group_rms(yc[:, cols]) * gc_ref[:, cols]).astype(out_ref.dtype)


def _mix_out_kernel(x_ref, za_ref, bg_ref, cg_ref, hc_ref, cgh_ref, hch_ref, mb_ref, lng_ref, lnb_ref, ws_ref,
                    bs_ref, cw_ref, ga_ref, gc_ref, w0_ref, w1_ref, w2_ref, w3_ref, *rest, rows, blocks_per_seq,
                    n_cast):
    cast_src, (gate_ref, up_ref, o_ref), cast_dst = rest[:n_cast], rest[n_cast:n_cast + 3], rest[n_cast + 3:2 * n_cast + 3]
    gu_ref, vn_ref, mixed_ref, ma_ref, mc_ref = rest[2 * n_cast + 3:]
    _cast_slabs(cast_src, cast_dst)
    _cast_interleaved(gate_ref, up_ref, gu_ref, FFN_GU_CHUNK)
    o_ref[...] = (x_ref[...] + jnp.dot(mb_ref[:, :GMLP_WIDTH], w1_ref[...], preferred_element_type=F32)
                  + jnp.dot(mb_ref[:, GMLP_WIDTH:], w2_ref[...], preferred_element_type=F32))
    _gmlp_mixer(za_ref, lng_ref, lnb_ref, ws_ref, bs_ref, ga_ref, vn_ref, mixed_ref, ma_ref, rows)
    _short_conv_mixer(bg_ref, cg_ref, hc_ref, cgh_ref, hch_ref, cw_ref, gc_ref, mc_ref,
                      pl.program_id(0) % blocks_per_seq == 0)
    o_ref[...] += (jnp.dot(ma_ref[...], w0_ref[...], preferred_element_type=F32)
                   + jnp.dot(mc_ref[...], w3_ref[...], preferred_element_type=F32))


def _mix_out_proj(x2, z, mix_b, ln_g, ln_b, ws, bs, conv_w, gain_a, gain_c, w, layer, seq_len, cast_weights,
                  gate_up):
    m, d = x2.shape
    rows = OUT_PROJ_TM
    assert m % rows == 0 and seq_len % rows == 0 and rows % GMLP_CHUNK == 0 and C_OFF % CONV_WIDTH == 0
    assert DIFF_WIDTH == 2 * GMLP_WIDTH and GMLP_WIDTH == CONV_WIDTH
    cb = C_OFF // CONV_WIDTH
    hb = rows // CONV_HALO_ROWS
    full = lambda shape: pl.BlockSpec(shape, lambda i: (0,) * len(shape))
    zcol = lambda width, col: pl.BlockSpec((rows, width), lambda i: (i, col))
    halo = lambda col: pl.BlockSpec((CONV_HALO_ROWS, CONV_WIDTH), lambda i: (jnp.maximum(i * hb - 1, 0), col))
    wrows = lambda blk: pl.BlockSpec((None, GMLP_WIDTH, d), lambda i: (layer, blk, 0))
    cast_in, cast_out, cast_shapes = _cast_specs(cast_weights, m // rows, lambda i: i, 1)
    gu_in, gu_out, gu_shape = _interleaved_pair_specs(*gate_up, m // rows, lambda i: i, 1)
    outs = pl.pallas_call(
        functools.partial(_mix_out_kernel, rows=rows, blocks_per_seq=seq_len // rows, n_cast=len(cast_weights)),
        grid=(m // rows,),
        in_specs=[
            pl.BlockSpec((rows, d), lambda i: (i, 0)),
            zcol(2 * GMLP_WIDTH, 0), zcol(CONV_WIDTH, cb), zcol(CONV_WIDTH, cb + 1), zcol(CONV_WIDTH, cb + 2),
            halo(cb + 1), halo(cb + 2),
            pl.BlockSpec((rows, DIFF_WIDTH), lambda i: (i, 0)),
            full((N_GROUPS, HEAD_DIM)), full((N_GROUPS, HEAD_DIM)),
            full((N_GROUPS, GMLP_CHUNK, GMLP_CHUNK)), full((N_GROUPS, rows, 1)),
            full((CONV_K, CONV_WIDTH)), full((1, GMLP_WIDTH)), full((1, CONV_WIDTH)),
            wrows(0), wrows(1), wrows(2), wrows(3),
        ] + cast_in + gu_in,
        out_specs=[pl.BlockSpec((rows, d), lambda i: (i, 0))] + cast_out + [gu_out],
        out_shape=[jax.ShapeDtypeStruct((m, d), F32)] + cast_shapes + [gu_shape],
        scratch_shapes=[pltpu.VMEM((rows, GMLP_WIDTH), BF16),
                        pltpu.VMEM((rows, GMLP_WIDTH), F32),
                        pltpu.VMEM((rows, GMLP_WIDTH), BF16),
                        pltpu.VMEM((rows, CONV_WIDTH), BF16)],
        compiler_params=_params("arbitrary"),
        name="mix_out_proj",
    )(x2, z, z, z, z, z, z, mix_b, ln_g, ln_b, ws, bs, conv_w, gain_a, gain_c, w, w, w, w, *[cw for cw, _ in cast_weights],
      gate_up[0], gate_up[1])
    return outs[0], outs[1:-1], outs[-1]


def _attn_kernel(lq1_ref, lk1_ref, lq2_ref, lk2_ref, q_ref, k_ref, v_ref, gain_ref, o_ref, m_ref, l_ref, acc_ref,
                 *bufs, tq, tk, n_q, n_buf, lam_init):
    rc = ATTN_ROW_CHUNK
    n_tiles = tk // HEAD_DIM
    sub = tq // tk
    p_bufs, a_bufs = bufs[:n_buf], bufs[n_buf:]

    def key_rows(j):
        return slice(j * tk, (j + 1) * tk)

    def scores(j, row_lo):
        return [lax.dot_general(q_ref[0, row_lo:, mp * HEAD_DIM:(mp + 1) * HEAD_DIM],
                                k_ref[0, key_rows(j), mp * HEAD_DIM:(mp + 1) * HEAD_DIM],
                                (((1,), (1,)), ((), ())), preferred_element_type=F32) for mp in range(2)]

    def softmax(s_maps, buf, k0, row_lo, first):
        p_ref, alpha_ref = p_bufs[buf], a_bufs[buf]
        live_rows = tq - row_lo
        for mp in range(2):
            for c in range(live_rows // rc):
                q0 = row_lo + c * rc
                rows = slice(mp * tq + q0, mp * tq + q0 + rc)
                p_rows = slice(mp * live_rows + c * rc, mp * live_rows + (c + 1) * rc)
                live = [t for t in range(n_tiles) if k0 is None or k0 + t * HEAD_DIM <= q0 + rc - 1]
                tiles = []
                for t in live:
                    st = s_maps[mp][c * rc:(c + 1) * rc, t * HEAD_DIM:(t + 1) * HEAD_DIM]
                    if k0 is not None and k0 + (t + 1) * HEAD_DIM - 1 > q0:
                        row = lax.broadcasted_iota(jnp.int32, (rc, HEAD_DIM), 0) + q0
                        col = lax.broadcasted_iota(jnp.int32, (rc, HEAD_DIM), 1) + (k0 + t * HEAD_DIM)
                        st = jnp.where(col <= row, st, -jnp.inf)
                    tiles.append(st)
                m_blk = jnp.max(functools.reduce(jnp.maximum, tiles), axis=-1, keepdims=True)
                if first:
                    m_new = jnp.broadcast_to(m_blk, (rc, HEAD_DIM))
                    ps = [jnp.exp2(st - m_new) for st in tiles]
                    l_ref[rows, :] = functools.reduce(jnp.add, ps)
                else:
                    m_old = m_ref[rows, :]
                    m_new = jnp.maximum(m_old, m_blk)
                    alpha = jnp.exp2(m_old - m_new)
                    ps = [jnp.exp2(st - m_new) for st in tiles]
                    l_ref[rows, :] = alpha * l_ref[rows, :] + functools.reduce(jnp.add, ps)
                    alpha_ref[p_rows, :] = alpha
                m_ref[rows, :] = m_new
                for t in range(n_tiles):
                    p_t = ps[live.index(t)].astype(BF16) if t in live else jnp.zeros((rc, HEAD_DIM), BF16)
                    p_ref[p_rows, t * HEAD_DIM:(t + 1) * HEAD_DIM] = p_t

    def accumulate(j, buf, row_lo, first):
        live_rows = tq - row_lo
        pv = jnp.dot(p_bufs[buf][:2 * live_rows, :], v_ref[0, key_rows(j), :],
                     preferred_element_type=F32)
        for mp in range(2):
            rows = slice(mp * tq + row_lo, (mp + 1) * tq)
            c_rows = slice(mp * live_rows, (mp + 1) * live_rows)
            if first:
                acc_ref[rows, :] = pv[c_rows, :]
            else:
                alpha = a_bufs[buf][c_rows, :]
                for t in range(DIFF_V_DIM // HEAD_DIM):
                    cols = slice(t * HEAD_DIM, (t + 1) * HEAD_DIM)
                    acc_ref[rows, cols] = acc_ref[rows, cols] * alpha + pv[c_rows, cols]

    def query_block(qi):
        n_keys = (qi + 1) * sub
        k0 = [None if j < qi * sub else (j - qi * sub) * tk for j in range(n_keys)]
        row_lo = [0 if k is None else k for k in k0]
        s_next = scores(0, row_lo[0])
        for j in range(n_keys):
            s = s_next
            if j + 1 < n_keys:
                s_next = scores(j + 1, row_lo[j + 1])
            if j >= 1:
                accumulate(j - 1, (j - 1) % n_buf, row_lo[j - 1], first=(j == 1))
            softmax(s, j % n_buf, k0[j], row_lo[j], first=(j == 0))
        accumulate(n_keys - 1, (n_keys - 1) % n_buf, row_lo[n_keys - 1], first=(n_keys == 1))

        lam = (jnp.exp(jnp.sum(lq1_ref[...] * lk1_ref[...], axis=-1, keepdims=True))
               - jnp.exp(jnp.sum(lq2_ref[...] * lk2_ref[...], axis=-1, keepdims=True)) + lam_init)
        o = acc_ref[...] * (1.0 / jnp.sum(l_ref[...], axis=-1, keepdims=True))
        o = o[:tq] - lam * o[tq:]
        o = _group_rms(o) * (1.0 - lam_init) * gain_ref[0]
        o_ref[0] = o.astype(o_ref.dtype)

    for qi in range(n_q):
        pl.when(pl.program_id(2) == qi)(functools.partial(query_block, qi))


def _diff_attention(z3, lq1, lk1, lq2, lk2, gain_b, lam_init):
    b, s, _ = z3.shape
    tq, tk = ATTN_TQ, ATTN_TK
    assert s % tq == 0 and tq % tk == 0
    n_buf = s // tk
    qb, kb, vb = Q_OFF // DIFF_V_DIM, K_OFF // DIFF_V_DIM, V_OFF // DIFF_V_DIM
    vec = pl.BlockSpec((1, HEAD_DIM), lambda bi, h, i: (0, 0))
    return pl.pallas_call(
        functools.partial(_attn_kernel, tq=tq, tk=tk, n_q=s // tq, n_buf=n_buf, lam_init=lam_init),
        grid=(b, N_GROUPS, s // tq),
        in_specs=[
            vec, vec, vec, vec,
            pl.BlockSpec((1, tq, DIFF_V_DIM), lambda bi, h, i: (bi, i, qb + h)),
            pl.BlockSpec((1, s, DIFF_V_DIM), lambda bi, h, i: (bi, 0, kb + h)),
            pl.BlockSpec((1, s, DIFF_V_DIM), lambda bi, h, i: (bi, 0, vb + h)),
            pl.BlockSpec((1, 1, DIFF_V_DIM), lambda bi, h, i: (h, 0, 0)),
        ],
        out_specs=pl.BlockSpec((1, tq, DIFF_V_DIM), lambda bi, h, i: (bi, i, h)),
        out_shape=jax.ShapeDtypeStruct((b, s, DIFF_WIDTH), BF16),
        scratch_shapes=[
            pltpu.VMEM((2 * tq, HEAD_DIM), F32),
            pltpu.VMEM((2 * tq, HEAD_DIM), F32),
            pltpu.VMEM((2 * tq, DIFF_V_DIM), F32),
        ] + [pltpu.VMEM((2 * tq, tk), BF16)] * n_buf
          + [pltpu.VMEM((2 * tq, HEAD_DIM), F32)] * n_buf,
        compiler_params=_params("parallel", "parallel", "arbitrary"),
        name="diff_attention",
    )(lq1, lk1, lq2, lk2, z3, z3, z3, gain_b)


def _ffn_kernel(x_ref, g_ref, wgu_ref, wd_ref, fg_ref, o_ref, h_ref, *, final_norm):
    f = pl.program_id(1)
    last_f = pl.num_programs(1) - 1

    def step(first, last):
        if first:
            h_ref[...] = _rms_rows(x_ref[...], g_ref[...]).astype(BF16)
        gu = jnp.dot(h_ref[...], wgu_ref[...], preferred_element_type=F32)
        ck = FFN_GU_CHUNK
        acts = []
        for k in range(gu.shape[1] // (2 * ck)):
            gate, up = gu[:, 2 * k * ck:(2 * k + 1) * ck], gu[:, (2 * k + 1) * ck:(2 * k + 2) * ck]
            acts.append((gate * jax.nn.sigmoid(gate) * up).astype(BF16))
        act = jnp.concatenate(acts, axis=1)
        out = (x_ref[...] if first else o_ref[...]) + jnp.dot(act, wd_ref[...], preferred_element_type=F32)
        if last and final_norm:
            out = _rms_rows(out, fg_ref[...])
        o_ref[...] = out

    pl.when(f == 0)(functools.partial(step, True, False))
    if final_norm:
        pl.when((f > 0) & (f < last_f))(functools.partial(step, False, False))
        pl.when(f == last_f)(functools.partial(step, False, True))
    else:
        pl.when(f > 0)(functools.partial(step, False, False))


def _ffn(x2, gain, wgu, wd, layer, final_gain, final_norm):
    m, d = x2.shape
    dff = wd.shape[1]
    tm, tf = FFN_TM, FFN_TF
    assert m % tm == 0 and dff % tf == 0 and tf % FFN_GU_CHUNK == 0
    return pl.pallas_call(
        functools.partial(_ffn_kernel, final_norm=final_norm),
        grid=(m // tm, dff // tf),
        in_specs=[
            pl.BlockSpec((tm, d), lambda i, f: (i, 0)),
            pl.BlockSpec((1, d), lambda i, f: (0, 0)),
            pl.BlockSpec((None, d, 2 * tf), lambda i, f: (layer, 0, f)),
            pl.BlockSpec((None, tf, d), lambda i, f: (layer, f, 0)),
            pl.BlockSpec((1, d), lambda i, f: (0, 0)),
        ],
        out_specs=pl.BlockSpec((tm, d), lambda i, f: (i, 0)),
        out_shape=jax.ShapeDtypeStruct((m, d), F32),
        scratch_shapes=[pltpu.VMEM((tm, d), BF16)],
        compiler_params=_params("parallel", "arbitrary"),
        name="ffn",
    )(x2, gain, wgu, wd, final_gain)


def kernel(x, positions, attn_norm, w_in, gmlp_ln_g, gmlp_ln_b, gmlp_ws, gmlp_bs, lambda_q1, lambda_k1,
           lambda_q2, lambda_k2, conv_w, mix_norm, w_out, ffn_norm, w_gate, w_up, w_down, final_norm):
    b, s, d = x.shape
    m = b * s
    depth = w_in.shape[0]

    inv_freq = 1.0 / (ROPE_THETA ** (jnp.arange(0, HEAD_DIM, 2, dtype=F32) / HEAD_DIM))
    ang = positions.astype(F32).reshape(m, 1) * jnp.concatenate([inv_freq, inv_freq]).reshape(1, HEAD_DIM)
    sign = jnp.concatenate([-jnp.ones((HEAD_DIM // 2,), F32), jnp.ones((HEAD_DIM // 2,), F32)]).reshape(1, HEAD_DIM)

    w_in_l, w_out_l = w_in[0].astype(BF16), w_out[0].astype(BF16)
    x2 = x.reshape(m, d)
    for l in range(depth):
        lam_init = 0.8 - 0.6 * math.exp(-0.3 * l)
        gain = mix_norm[l]
        z = _in_proj(x2, attn_norm[l].reshape(1, d), w_in_l[None], 0, ang, sign)
        z3 = z.reshape(b, s, z.shape[1])
        mix_b = _diff_attention(
            z3, lambda_q1[l].reshape(1, HEAD_DIM), lambda_k1[l].reshape(1, HEAD_DIM),
            lambda_q2[l].reshape(1, HEAD_DIM), lambda_k2[l].reshape(1, HEAD_DIM),
            gain[GMLP_WIDTH:GMLP_WIDTH + DIFF_WIDTH].reshape(N_GROUPS, 1, DIFF_V_DIM), lam_init)
        next_proj = [(w_in, l + 1), (w_out, l + 1)] if l + 1 < depth else []
        x2, (w_down_l, *casts), w_gu_l = _mix_out_proj(
            x2, z, mix_b.reshape(m, DIFF_WIDTH), gmlp_ln_g[l], gmlp_ln_b[l], gmlp_ws[l],
            jnp.tile(gmlp_bs[l], (1, OUT_PROJ_TM // GMLP_CHUNK)).reshape(N_GROUPS, OUT_PROJ_TM, 1), conv_w[l],
            gain[:GMLP_WIDTH].reshape(1, GMLP_WIDTH), gain[GMLP_WIDTH + DIFF_WIDTH:].reshape(1, CONV_WIDTH),
            w_out_l[None], 0, s, [(w_down, l)] + next_proj, (w_gate, w_up, l))
        x2 = _ffn(x2, ffn_norm[l].reshape(1, d), w_gu_l[None], w_down_l[None], 0,
                  final_norm.reshape(1, d), final_norm=(l == depth - 1))
        if l + 1 < depth:
            w_in_l, w_out_l = casts
    return x2.reshape(b, s, d)
```

```python
import functools
import math

import jax
import jax.numpy as jnp
from jax import lax
from jax.experimental import pallas as pl
from jax.experimental.pallas import tpu as pltpu

F32 = jnp.float32
BF16 = jnp.bfloat16

HEAD_DIM = 128
N_GROUPS = 4
GMLP_WIDTH = N_GROUPS * HEAD_DIM
GMLP_CHUNK = 128
DIFF_QK_WIDTH = N_GROUPS * 2 * HEAD_DIM
DIFF_V_DIM = 2 * HEAD_DIM
DIFF_WIDTH = N_GROUPS * DIFF_V_DIM
CONV_WIDTH = N_GROUPS * HEAD_DIM
CONV_K = 3
ROPE_THETA = 10000.0
RMS_EPS = 1e-6
LN_EPS = 1e-5
LOG2_E = math.log2(math.e)

Q_OFF = 2 * GMLP_WIDTH
K_OFF = Q_OFF + DIFF_QK_WIDTH
V_OFF = K_OFF + DIFF_QK_WIDTH
C_OFF = V_OFF + DIFF_WIDTH

VMEM_LIMIT_BYTES = 56 * 1024 * 1024

IN_PROJ_TM = 512
IN_PROJ_TN = 5632
ATTN_TQ = 1024
ATTN_TK = 512
CAST_SLAB_ROWS = 16
ATTN_ROW_CHUNK = 32
OUT_PROJ_TM = 512
FFN_TM = 1024
FFN_TF = 512
FFN_GU_CHUNK = 256
CONV_HALO_ROWS = 16


def _params(*semantics):
    return pltpu.CompilerParams(dimension_semantics=semantics, vmem_limit_bytes=VMEM_LIMIT_BYTES)


def _rms_rows(x, gain):
    return x * lax.rsqrt(jnp.mean(x * x, axis=-1, keepdims=True) + RMS_EPS) * gain


def _group_rms(x):
    return x * lax.rsqrt(jnp.mean(x * x, axis=-1, keepdims=True) + RMS_EPS)


def _gelu(x):
    return 0.5 * x * (1.0 + lax.erf(x * math.sqrt(0.5)))


def _cast_specs(cast_weights, n_steps, step_of, n_grid_axes):
    in_specs, out_specs, out_shapes = [], [], []
    for w, layer in cast_weights:
        _, r, c = w.shape
        slab = next(k for k in range(CAST_SLAB_ROWS, r + 1, CAST_SLAB_ROWS) if r % k == 0 and r // k <= n_steps)
        blk = lambda *idx, last=r // slab - 1: jnp.minimum(step_of(*idx[:n_grid_axes]), last)
        in_specs.append(pl.BlockSpec((None, slab, c), lambda *idx, layer=layer, blk=blk: (layer, blk(*idx), 0)))
        out_specs.append(pl.BlockSpec((slab, c), lambda *idx, blk=blk: (blk(*idx), 0)))
        out_shapes.append(jax.ShapeDtypeStruct((r, c), BF16))
    return in_specs, out_specs, out_shapes


def _interleaved_pair_specs(w_a, w_b, layer, n_steps, step_of, n_grid_axes):
    (in_a,), (out_a,), _ = _cast_specs([(w_a, layer)], n_steps, step_of, n_grid_axes)
    (in_b,), _, _ = _cast_specs([(w_b, layer)], n_steps, step_of, n_grid_axes)
    _, r, c = w_a.shape
    slab = out_a.block_shape[0]
    out = pl.BlockSpec((slab, 2 * c), out_a.index_map)
    return [in_a, in_b], out, jax.ShapeDtypeStruct((r, 2 * c), BF16)


def _cast_interleaved(a_ref, b_ref, dst_ref, chunk):
    for k in range(a_ref.shape[1] // chunk):
        dst_ref[:, 2 * k * chunk:(2 * k + 1) * chunk] = a_ref[:, k * chunk:(k + 1) * chunk].astype(dst_ref.dtype)
        dst_ref[:, (2 * k + 1) * chunk:(2 * k + 2) * chunk] = b_ref[:, k * chunk:(k + 1) * chunk].astype(dst_ref.dtype)


def _cast_slabs(src_refs, dst_refs):
    for src_ref, dst_ref in zip(src_refs, dst_refs):
        dst_ref[...] = src_ref[...].astype(dst_ref.dtype)


def _in_proj_kernel(x_ref, g_ref, w_ref, ang_ref, sign_ref, z_ref, h_ref, cos_ref, sin_ref, *, tn, n_col_tiles):
    j = pl.program_id(1)

    def tile(col0):
        first = col0 == (n_col_tiles - 1) * tn
        if first:
            h_ref[...] = _rms_rows(x_ref[...], g_ref[...]).astype(BF16)
        acc = jnp.dot(h_ref[...], w_ref[...], preferred_element_type=F32)
        if first:
            cos_ref[...] = jnp.cos(ang_ref[...])
            sin_ref[...] = jnp.sin(ang_ref[...]) * sign_ref[...]
        cosf, sinf = cos_ref[...], sin_ref[...]
        for c in range(tn // HEAD_DIM):
            col = col0 + c * HEAD_DIM
            r = acc[:, c * HEAD_DIM:(c + 1) * HEAD_DIM]
            if col < Q_OFF:
                r = _gelu(r)
            elif col < V_OFF:
                r = r * cosf + pltpu.roll(r, HEAD_DIM // 2, 1) * sinf
                if col < K_OFF:
                    r = r * (LOG2_E / math.sqrt(HEAD_DIM))
            z_ref[:, c * HEAD_DIM:(c + 1) * HEAD_DIM] = r.astype(z_ref.dtype)

    for jt in range(n_col_tiles):
        pl.when(j == n_col_tiles - 1 - jt)(functools.partial(tile, jt * tn))


def _in_proj(x2, gain, w, layer, ang, sign):
    m, d = x2.shape
    n = w.shape[2]
    tm, tn = IN_PROJ_TM, IN_PROJ_TN
    assert m % tm == 0 and n % tn == 0 and tn % HEAD_DIM == 0
    return pl.pallas_call(
        functools.partial(_in_proj_kernel, tn=tn, n_col_tiles=n // tn),
        grid=(m // tm, n // tn),
        in_specs=[
            pl.BlockSpec((tm, d), lambda i, j: (i, 0)),
            pl.BlockSpec((1, d), lambda i, j: (0, 0)),
            pl.BlockSpec((None, d, tn), lambda i, j: (layer, 0, n // tn - 1 - j),
                         pipeline_mode=pl.Buffered(1) if n == tn else None),
            pl.BlockSpec((tm, HEAD_DIM), lambda i, j: (i, 0)),
            pl.BlockSpec((1, HEAD_DIM), lambda i, j: (0, 0)),
        ],
        out_specs=pl.BlockSpec((tm, tn), lambda i, j: (i, n // tn - 1 - j)),
        out_shape=jax.ShapeDtypeStruct((m, n), BF16),
        scratch_shapes=[pltpu.VMEM((tm, d), BF16),
                        pltpu.VMEM((tm, HEAD_DIM), F32),
                        pltpu.VMEM((tm, HEAD_DIM), F32)],
        compiler_params=_params("parallel", "arbitrary"),
        name="in_proj",
    )(x2, gain, w, ang, sign)


def _gmlp_mixer(za_ref, lng_ref, lnb_ref, ws_ref, bs_ref, ga_ref, vn_ref, mixed_ref, out_ref, rows):
    groups = [slice(g * HEAD_DIM, (g + 1) * HEAD_DIM) for g in range(N_GROUPS)]
    for g, cols in enumerate(groups):
        v = za_ref[:, GMLP_WIDTH + cols.start:GMLP_WIDTH + cols.stop].astype(F32)
        mu = jnp.mean(v, axis=-1, keepdims=True)
        vc = v - mu
        var = jnp.mean(vc * vc, axis=-1, keepdims=True)
        vn = vc * lax.rsqrt(var + LN_EPS) * lng_ref[g:g + 1, :] + lnb_ref[g:g + 1, :]
        vn_ref[:, cols] = vn.astype(BF16)
    t_idx = lax.broadcasted_iota(jnp.int32, (GMLP_CHUNK, GMLP_CHUNK), 0)
    s_idx = lax.broadcasted_iota(jnp.int32, (GMLP_CHUNK, GMLP_CHUNK), 1)
    for g, cols in enumerate(groups):
        w = jnp.where(s_idx <= t_idx, ws_ref[g], 0.0).astype(BF16)
        for c in range(rows // GMLP_CHUNK):
            chunk = slice(c * GMLP_CHUNK, (c + 1) * GMLP_CHUNK)
            mixed_ref[chunk, cols] = jnp.dot(w, vn_ref[chunk, cols], preferred_element_type=F32)
    for g, cols in enumerate(groups):
        u = za_ref[:, cols].astype(F32)
        ya = _group_rms(u * (mixed_ref[:, cols] + bs_ref[g])) * ga_ref[:, cols]
        out_ref[:, cols] = ya.astype(out_ref.dtype)


def _short_conv_mixer(bg_ref, cg_ref, hc_ref, cgh_ref, hch_ref, cw_ref, gc_ref, out_ref, at_seq_start):
    xh = cg_ref[...].astype(F32) * hc_ref[...].astype(F32)
    halo = cgh_ref[...].astype(F32) * hch_ref[...].astype(F32)
    halo = jnp.where(at_seq_start, 0.0, halo)
    row = lax.broadcasted_iota(jnp.int32, xh.shape, 0)
    prev1 = jnp.where(row == 0, halo[CONV_HALO_ROWS - 1:CONV_HALO_ROWS, :], pltpu.roll(xh, 1, 0))
    prev2 = jnp.where(row == 0, halo[CONV_HALO_ROWS - 2:CONV_HALO_ROWS - 1, :],
                      jnp.where(row == 1, halo[CONV_HALO_ROWS - 1:CONV_HALO_ROWS, :], pltpu.roll(xh, 2, 0)))
    y = cw_ref[0:1, :] * prev2 + cw_ref[1:2, :] * prev1 + cw_ref[2:3, :] * xh
    yc = bg_ref[...].astype(F32) * y
    for g in range(N_GROUPS):
        cols = slice(g * HEAD_DIM, (g + 1) * HEAD_DIM)
        out_ref[:, cols] = (_group_rms(yc[:, cols]) * gc_ref[:, cols]).astype(out_ref.dtype)


def _mix_out_kernel(x_ref, za_ref, bg_ref, cg_ref, hc_ref, cgh_ref, hch_ref, mb_ref, lng_ref, lnb_ref, ws_ref,
                    bs_ref, cw_ref, ga_ref, gc_ref, w_ref, *rest, rows, blocks_per_seq, n_cast):
    cast_src, (gate_ref, up_ref, o_ref), cast_dst = rest[:n_cast], rest[n_cast:n_cast + 3], rest[n_cast + 3:2 * n_cast + 3]
    gu_ref, vn_ref, mixed_ref, ma_ref, mc_ref = rest[2 * n_cast + 3:]
    _cast_slabs(cast_src, cast_dst)
    _cast_interleaved(gate_ref, up_ref, gu_ref, FFN_GU_CHUNK)
    b_lo, c_lo = GMLP_WIDTH, GMLP_WIDTH + DIFF_WIDTH
    o_ref[...] = x_ref[...] + jnp.dot(mb_ref[...], w_ref[b_lo:c_lo, :], preferred_element_type=F32)
    _gmlp_mixer(za_ref, lng_ref, lnb_ref, ws_ref, bs_ref, ga_ref, vn_ref, mixed_ref, ma_ref, rows)
    _short_conv_mixer(bg_ref, cg_ref, hc_ref, cgh_ref, hch_ref, cw_ref, gc_ref, mc_ref,
                      pl.program_id(0) % blocks_per_seq == 0)
    o_ref[...] += (jnp.dot(ma_ref[...], w_ref[:b_lo, :], preferred_element_type=F32)
                   + jnp.dot(mc_ref[...], w_ref[c_lo:, :], preferred_element_type=F32))


def _mix_out_proj(x2, z, mix_b, ln_g, ln_b, ws, bs, conv_w, gain_a, gain_c, w, layer, seq_len, cast_weights,
                  gate_up):
    m, d = x2.shape
    rows = OUT_PROJ_TM
    assert m % rows == 0 and seq_len % rows == 0 and rows % GMLP_CHUNK == 0 and C_OFF % CONV_WIDTH == 0
    assert w.shape[1] == GMLP_WIDTH + DIFF_WIDTH + CONV_WIDTH
    cb = C_OFF // CONV_WIDTH
    hb = rows // CONV_HALO_ROWS
    full = lambda shape: pl.BlockSpec(shape, lambda i: (0,) * len(shape))
    zcol = lambda width, col: pl.BlockSpec((rows, width), lambda i: (i, col))
    halo = lambda col: pl.BlockSpec((CONV_HALO_ROWS, CONV_WIDTH), lambda i: (jnp.maximum(i * hb - 1, 0), col))
    cast_in, cast_out, cast_shapes = _cast_specs(cast_weights, m // rows, lambda i: i, 1)
    gu_in, gu_out, gu_shape = _interleaved_pair_specs(*gate_up, m // rows, lambda i: i, 1)
    outs = pl.pallas_call(
        functools.partial(_mix_out_kernel, rows=rows, blocks_per_seq=seq_len // rows, n_cast=len(cast_weights)),
        grid=(m // rows,),
        in_specs=[
            pl.BlockSpec((rows, d), lambda i: (i, 0)),
            zcol(2 * GMLP_WIDTH, 0), zcol(CONV_WIDTH, cb), zcol(CONV_WIDTH, cb + 1), zcol(CONV_WIDTH, cb + 2),
            halo(cb + 1), halo(cb + 2),
            pl.BlockSpec((rows, DIFF_WIDTH), lambda i: (i, 0)),
            full((N_GROUPS, HEAD_DIM)), full((N_GROUPS, HEAD_DIM)),
            full((N_GROUPS, GMLP_CHUNK, GMLP_CHUNK)), full((N_GROUPS, rows, 1)),
            full((CONV_K, CONV_WIDTH)), full((1, GMLP_WIDTH)), full((1, CONV_WIDTH)),
            pl.BlockSpec((None,) + w.shape[1:], lambda i: (layer, 0, 0), pipeline_mode=pl.Buffered(1)),
        ] + cast_in + gu_in,
        out_specs=[pl.BlockSpec((rows, d), lambda i: (i, 0))] + cast_out + [gu_out],
        out_shape=[jax.ShapeDtypeStruct((m, d), F32)] + cast_shapes + [gu_shape],
        scratch_shapes=[pltpu.VMEM((rows, GMLP_WIDTH), BF16),
                        pltpu.VMEM((rows, GMLP_WIDTH), F32),
                        pltpu.VMEM((rows, GMLP_WIDTH), BF16),
                        pltpu.VMEM((rows, CONV_WIDTH), BF16)],
        compiler_params=_params("arbitrary"),
        name="mix_out_proj",
    )(x2, z, z, z, z, z, z, mix_b, ln_g, ln_b, ws, bs, conv_w, gain_a, gain_c, w, *[cw for cw, _ in cast_weights],
      gate_up[0], gate_up[1])
    return outs[0], outs[1:-1], outs[-1]


def _attn_kernel(lq1_ref, lk1_ref, lq2_ref, lk2_ref, q_ref, k_ref, v_ref, gain_ref, o_ref, m_ref, l_ref, acc_ref,
                 *bufs, tq, tk, n_q, n_buf, lam_init):
    rc = ATTN_ROW_CHUNK
    n_tiles = tk // HEAD_DIM
    sub = tq // tk
    p_bufs, a_bufs = bufs[:n_buf], bufs[n_buf:]

    def key_rows(j):
        return slice(j * tk, (j + 1) * tk)

    def scores(j, row_lo):
        return [lax.dot_general(q_ref[0, row_lo:, mp * HEAD_DIM:(mp + 1) * HEAD_DIM],
                                k_ref[0, key_rows(j), mp * HEAD_DIM:(mp + 1) * HEAD_DIM],
                                (((1,), (1,)), ((), ())), preferred_element_type=F32) for mp in range(2)]

    def softmax(s_maps, buf, k0, row_lo, first):
        p_ref, alpha_ref = p_bufs[buf], a_bufs[buf]
        live_rows = tq - row_lo
        for mp in range(2):
            for c in range(live_rows // rc):
                q0 = row_lo + c * rc
                rows = slice(mp * tq + q0, mp * tq + q0 + rc)
                p_rows = slice(mp * live_rows + c * rc, mp * live_rows + (c + 1) * rc)
                live = [t for t in range(n_tiles) if k0 is None or k0 + t * HEAD_DIM <= q0 + rc - 1]
                tiles = []
                for t in live:
                    st = s_maps[mp][c * rc:(c + 1) * rc, t * HEAD_DIM:(t + 1) * HEAD_DIM]
                    if k0 is not None and k0 + (t + 1) * HEAD_DIM - 1 > q0:
                        row = lax.broadcasted_iota(jnp.int32, (rc, HEAD_DIM), 0) + q0
                        col = lax.broadcasted_iota(jnp.int32, (rc, HEAD_DIM), 1) + (k0 + t * HEAD_DIM)
                        st = jnp.where(col <= row, st, -jnp.inf)
                    tiles.append(st)
                m_blk = jnp.max(functools.reduce(jnp.maximum, tiles), axis=-1, keepdims=True)
                if first:
                    m_new = jnp.broadcast_to(m_blk, (rc, HEAD_DIM))
                    ps = [jnp.exp2(st - m_new) for st in tiles]
                    l_ref[rows, :] = functools.reduce(jnp.add, ps)
                else:
                    m_old = m_ref[rows, :]
                    m_new = jnp.maximum(m_old, m_blk)
                    alpha = jnp.exp2(m_old - m_new)
                    ps = [jnp.exp2(st - m_new) for st in tiles]
                    l_ref[rows, :] = alpha * l_ref[rows, :] + functools.reduce(jnp.add, ps)
                    alpha_ref[p_rows, :] = alpha
                m_ref[rows, :] = m_new
                for t in range(n_tiles):
                    p_t = ps[live.index(t)].astype(BF16) if t in live else jnp.zeros((rc, HEAD_DIM), BF16)
                    p_ref[p_rows, t * HEAD_DIM:(t + 1) * HEAD_DIM] = p_t

    def accumulate(j, buf, row_lo, first):
        live_rows = tq - row_lo
        pv = jnp.dot(p_bufs[buf][:2 * live_rows, :], v_ref[0, key_rows(j), :],
                     preferred_element_type=F32)
        for mp in range(2):
            rows = slice(mp * tq + row_lo, (mp + 1) * tq)
            c_rows = slice(mp * live_rows, (mp + 1) * live_rows)
            if first:
                acc_ref[rows, :] = pv[c_rows, :]
            else:
                alpha = a_bufs[buf][c_rows, :]
                for t in range(DIFF_V_DIM // HEAD_DIM):
                    cols = slice(t * HEAD_DIM, (t + 1) * HEAD_DIM)
                    acc_ref[rows, cols] = acc_ref[rows, cols] * alpha + pv[c_rows, cols]

    def query_block(qi):
        n_keys = (qi + 1) * sub
        k0 = [None if j < qi * sub else (j - qi * sub) * tk for j in range(n_keys)]
        row_lo = [0 if k is None else k for k in k0]
        s_next = scores(0, row_lo[0])
        for j in range(n_keys):
            s = s_next
            if j + 1 < n_keys:
                s_next = scores(j + 1, row_lo[j + 1])
            if j >= 1:
                accumulate(j - 1, (j - 1) % n_buf, row_lo[j - 1], first=(j == 1))
            softmax(s, j % n_buf, k0[j], row_lo[j], first=(j == 0))
        accumulate(n_keys - 1, (n_keys - 1) % n_buf, row_lo[n_keys - 1], first=(n_keys == 1))

        lam = (jnp.exp(jnp.sum(lq1_ref[...] * lk1_ref[...], axis=-1, keepdims=True))
               - jnp.exp(jnp.sum(lq2_ref[...] * lk2_ref[...], axis=-1, keepdims=True)) + lam_init)
        o = acc_ref[...] * (1.0 / jnp.sum(l_ref[...], axis=-1, keepdims=True))
        o = o[:tq] - lam * o[tq:]
        o = _group_rms(o) * (1.0 - lam_init) * gain_ref[0]
        o_ref[0] = o.astype(o_ref.dtype)

    for qi in range(n_q):
        pl.when(pl.program_id(2) == qi)(functools.partial(query_block, qi))


def _diff_attention(z3, lq1, lk1, lq2, lk2, gain_b, lam_init):
    b, s, _ = z3.shape
    tq, tk = ATTN_TQ, ATTN_TK
    assert s % tq == 0 and tq % tk == 0
    n_buf = s // tk
    qb, kb, vb = Q_OFF // DIFF_V_DIM, K_OFF // DIFF_V_DIM, V_OFF // DIFF_V_DIM
    vec = pl.BlockSpec((1, HEAD_DIM), lambda bi, h, i: (0, 0))
    return pl.pallas_call(
        functools.partial(_attn_kernel, tq=tq, tk=tk, n_q=s // tq, n_buf=n_buf, lam_init=lam_init),
        grid=(b, N_GROUPS, s // tq),
        in_specs=[
            vec, vec, vec, vec,
            pl.BlockSpec((1, tq, DIFF_V_DIM), lambda bi, h, i: (bi, i, qb + h)),
            pl.BlockSpec((1, s, DIFF_V_DIM), lambda bi, h, i: (bi, 0, kb + h)),
            pl.BlockSpec((1, s, DIFF_V_DIM), lambda bi, h, i: (bi, 0, vb + h)),
            pl.BlockSpec((1, 1, DIFF_V_DIM), lambda bi, h, i: (h, 0, 0)),
        ],
        out_specs=pl.BlockSpec((1, tq, DIFF_V_DIM), lambda bi, h, i: (bi, i, h)),
        out_shape=jax.ShapeDtypeStruct((b, s, DIFF_WIDTH), BF16),
        scratch_shapes=[
            pltpu.VMEM((2 * tq, HEAD_DIM), F32),
            pltpu.VMEM((2 * tq, HEAD_DIM), F32),
            pltpu.VMEM((2 * tq, DIFF_V_DIM), F32),
        ] + [pltpu.VMEM((2 * tq, tk), BF16)] * n_buf
          + [pltpu.VMEM((2 * tq, HEAD_DIM), F32)] * n_buf,
        compiler_params=_params("parallel", "parallel", "arbitrary"),
        name="diff_attention",
    )(lq1, lk1, lq2, lk2, z3, z3, z3, gain_b)


def _ffn_kernel(x_ref, g_ref, wgu_ref, wd_ref, fg_ref, o_ref, h_ref, *, final_norm):
    f = pl.program_id(1)
    last_f = pl.num_programs(1) - 1

    def step(first, last):
        if first:
            h_ref[...] = _rms_rows(x_ref[...], g_ref[...]).astype(BF16)
        gu = jnp.dot(h_ref[...], wgu_ref[...], preferred_element_type=F32)
        ck = FFN_GU_CHUNK
        acts = []
        for k in range(gu.shape[1] // (2 * ck)):
            gate, up = gu[:, 2 * k * ck:(2 * k + 1) * ck], gu[:, (2 * k + 1) * ck:(2 * k + 2) * ck]
            acts.append((gate * jax.nn.sigmoid(gate) * up).astype(BF16))
        act = jnp.concatenate(acts, axis=1)
        out = (x_ref[...] if first else o_ref[...]) + jnp.dot(act, wd_ref[...], preferred_element_type=F32)
        if last and final_norm:
            out = _rms_rows(out, fg_ref[...])
        o_ref[...] = out

    pl.when(f == 0)(functools.partial(step, True, False))
    if final_norm:
        pl.when((f > 0) & (f < last_f))(functools.partial(step, False, False))
        pl.when(f == last_f)(functools.partial(step, False, True))
    else:
        pl.when(f > 0)(functools.partial(step, False, False))


def _ffn(x2, gain, wgu, wd, layer, final_gain, final_norm):
    m, d = x2.shape
    dff = wd.shape[1]
    tm, tf = FFN_TM, FFN_TF
    assert m % tm == 0 and dff % tf == 0 and tf % FFN_GU_CHUNK == 0
    return pl.pallas_call(
        functools.partial(_ffn_kernel, final_norm=final_norm),
        grid=(m // tm, dff // tf),
        in_specs=[
            pl.BlockSpec((tm, d), lambda i, f: (i, 0)),
            pl.BlockSpec((1, d), lambda i, f: (0, 0)),
            pl.BlockSpec((None, d, 2 * tf), lambda i, f: (layer, 0, f)),
            pl.BlockSpec((None, tf, d), lambda i, f: (layer, f, 0)),
            pl.BlockSpec((1, d), lambda i, f: (0, 0)),
        ],
        out_specs=pl.BlockSpec((tm, d), lambda i, f: (i, 0)),
        out_shape=jax.ShapeDtypeStruct((m, d), F32),
        scratch_shapes=[pltpu.VMEM((tm, d), BF16)],
        compiler_params=_params("parallel", "arbitrary"),
        name="ffn",
    )(x2, gain, wgu, wd, final_gain)


def kernel(x, positions, attn_norm, w_in, gmlp_ln_g, gmlp_ln_b, gmlp_ws, gmlp_bs, lambda_q1, lambda_k1,
           lambda_q2, lambda_k2, conv_w, mix_norm, w_out, ffn_norm, w_gate, w_up, w_down, final_norm):
    b, s, d = x.shape
    m = b * s
    depth = w_in.shape[0]

    inv_freq = 1.0 / (ROPE_THETA ** (jnp.arange(0, HEAD_DIM, 2, dtype=F32) / HEAD_DIM))
    ang = positions.astype(F32).reshape(m, 1) * jnp.concatenate([inv_freq, inv_freq]).reshape(1, HEAD_DIM)
    sign = jnp.concatenate([-jnp.ones((HEAD_DIM // 2,), F32), jnp.ones((HEAD_DIM // 2,), F32)]).reshape(1, HEAD_DIM)

    w_in_l, w_out_l = w_in[0].astype(BF16), w_out[0].astype(BF16)
    x2 = x.reshape(m, d)
    for l in range(depth):
        lam_init = 0.8 - 0.6 * math.exp(-0.3 * l)
        gain = mix_norm[l]
        z = _in_proj(x2, attn_norm[l].reshape(1, d), w_in_l[None], 0, ang, sign)
        z3 = z.reshape(b, s, z.shape[1])
        mix_b = _diff_attention(
            z3, lambda_q1[l].reshape(1, HEAD_DIM), lambda_k1[l].reshape(1, HEAD_DIM),
            lambda_q2[l].reshape(1, HEAD_DIM), lambda_k2[l].reshape(1, HEAD_DIM),
            gain[GMLP_WIDTH:GMLP_WIDTH + DIFF_WIDTH].reshape(N_GROUPS, 1, DIFF_V_DIM), lam_init)
        next_proj = [(w_in, l + 1), (w_out, l + 1)] if l + 1 < depth else []
        x2, (w_down_l, *casts), w_gu_l = _mix_out_proj(
            x2, z, mix_b.reshape(m, DIFF_WIDTH), gmlp_ln_g[l], gmlp_ln_b[l], gmlp_ws[l],
            jnp.tile(gmlp_bs[l], (1, OUT_PROJ_TM // GMLP_CHUNK)).reshape(N_GROUPS, OUT_PROJ_TM, 1), conv_w[l],
            gain[:GMLP_WIDTH].reshape(1, GMLP_WIDTH), gain[GMLP_WIDTH + DIFF_WIDTH:].reshape(1, CONV_WIDTH),
            w_out_l[None], 0, s, [(w_down, l)] + next_proj, (w_gate, w_up, l))
        x2 = _ffn(x2, ffn_norm[l].reshape(1, d), w_gu_l[None], w_down_l[None], 0,
                  final_norm.reshape(1, d), final_norm=(l == depth - 1))
        if l + 1 < depth:
            w_in_l, w_out_l = casts
    return x2.reshape(b, s, d)
```
